```python
import math
import jax
import jax.numpy as jnp
from jax import lax
import numpy as np

D_MODEL = 1024
BATCH = 8
SEQ = 4096
DEPTH = 2

PLE_DIM = 256
HEAD_DIM = 64
RMS_EPS = 1e-6
SSM_WIDTH = D_MODEL // 2
SSM_GROUP = 16
SSM_GROUPS = SSM_WIDTH // SSM_GROUP
SSM_STATE = 64
DT_MIN = 1e-3
DT_MAX = 1e-1
MOBA_WIDTH = D_MODEL - SSM_WIDTH
MOBA_HEADS = MOBA_WIDTH // HEAD_DIM
MOBA_BLOCK = 256
MOBA_TOPK = 3
MOBA_QCHUNK = 32
REL_BUCKETS = 32
REL_MAX_DIST = 128
SB_HEADS = D_MODEL // HEAD_DIM
SB_QBLOCK = 128
N_EXPERTS = 32
TOP_K = 4
D_FF = D_MODEL
SWIGLU_LIMIT = 7.0
SWIGLU_ALPHA = 1.702
MOE_BLOCK = 128
N_EVEN = (DEPTH + 1) // 2
N_ODD = DEPTH // 2

kernel_name = 'hybrid_s5_moba_stickbreak_moe'


def rmsnorm(x, g):
    xf = x.astype(jnp.float32)
    y = xf * lax.rsqrt(jnp.mean(xf * xf, axis=-1, keepdims=True) + RMS_EPS)
    return (y * g.astype(jnp.float32)).astype(x.dtype)


def split_heads(t, n_heads):
    b, l, _ = t.shape
    return t.reshape(b, l, n_heads, HEAD_DIM).transpose(0, 2, 1, 3)


def merge_heads(t):
    b, h, l, d = t.shape
    return t.transpose(0, 2, 1, 3).reshape(b, l, h * d)


def rel_bucket(dist):
    exact = REL_BUCKETS // 2
    n = jnp.maximum(dist, 0)
    nf = jnp.maximum(n, 1).astype(jnp.float32)
    log_ratio = jnp.log(nf / exact) / math.log(REL_MAX_DIST / exact)
    large = exact + (log_ratio * (REL_BUCKETS - exact)).astype(jnp.int32)
    large = jnp.minimum(large, REL_BUCKETS - 1)
    return jnp.where(n < exact, n, large)


def s5_mixer(u, a_re, a_im, b_re, b_im, c_re, c_im, d_skip, log_step, glu_w, glu_b):
    bsz, seq, _ = u.shape
    f32 = jnp.float32
    ug = u.astype(f32).reshape(bsz, seq, SSM_GROUPS, SSM_GROUP)
    lam = lax.complex(a_re.astype(f32), a_im.astype(f32))
    step = jnp.exp(log_step.astype(f32))[:, None]
    lam_bar = jnp.exp(lam * step)
    b_mat = lax.complex(b_re.astype(f32), b_im.astype(f32))
    b_bar = ((lam_bar - 1.0) / lam)[:, :, None] * b_mat
    bu = jnp.einsum('blgh,gnh->lbgn', ug.astype(jnp.complex64), b_bar)
    a_elems = jnp.broadcast_to(lam_bar[None, None], (seq, 1) + lam_bar.shape)

    def combine(left, right):
        a_l, b_l = left
        a_r, b_r = right
        return a_r * a_l, a_r * b_l + b_r

    _, states = lax.associative_scan(combine, (a_elems, bu), axis=0)
    c_mat = lax.complex(c_re.astype(f32), c_im.astype(f32))
    y = jnp.real(jnp.einsum('lbgn,ghn->blgh', states, c_mat)) + d_skip.astype(f32) * ug
    z = jax.nn.gelu(y.reshape(bsz, seq, SSM_WIDTH))
    out = z * jax.nn.sigmoid(z @ glu_w.astype(f32) + glu_b.astype(f32))
    return out.astype(u.dtype)


def moba_mixer(q, k, v, rel_bias):
    bsz, nh, seq, dh = q.shape
    f32 = jnp.float32
    n_blk = -(-seq // MOBA_BLOCK)
    pad = n_blk * MOBA_BLOCK - seq
    k_pad = jnp.pad(k, ((0, 0), (0, 0), (0, pad), (0, 0)))
    v_pad = jnp.pad(v, ((0, 0), (0, 0), (0, pad), (0, 0)))
    k_blocks = k_pad.reshape(bsz, nh, n_blk, MOBA_BLOCK, dh)
    v_blocks = v_pad.reshape(bsz, nh, n_blk, MOBA_BLOCK, dh)
    k_mean = jnp.mean(k_blocks.astype(f32), axis=3)
    gate = jnp.einsum('bhld,bhnd->bhln', q.astype(f32), k_mean)
    q_blk = jnp.arange(seq) // MOBA_BLOCK
    fully_past = jnp.arange(n_blk)[None, :] < q_blk[:, None]
    gate = jnp.where(fully_past, gate, -jnp.inf)
    n_sel = min(MOBA_TOPK, n_blk)
    _, sel = lax.top_k(gate, n_sel)
    sel_valid = jnp.arange(n_sel)[None, :] < q_blk[:, None]

    n_chunk = seq // MOBA_QCHUNK
    q_ch = q.reshape(bsz, nh, n_chunk, MOBA_QCHUNK, dh).transpose(2, 0, 1, 3, 4)
    sel_ch = sel.reshape(bsz, nh, n_chunk, MOBA_QCHUNK, n_sel).transpose(2, 0, 1, 3, 4)
    valid_ch = sel_valid.reshape(n_chunk, MOBA_QCHUNK, n_sel)
    bias_hb = rel_bias.astype(f32).T
    b_idx = jnp.arange(bsz)[:, None, None, None]
    h_idx = jnp.arange(nh)[None, :, None, None]
    h_idx5 = jnp.arange(nh)[None, :, None, None, None]
    offs = jnp.arange(MOBA_BLOCK)
    scale = dh ** -0.5

    def chunk(args):
        c, q_c, sel_c, valid_c = args
        t = c * MOBA_QCHUNK + jnp.arange(MOBA_QCHUNK)
        k_sel = k_blocks[b_idx, h_idx, sel_c]
        v_sel = v_blocks[b_idx, h_idx, sel_c]
        s_sel = jnp.einsum('bhqd,bhqnkd->bhqnk', q_c, k_sel).astype(f32) * scale
        dist_sel = t[:, None, None] - (sel_c[..., None] * MOBA_BLOCK + offs)
        s_sel = s_sel + bias_hb[h_idx5, rel_bucket(dist_sel)]
        s_sel = jnp.where(valid_c[None, None, :, :, None], s_sel, -jnp.inf)
        own = (c * MOBA_QCHUNK) // MOBA_BLOCK
        k_own = lax.dynamic_slice_in_dim(k_pad, own * MOBA_BLOCK, MOBA_BLOCK, axis=2)
        v_own = lax.dynamic_slice_in_dim(v_pad, own * MOBA_BLOCK, MOBA_BLOCK, axis=2)
        s_own = jnp.einsum('bhqd,bhkd->bhqk', q_c, k_own).astype(f32) * scale
        dist_own = t[:, None] - (own * MOBA_BLOCK + offs)[None, :]
        s_own = s_own + bias_hb[:, rel_bucket(dist_own)]
        s_own = jnp.where(dist_own >= 0, s_own, -jnp.inf)
        logits = jnp.concatenate(
            [s_sel.reshape(bsz, nh, MOBA_QCHUNK, n_sel * MOBA_BLOCK), s_own], axis=-1)
        w = jax.nn.softmax(logits, axis=-1).astype(v.dtype)
        w_sel = w[..., :n_sel * MOBA_BLOCK].reshape(bsz, nh, MOBA_QCHUNK, n_sel, MOBA_BLOCK)
        w_own = w[..., n_sel * MOBA_BLOCK:]
        return (jnp.einsum('bhqnk,bhqnkd->bhqd', w_sel, v_sel)
                + jnp.einsum('bhqk,bhkd->bhqd', w_own, v_own))

    out = lax.map(chunk, (jnp.arange(n_chunk), q_ch, sel_ch, valid_ch))
    return out.transpose(1, 2, 0, 3, 4).reshape(bsz, nh, seq, dh)


def stick_breaking_mixer(q, k, v):
    bsz, nh, seq, dh = q.shape
    f32 = jnp.float32
    n_qb = seq // SB_QBLOCK
    q_bl = q.reshape(bsz, nh, n_qb, SB_QBLOCK, dh).transpose(2, 0, 1, 3, 4)
    k_pos = jnp.arange(seq)
    scale = dh ** -0.5

    def block(args):
        c, q_c = args
        t = c * SB_QBLOCK + jnp.arange(SB_QBLOCK)
        z = jnp.einsum('bhqd,bhkd->bhqk', q_c, k).astype(f32) * scale
        past = k_pos[None, :] < t[:, None]
        log_keep = jnp.where(past, jax.nn.log_sigmoid(-z), 0.0)
        later = lax.cumsum(log_keep, axis=3, reverse=True) - log_keep
        w = jnp.where(past, jnp.exp(jax.nn.log_sigmoid(z) + later), 0.0)
        return jnp.einsum('bhqk,bhkd->bhqd', w.astype(v.dtype), v)

    out = lax.map(block, (jnp.arange(n_qb), q_bl))
    return out.transpose(1, 2, 0, 3, 4).reshape(bsz, nh, seq, dh)


def moe_ffn(h, router_w, router_b, w_gate, b_gate, w_up, b_up, w_down, b_down):
    bsz, seq, d = h.shape
    n_tok = bsz * seq
    xt = h.reshape(n_tok, d)
    logits = (xt @ router_w + router_b).astype(jnp.float32)
    top_val, top_idx = lax.top_k(logits, TOP_K)
    gates = jax.nn.softmax(top_val, axis=-1)
    n_assign = n_tok * TOP_K
    flat_e = top_idx.reshape(-1)
    order = jnp.argsort(flat_e)
    sorted_e = flat_e[order]
    sorted_tok = order // TOP_K
    sorted_gate = gates.reshape(-1)[order]
    counts = jnp.bincount(flat_e, length=N_EXPERTS)
    padded = (counts + MOE_BLOCK - 1) // MOE_BLOCK * MOE_BLOCK
    pad_end = jnp.cumsum(padded)
    pad_start = pad_end - padded
    start = jnp.cumsum(counts) - counts
    dest = pad_start[sorted_e] + jnp.arange(n_assign) - start[sorted_e]
    n_blocks = -(-n_assign // MOE_BLOCK) + N_EXPERTS
    n_slots = n_blocks * MOE_BLOCK
    slot_tok = jnp.full((n_slots,), n_tok, jnp.int32).at[dest].set(sorted_tok)
    block_e = jnp.minimum(
        jnp.searchsorted(pad_end, jnp.arange(n_blocks) * MOE_BLOCK, side='right'), N_EXPERTS - 1)
    x_pad = jnp.concatenate([xt, jnp.zeros((1, d), xt.dtype)], axis=0)
    xs = x_pad[slot_tok].reshape(n_blocks, MOE_BLOCK, d)

    def expert_block(args):
        xb, e = args
        g = jnp.minimum(xb @ w_gate[e] + b_gate[e], SWIGLU_LIMIT)
        u = jnp.clip(xb @ w_up[e] + b_up[e], -SWIGLU_LIMIT, SWIGLU_LIMIT)
        act = g * jax.nn.sigmoid(SWIGLU_ALPHA * g) * (u + 1.0)
        return act @ w_down[e] + b_down[e]

    ys = lax.map(expert_block, (xs, block_e)).reshape(n_slots, d)
    out = jnp.zeros((n_tok, d), jnp.float32).at[sorted_tok].add(
        ys[dest].astype(jnp.float32) * sorted_gate[:, None])
    return out.astype(h.dtype).reshape(bsz, seq, d)


def per_layer_embed(p_i, r, g_norm, w_ple, w_ple_gate):
    proj = p_i @ w_ple
    gate = jax.nn.sigmoid(rmsnorm(r, g_norm) @ w_ple_gate)
    return proj * gate


def mixer_ab(h, w_in, a_re, a_im, b_re, b_im, c_re, c_im, d_skip, log_step,
             glu_w, glu_b, w_out, rel_bias):
    proj = h @ w_in
    u = proj[..., :SSM_WIDTH]
    o = SSM_WIDTH
    q = split_heads(proj[..., o:o + MOBA_WIDTH], MOBA_HEADS)
    k = split_heads(proj[..., o + MOBA_WIDTH:o + 2 * MOBA_WIDTH], MOBA_HEADS)
    v = split_heads(proj[..., o + 2 * MOBA_WIDTH:o + 3 * MOBA_WIDTH], MOBA_HEADS)
    y_a = s5_mixer(u, a_re, a_im, b_re, b_im, c_re, c_im, d_skip, log_step, glu_w, glu_b)
    y_b = merge_heads(moba_mixer(q, k, v, rel_bias))
    return jnp.concatenate([y_a, y_b], axis=-1) @ w_out


def mixer_c(h, w_in, w_out):
    proj = h @ w_in
    q = split_heads(proj[..., :D_MODEL], SB_HEADS)
    k = split_heads(proj[..., D_MODEL:2 * D_MODEL], SB_HEADS)
    v = split_heads(proj[..., 2 * D_MODEL:], SB_HEADS)
    return merge_heads(stick_breaking_mixer(q, k, v)) @ w_out


def setup_inputs(seed: int = 0) -> dict:
    key = jax.random.key(seed)
    ks = iter(jax.random.split(key, 48))
    f32 = jnp.float32

    def nrm(shape, scale):
        return jax.random.normal(next(ks), shape, f32) * scale

    ssm_shape = (N_EVEN, SSM_GROUPS, SSM_STATE)
    n_idx = jnp.arange(SSM_STATE, dtype=f32)
    return {
        'x': nrm((BATCH, SEQ, D_MODEL), 1.0),
        'p': nrm((DEPTH, BATCH, SEQ, PLE_DIM), 1.0),
        'norm_mix': 1.0 + nrm((DEPTH, D_MODEL), 0.02),
        'norm_ffn': 1.0 + nrm((DEPTH, D_MODEL), 0.02),
        'norm_ple': 1.0 + nrm((DEPTH, D_MODEL), 0.02),
        'norm_final': 1.0 + nrm((D_MODEL,), 0.02),
        'w_in_ab': nrm((N_EVEN, D_MODEL, SSM_WIDTH + 3 * MOBA_WIDTH), D_MODEL ** -0.5),
        'ssm_a_re': -0.5 * jnp.exp(nrm(ssm_shape, 0.05)),
        'ssm_a_im': math.pi * n_idx + nrm(ssm_shape, 0.01),
        'ssm_b_re': nrm((N_EVEN, SSM_GROUPS, SSM_STATE, SSM_GROUP), (2 * SSM_GROUP) ** -0.5),
        'ssm_b_im': nrm((N_EVEN, SSM_GROUPS, SSM_STATE, SSM_GROUP), (2 * SSM_GROUP) ** -0.5),
        'ssm_c_re': nrm((N_EVEN, SSM_GROUPS, SSM_GROUP, SSM_STATE), SSM_STATE ** -0.5),
        'ssm_c_im': nrm((N_EVEN, SSM_GROUPS, SSM_GROUP, SSM_STATE), SSM_STATE ** -0.5),
        'ssm_d': nrm((N_EVEN, SSM_GROUPS, SSM_GROUP), 1.0),
        'ssm_log_step': jax.random.uniform(next(ks), (N_EVEN, SSM_GROUPS), f32,
                                           math.log(DT_MIN), math.log(DT_MAX)),
        'glu_w': nrm((N_EVEN, SSM_WIDTH, SSM_WIDTH), SSM_WIDTH ** -0.5),
        'glu_b': nrm((N_EVEN, SSM_WIDTH), 0.01),
        'w_out_ab': nrm((N_EVEN, SSM_WIDTH + MOBA_WIDTH, D_MODEL), D_MODEL ** -0.5),
        'rel_bias': nrm((REL_BUCKETS, MOBA_HEADS), 0.1),
        'w_in_c': nrm((N_ODD, D_MODEL, 3 * D_MODEL), D_MODEL ** -0.5),
        'w_out_c': nrm((N_ODD, D_MODEL, D_MODEL), D_MODEL ** -0.5),
        'router_w': nrm((DEPTH, D_MODEL, N_EXPERTS), D_MODEL ** -0.5),
        'router_b': nrm((DEPTH, N_EXPERTS), 0.01),
        'w_gate': nrm((DEPTH, N_EXPERTS, D_MODEL, D_FF), D_MODEL ** -0.5),
        'b_gate': nrm((DEPTH, N_EXPERTS, D_FF), 0.01),
        'w_up': nrm((DEPTH, N_EXPERTS, D_MODEL, D_FF), D_MODEL ** -0.5),
        'b_up': nrm((DEPTH, N_EXPERTS, D_FF), 0.01),
        'w_down': nrm((DEPTH, N_EXPERTS, D_FF, D_MODEL), D_FF ** -0.5),
        'b_down': nrm((DEPTH, N_EXPERTS, D_MODEL), 0.01),
        'w_ple': nrm((DEPTH, PLE_DIM, D_MODEL), PLE_DIM ** -0.5),
        'w_ple_gate': nrm((DEPTH, D_MODEL, D_MODEL), D_MODEL ** -0.5),
    }


def reference(x, p, norm_mix, norm_ffn, norm_ple, norm_final, w_in_ab, ssm_a_re, ssm_a_im,
              ssm_b_re, ssm_b_im, ssm_c_re, ssm_c_im, ssm_d, ssm_log_step, glu_w, glu_b,
              w_out_ab, rel_bias, w_in_c, w_out_c, router_w, router_b, w_gate, b_gate,
              w_up, b_up, w_down, b_down, w_ple, w_ple_gate):
    r = x
    for i in range(DEPTH):
        j = i // 2
        h = rmsnorm(r, norm_mix[i])
        if i % 2 == 0:
            r = r + mixer_ab(h, w_in_ab[j], ssm_a_re[j], ssm_a_im[j], ssm_b_re[j], ssm_b_im[j],
                             ssm_c_re[j], ssm_c_im[j], ssm_d[j], ssm_log_step[j],
                             glu_w[j], glu_b[j], w_out_ab[j], rel_bias)
        else:
            r = r + mixer_c(h, w_in_c[j], w_out_c[j])
        h = rmsnorm(r, norm_ffn[i])
        r = r + moe_ffn(h, router_w[i], router_b[i], w_gate[i], b_gate[i],
                        w_up[i], b_up[i], w_down[i], b_down[i])
        r = r + per_layer_embed(p[i], r, norm_ple[i], w_ple[i], w_ple_gate[i])
    return rmsnorm(r, norm_final)
```

```python
import functools
import math

import jax
import jax.numpy as jnp
from jax import lax
from jax.experimental import pallas as pl
from jax.experimental.pallas import tpu as pltpu

F32 = jnp.float32
BF16 = jnp.bfloat16
I32 = jnp.int32

RMS_EPS = 1e-6
HEAD_DIM = 64
LANES = 128
SUBLANES = 8
SSM_GROUP = 16
SSM_STATE = 64
SSM_SLAB_GROUPS = LANES // SSM_GROUP
MOBA_BLOCK = 256
MOBA_TOPK = 3
REL_BUCKETS = 32
REL_MAX_DIST = 128
SB_BLOCK = 256
N_EXPERTS = 32
TOP_K = 4
SWIGLU_LIMIT = 7.0
SWIGLU_ALPHA = 1.702
MOE_ROWS = 256
NEG = -1e30
MIB = 1024 * 1024

_NT = (((1,), (1,)), ((), ()))


def _cparams(n_axes, vmem_mib):
    return pltpu.CompilerParams(
        dimension_semantics=("arbitrary",) * n_axes, vmem_limit_bytes=vmem_mib * MIB)


def _rms(x, g):
    ms = jnp.mean(x * x, axis=-1, keepdims=True)
    return x * lax.rsqrt(ms + RMS_EPS) * g


def _dot(a, b):
    return jnp.dot(a, b, preferred_element_type=F32)


def _dot_nt(a, b):
    return lax.dot_general(a, b, _NT, preferred_element_type=F32)


def _norm_matmul_kernel(x_ref, g_ref, w_ref, *o_refs, splits):
    h = _rms(x_ref[...], g_ref[...]).astype(BF16)
    y = _dot(h, w_ref[...])
    off = 0
    for o_ref, s in zip(o_refs, splits):
        o_ref[...] = y[:, off:off + s].astype(o_ref.dtype)
        off += s


def norm_matmul(x, g, w, splits, dtypes, tm=512):
    n, d = x.shape
    nout = w.shape[1]
    return pl.pallas_call(
        functools.partial(_norm_matmul_kernel, splits=splits),
        grid=(n // tm,),
        in_specs=[pl.BlockSpec((tm, d), lambda i: (i, 0)),
                  pl.BlockSpec((1, d), lambda i: (0, 0)),
                  pl.BlockSpec((d, nout), lambda i: (0, 0))],
        out_specs=[pl.BlockSpec((tm, s), lambda i: (i, 0)) for s in splits],
        out_shape=[jax.ShapeDtypeStruct((n, s), dt) for s, dt in zip(splits, dtypes)],
        compiler_params=_cparams(1, 48),
        name="norm_matmul",
    )(x, g.reshape(1, d), w)


def _matmul_residual_kernel(r_ref, *refs, n_in):
    acc = r_ref[...]
    for a_ref, w_ref in zip(refs[:n_in], refs[n_in:2 * n_in]):
        acc = acc + _dot(a_ref[...], w_ref[...])
    refs[2 * n_in][...] = acc


def matmul_residual(r, a_list, w_list, tm=512):
    n, d = r.shape
    n_in = len(a_list)
    in_specs = [pl.BlockSpec((tm, d), lambda i: (i, 0))]
    in_specs += [pl.BlockSpec((tm, a.shape[1]), lambda i: (i, 0)) for a in a_list]
    in_specs += [pl.BlockSpec(w.shape, lambda i: (0, 0)) for w in w_list]
    return pl.pallas_call(
        functools.partial(_matmul_residual_kernel, n_in=n_in),
        grid=(n // tm,),
        in_specs=in_specs,
        out_specs=pl.BlockSpec((tm, d), lambda i: (i, 0)),
        out_shape=jax.ShapeDtypeStruct((n, d), F32),
        compiler_params=_cparams(1, 32),
        name="matmul_residual",
    )(r, *a_list, *w_list)


def _s5_kernel(u_ref, a_ref, b_ref, c_ref, d_ref, gw_ref, gb_ref, o_ref, s_ref, x_ref,
               *, steps, n_slab):
    half = SSM_SLAB_GROUPS * SSM_STATE
    slab = 2 * half

    @pl.when(pl.program_id(0) == 0)
    def _():
        x_ref[...] = jnp.zeros_like(x_ref)

    u = u_ref[...]
    ub = u.astype(BF16)
    for s in range(n_slab):
        s_ref[:, s * slab:(s + 1) * slab] = _dot(ub[:, s * LANES:(s + 1) * LANES], b_ref[s])

    for s in range(n_slab):
        re = slice(s * slab, s * slab + half)
        im = slice(s * slab + half, (s + 1) * slab)
        ar = a_ref[:, re]
        ai = a_ref[:, im]

        def step(t, carry, re=re, im=im, ar=ar, ai=ai):
            xr, xi = carry
            rows = pl.ds(pl.multiple_of(t * SUBLANES, SUBLANES), SUBLANES)
            nr = ar * xr - ai * xi + s_ref[rows, re]
            ni = ar * xi + ai * xr + s_ref[rows, im]
            s_ref[rows, re] = nr
            s_ref[rows, im] = ni
            return nr, ni

        xr, xi = lax.fori_loop(0, steps, step, (x_ref[:, re], x_ref[:, im]), unroll=8)
        x_ref[:, re] = xr
        x_ref[:, im] = xi

    y = jnp.concatenate(
        [_dot(s_ref[:, s * slab:(s + 1) * slab].astype(BF16), c_ref[s]) for s in range(n_slab)],
        axis=1)
    y = y + d_ref[...] * u
    z = jax.nn.gelu(y)
    gate = jax.nn.sigmoid(_dot(z.astype(BF16), gw_ref[...]) + gb_ref[...])
    o_ref[...] = (z * gate).astype(o_ref.dtype)


def s5_mixer(u_tb, a_mat, b_mat, c_mat, d_vec, glu_w, glu_b, steps=64):
    rows, width = u_tb.shape
    n_slab = width // LANES
    n_state = 2 * n_slab * SSM_SLAB_GROUPS * SSM_STATE
    tm = steps * SUBLANES
    return pl.pallas_call(
        functools.partial(_s5_kernel, steps=steps, n_slab=n_slab),
        grid=(rows // tm,),
        in_specs=[pl.BlockSpec((tm, width), lambda i: (i, 0)),
                  pl.BlockSpec(a_mat.shape, lambda i: (0, 0)),
                  pl.BlockSpec(b_mat.shape, lambda i: (0, 0, 0)),
                  pl.BlockSpec(c_mat.shape, lambda i: (0, 0, 0)),
                  pl.BlockSpec((1, width), lambda i: (0, 0)),
                  pl.BlockSpec(glu_w.shape, lambda i: (0, 0)),
                  pl.BlockSpec((1, width), lambda i: (0, 0))],
        out_specs=pl.BlockSpec((tm, width), lambda i: (i, 0)),
        out_shape=jax.ShapeDtypeStruct((rows, width), BF16),
        scratch_shapes=[pltpu.VMEM((tm, n_state), F32), pltpu.VMEM((SUBLANES, n_state), F32)],
        compiler_params=_cparams(1, 40),
        name="s5_mixer",
    )(u_tb, a_mat, b_mat, c_mat, d_vec, glu_w, glu_b)


def _s5_params(a_re, a_im, b_re, b_im, c_re, c_im, log_step):
    n_grp = a_re.shape[0]
    n_slab = n_grp // SSM_SLAB_GROUPS
    lam = lax.complex(a_re.astype(F32), a_im.astype(F32))
    step = jnp.exp(log_step.astype(F32))[:, None]
    lam_bar = jnp.exp(lam * step)
    b_bar = ((lam_bar - 1.0) / lam)[:, :, None] * lax.complex(b_re.astype(F32), b_im.astype(F32))
    eye = jnp.eye(SSM_SLAB_GROUPS, dtype=F32)

    def slabbed(t):
        return t.reshape((n_slab, SSM_SLAB_GROUPS) + t.shape[1:])

    a_mat = jnp.concatenate(
        [slabbed(jnp.real(lam_bar)).reshape(n_slab, -1), slabbed(jnp.imag(lam_bar)).reshape(n_slab, -1)],
        axis=1).reshape(1, -1)
    a_mat = jnp.broadcast_to(a_mat, (SUBLANES, a_mat.shape[1]))
    b_parts = [jnp.einsum('sgnh,gk->sghkn', slabbed(part(b_bar)), eye)
               for part in (jnp.real, jnp.imag)]
    b_mat = jnp.stack(b_parts, axis=3).reshape(n_slab, LANES, -1)
    c_parts = [jnp.einsum('sghn,gk->sgnkh', slabbed(part), eye)
               for part in (c_re.astype(F32), -c_im.astype(F32))]
    c_mat = jnp.stack(c_parts, axis=1).reshape(n_slab, -1, LANES)
    return a_mat, b_mat.astype(BF16), c_mat.astype(BF16)


def _head_masks(shape):
    lane = lax.broadcasted_iota(I32, shape, 1)
    return lane < HEAD_DIM, lane >= HEAD_DIM


def _moba_kernel(q_ref, k_ref, v_ref, tb_ref, o_ref, km_ref, *, n_blk):
    blk = MOBA_BLOCK
    qb = pl.program_id(2)
    lane = lax.broadcasted_iota(I32, (blk, LANES), 1)
    row = lax.broadcasted_iota(I32, (blk, blk), 0)
    col = lax.broadcasted_iota(I32, (blk, blk), 1)

    @pl.when(qb == 0)
    def _():
        km_ref[...] = jnp.zeros_like(km_ref)
        for n in range(n_blk):
            kb = k_ref[0, n * blk:(n + 1) * blk, :].astype(F32)
            km_ref[n:n + 1, :] = jnp.sum(kb, axis=0, keepdims=True) / blk

    q = q_ref[0]
    own = pl.ds(pl.multiple_of(qb * blk, blk), blk)
    outs = []
    for hh, hmask in enumerate(_head_masks((blk, LANES))):
        qm = jnp.where(hmask, q, jnp.zeros_like(q))
        gate = lax.dot_general(qm.astype(F32), km_ref[...], _NT, preferred_element_type=F32,
                               precision=lax.Precision.HIGHEST)
        gate = jnp.where(lane < qb, gate, -jnp.inf)
        sel = jnp.zeros((blk, LANES), F32)
        for r in range(MOBA_TOPK):
            top = jnp.max(gate, axis=1, keepdims=True)
            idx = jnp.min(jnp.where(gate == top, lane, LANES), axis=1, keepdims=True)
            hit = lane == idx
            sel = jnp.where(jnp.logical_and(hit, r < qb), 1.0, sel)
            gate = jnp.where(hit, -jnp.inf, gate)

        qs = qm * (HEAD_DIM ** -0.5)
        s = _dot_nt(qs, k_ref[0, own, :]) + tb_ref[hh, 0]
        s = jnp.where(col <= row, s, NEG)
        m = jnp.max(s, axis=1, keepdims=True)
        p = jnp.exp(s - m)
        l = jnp.sum(p, axis=1, keepdims=True)
        acc = _dot(p.astype(BF16), v_ref[0, own, :])

        def body(n, carry, hh=hh, sel=sel, qs=qs):
            m, l, acc = carry
            rows = pl.ds(pl.multiple_of(n * blk, blk), blk)
            chosen = jnp.sum(jnp.where(lane == n, sel, 0.0), axis=1, keepdims=True) > 0.0
            s = _dot_nt(qs, k_ref[0, rows, :]) + tb_ref[hh, jnp.minimum(qb - n, 2)]
            s = jnp.where(chosen, s, NEG)
            m_new = jnp.maximum(m, jnp.max(s, axis=1, keepdims=True))
            alpha = jnp.exp(m - m_new)
            p = jnp.exp(s - m_new)
            l = alpha * l + jnp.sum(p, axis=1, keepdims=True)
            acc = alpha * acc + _dot(p.astype(BF16), v_ref[0, rows, :])
            return m_new, l, acc

        m, l, acc = lax.fori_loop(0, qb, body, (m, l, acc))
        outs.append(acc / l)
    o_ref[0] = jnp.where(lane < HEAD_DIM, outs[0], outs[1]).astype(o_ref.dtype)


def _rel_bucket(dist):
    exact = REL_BUCKETS // 2
    n = jnp.maximum(dist, 0)
    nf = jnp.maximum(n, 1).astype(F32)
    log_ratio = jnp.log(nf / exact) / math.log(REL_MAX_DIST / exact)
    large = exact + (log_ratio * (REL_BUCKETS - exact)).astype(I32)
    large = jnp.minimum(large, REL_BUCKETS - 1)
    return jnp.where(n < exact, n, large)


def _moba_bias_tables(rel_bias):
    assert REL_MAX_DIST <= MOBA_BLOCK + 1
    i = jnp.arange(MOBA_BLOCK)[:, None]
    j = jnp.arange(MOBA_BLOCK)[None, :]
    dist = jnp.arange(3)[:, None, None] * MOBA_BLOCK + (i - j)[None]
    return rel_bias.astype(F32).T[:, _rel_bucket(dist)]


def moba_mixer(qkv, rel_bias, width):
    bsz, seq, _ = qkv.shape
    blk = MOBA_BLOCK
    n_blk = seq // blk
    n_pair = width // LANES
    assert seq % blk == 0 and n_blk <= LANES
    tables = _moba_bias_tables(rel_bias)
    return pl.pallas_call(
        functools.partial(_moba_kernel, n_blk=n_blk),
        grid=(bsz, n_pair, n_blk),
        in_specs=[pl.BlockSpec((1, blk, LANES), lambda b, p, i: (b, i, p)),
                  pl.BlockSpec((1, seq, LANES), lambda b, p, i: (b, 0, n_pair + p)),
                  pl.BlockSpec((1, seq, LANES), lambda b, p, i: (b, 0, 2 * n_pair + p)),
                  pl.BlockSpec((2, 3, blk, blk), lambda b, p, i: (p, 0, 0, 0))],
        out_specs=pl.BlockSpec((1, blk, LANES), lambda b, p, i: (b, i, p)),
        out_shape=jax.ShapeDtypeStruct((bsz, seq, width), BF16),
        scratch_shapes=[pltpu.VMEM((LANES, LANES), F32)],
        compiler_params=_cparams(3, 32),
        name="moba_mixer",
    )(qkv, qkv, qkv, tables)


def _sb_kernel(q_ref, k_ref, v_ref, tri_ref, o_ref):
    blk = SB_BLOCK
    qb = pl.program_id(2)
    lane = lax.broadcasted_iota(I32, (blk, LANES), 1)
    row = lax.broadcasted_iota(I32, (blk, blk), 0)
    col = lax.broadcasted_iota(I32, (blk, blk), 1)
    past = col < row
    q = q_ref[0]
    own = pl.ds(pl.multiple_of(qb * blk, blk), blk)

    def block(qs, kn, vn, carried, diagonal):
        z = _dot_nt(qs, kn)
        softplus = jnp.maximum(z, 0.0) + jnp.log(1.0 + jnp.exp(-jnp.abs(z)))
        log_keep = -softplus
        if diagonal:
            log_keep = jnp.where(past, log_keep, 0.0)
        hi = log_keep.astype(BF16)
        lo = (log_keep - hi.astype(F32)).astype(BF16)
        later = _dot(hi, tri_ref[...]) + _dot(lo, tri_ref[...])
        if carried is not None:
            later = later + carried
        w = jnp.exp(z - softplus + later)
        if diagonal:
            w = jnp.where(past, w, 0.0)
        return _dot(w.astype(BF16), vn), jnp.sum(log_keep, axis=1, keepdims=True)

    outs = []
    for hmask in _head_masks((blk, LANES)):
        qs = jnp.where(hmask, q, jnp.zeros_like(q)) * (HEAD_DIM ** -0.5)
        acc, carried = block(qs, k_ref[0, own, :], v_ref[0, own, :], None, True)

        def body(i, carry, qs=qs):
            acc, carried = carry
            rows = pl.ds(pl.multiple_of((qb - 1 - i) * blk, blk), blk)
            pv, total = block(qs, k_ref[0, rows, :], v_ref[0, rows, :], carried, False)
            return acc + pv, carried + total

        acc, _ = lax.fori_loop(0, qb, body, (acc, carried))
        outs.append(acc)
    o_ref[0] = jnp.where(lane < HEAD_DIM, outs[0], outs[1]).astype(o_ref.dtype)


def stick_breaking_mixer(qkv, width):
    bsz, seq, _ = qkv.shape
    blk = SB_BLOCK
    n_pair = width // LANES
    tri = (jnp.arange(blk)[:, None] > jnp.arange(blk)[None, :]).astype(BF16)
    return pl.pallas_call(
        _sb_kernel,
        grid=(bsz, n_pair, seq // blk),
        in_specs=[pl.BlockSpec((1, blk, LANES), lambda b, p, i: (b, i, p)),
                  pl.BlockSpec((1, seq, LANES), lambda b, p, i: (b, 0, n_pair + p)),
                  pl.BlockSpec((1, seq, LANES), lambda b, p, i: (b, 0, 2 * n_pair + p)),
                  pl.BlockSpec((blk, blk), lambda b, p, i: (0, 0))],
        out_specs=pl.BlockSpec((1, blk, LANES), lambda b, p, i: (b, i, p)),
        out_shape=jax.ShapeDtypeStruct((bsz, seq, width), BF16),
        compiler_params=_cparams(3, 32),
        name="stick_breaking",
    )(qkv, qkv, qkv, tri)


def _router_kernel(r_ref, g_ref, w_ref, b_ref, idx_ref, gate_ref, rank_ref, cnt_ref, run_ref):
    tm = r_ref.shape[0]

    @pl.when(pl.program_id(0) == 0)
    def _():
        run_ref[...] = jnp.zeros_like(run_ref)

    h = _rms(r_ref[...], g_ref[...])
    logits = jnp.dot(h, w_ref[...], preferred_element_type=F32,
                     precision=lax.Precision.HIGHEST) + b_ref[...]
    lane = lax.broadcasted_iota(I32, (tm, LANES), 1)
    tops, hits = [], []
    for _ in range(TOP_K):
        top = jnp.max(logits, axis=1, keepdims=True)
        idx = jnp.min(jnp.where(logits == top, lane, LANES), axis=1, keepdims=True)
        hit = lane == idx
        logits = jnp.where(hit, -jnp.inf, logits)
        tops.append(top)
        hits.append(hit)
    exps = [jnp.exp(t - tops[0]) for t in tops]
    denom = exps[0]
    for e in exps[1:]:
        denom = denom + e

    member = jnp.zeros((tm, LANES), F32)
    for hit in hits:
        member = jnp.where(hit, 1.0, member)
    before = (lax.broadcasted_iota(I32, (tm, tm), 1) < lax.broadcasted_iota(I32, (tm, tm), 0))
    ahead = _dot(before.astype(BF16), member.astype(BF16)) + run_ref[...]

    idx_out = jnp.zeros((tm, LANES), I32)
    gate_out = jnp.zeros((tm, LANES), F32)
    rank_out = jnp.zeros((tm, LANES), F32)
    for k in range(TOP_K):
        idx_k = jnp.sum(jnp.where(hits[k], lane, 0), axis=1, keepdims=True)
        rank_k = jnp.sum(jnp.where(hits[k], ahead, 0.0), axis=1, keepdims=True)
        idx_out = jnp.where(lane == k, idx_k, idx_out)
        gate_out = jnp.where(lane == k, exps[k] / denom, gate_out)
        rank_out = jnp.where(lane == k, rank_k, rank_out)
    idx_ref[...] = idx_out
    gate_ref[...] = gate_out
    rank_ref[...] = rank_out.astype(I32)
    run_ref[...] = run_ref[...] + jnp.sum(member, axis=0, keepdims=True)
    cnt_ref[...] = run_ref[...].astype(I32)


def moe_router(r, g, router_w, router_b, tm=512):
    n, d = r.shape
    n_exp = router_w.shape[1]
    w_pad = jnp.pad(router_w.astype(F32), ((0, 0), (0, LANES - n_exp)))
    b_pad = jnp.pad(router_b.astype(F32).reshape(1, n_exp), ((0, 0), (0, LANES - n_exp)),
                    constant_values=-jnp.inf)
    tile = pl.BlockSpec((tm, LANES), lambda i: (i, 0))
    idx, gates, rank, counts = pl.pallas_call(
        _router_kernel,
        grid=(n // tm,),
        in_specs=[pl.BlockSpec((tm, d), lambda i: (i, 0)),
                  pl.BlockSpec((1, d), lambda i: (0, 0)),
                  pl.BlockSpec((d, LANES), lambda i: (0, 0)),
                  pl.BlockSpec((1, LANES), lambda i: (0, 0))],
        out_specs=[tile, tile, tile, pl.BlockSpec((1, LANES), lambda i: (0, 0))],
        out_shape=[jax.ShapeDtypeStruct((n, LANES), I32), jax.ShapeDtypeStruct((n, LANES), F32),
                   jax.ShapeDtypeStruct((n, LANES), I32), jax.ShapeDtypeStruct((1, LANES), I32)],
        scratch_shapes=[pltpu.VMEM((1, LANES), F32)],
        compiler_params=_cparams(1, 32),
        name="moe_router",
    )(r, g.reshape(1, d), w_pad, b_pad)
    return idx[:, :TOP_K], gates, rank[:, :TOP_K], counts[0, :n_exp]


def _dispatch_kernel(dest_ref, r_hbm, init_hbm, xs_hbm, sem, *, tile):
    del init_hbm
    base = pl.program_id(0) * tile

    def row_copy(t, slot):
        return pltpu.make_async_copy(r_hbm.at[pl.ds(t, 1)], xs_hbm.at[pl.ds(slot, 1)], sem)

    def start(j, c):
        for k in range(TOP_K):
            row_copy(base + j, dest_ref[0, 0, j * TOP_K + k]).start()
        return c

    def wait(j, c):
        for k in range(TOP_K):
            row_copy(0, 0).wait()
        return c

    lax.fori_loop(0, tile, start, 0)
    lax.fori_loop(0, tile, wait, 0)


def moe_dispatch(r, dest, n_slots, tile=256):
    n, d = r.shape
    dest_tiles = dest.reshape(n // tile, 1, tile * TOP_K)
    return pl.pallas_call(
        functools.partial(_dispatch_kernel, tile=tile),
        grid=(n // tile,),
        in_specs=[pl.BlockSpec((1, 1, tile * TOP_K), lambda i: (i, 0, 0), memory_space=pltpu.SMEM),
                  pl.BlockSpec(memory_space=pl.ANY),
                  pl.BlockSpec(memory_space=pl.ANY)],
        out_specs=pl.BlockSpec(memory_space=pl.ANY),
        out_shape=jax.ShapeDtypeStruct((n_slots, d), F32),
        scratch_shapes=[pltpu.SemaphoreType.DMA],
        input_output_aliases={2: 0},
        compiler_params=_cparams(1, 16),
        name="moe_dispatch",
    )(dest_tiles, r, jnp.zeros((n_slots, d), F32))


def _expert_kernel(be_ref, na_ref, x_ref, g_ref, wg_ref, bg_ref, wu_ref, bu_ref, wd_ref, bd_ref, y_ref):
    del be_ref
    active = pl.program_id(0) < na_ref[0]

    @pl.when(jnp.logical_not(active))
    def _():
        y_ref[...] = jnp.zeros_like(y_ref)

    @pl.when(active)
    def _():
        h = _rms(x_ref[...], g_ref[...]).astype(BF16)
        gate = jnp.minimum(_dot(h, wg_ref[0]) + bg_ref[0], SWIGLU_LIMIT)
        up = jnp.clip(_dot(h, wu_ref[0]) + bu_ref[0], -SWIGLU_LIMIT, SWIGLU_LIMIT)
        act = gate * jax.nn.sigmoid(SWIGLU_ALPHA * gate) * (up + 1.0)
        y_ref[...] = _dot(act.astype(BF16), wd_ref[0]) + bd_ref[0]


def moe_experts(xs, g, block_expert, n_active, w_gate, b_gate, w_up, b_up, w_down, b_down):
    n_slots, d = xs.shape
    n_exp, _, d_ff = w_gate.shape
    n_blocks = n_slots // MOE_ROWS

    def rows(i, be, na):
        return (jnp.minimum(i, na[0] - 1), 0)

    def expert3(i, be, na):
        return (be[i], 0, 0)

    grid_spec = pltpu.PrefetchScalarGridSpec(
        num_scalar_prefetch=2,
        grid=(n_blocks,),
        in_specs=[pl.BlockSpec((MOE_ROWS, d), rows),
                  pl.BlockSpec((1, d), lambda i, be, na: (0, 0)),
                  pl.BlockSpec((1, d, d_ff), expert3), pl.BlockSpec((1, 1, d_ff), expert3),
                  pl.BlockSpec((1, d, d_ff), expert3), pl.BlockSpec((1, 1, d_ff), expert3),
                  pl.BlockSpec((1, d_ff, d), expert3), pl.BlockSpec((1, 1, d), expert3)],
        out_specs=pl.BlockSpec((MOE_ROWS, d), lambda i, be, na: (i, 0)))
    return pl.pallas_call(
        _expert_kernel,
        grid_spec=grid_spec,
        out_shape=jax.ShapeDtypeStruct((n_slots, d), F32),
        compiler_params=_cparams(1, 48),
        name="moe_experts",
    )(block_expert, n_active, xs, g.reshape(1, d),
      w_gate, b_gate.reshape(n_exp, 1, d_ff), w_up, b_up.reshape(n_exp, 1, d_ff),
      w_down, b_down.reshape(n_exp, 1, d))


def _combine_kernel(dest_ref, ys_hbm, gate_ref, r_ref, p_ref, gp_ref, wp_ref, wpg_ref, gf_ref,
                    o_ref, buf_ref, sem, *, tile, final_norm):
    def row_copy(slot, k, j):
        return pltpu.make_async_copy(ys_hbm.at[pl.ds(slot, 1)], buf_ref.at[k, pl.ds(j, 1)], sem)

    def start(j, c):
        for k in range(TOP_K):
            row_copy(dest_ref[0, 0, j * TOP_K + k], k, j).start()
        return c

    def wait(j, c):
        for k in range(TOP_K):
            row_copy(0, k, 0).wait()
        return c

    lax.fori_loop(0, tile, start, 0)
    lax.fori_loop(0, tile, wait, 0)

    gates = gate_ref[...]
    r = r_ref[...]
    for k in range(TOP_K):
        r = r + buf_ref[k] * gates[:, k:k + 1]
    h = _rms(r, gp_ref[...]).astype(BF16)
    gate = jax.nn.sigmoid(_dot(h, wpg_ref[...]))
    r = r + _dot(p_ref[...].astype(BF16), wp_ref[...]) * gate
    if final_norm:
        r = _rms(r, gf_ref[...])
    o_ref[...] = r


def moe_combine_embed(ys, dest, gates, r, p, g_ple, w_ple, w_ple_gate, g_final, final_norm, tile=256):
    n, d = r.shape
    pd = p.shape[1]
    dest_tiles = dest.reshape(n // tile, 1, tile * TOP_K)
    return pl.pallas_call(
        functools.partial(_combine_kernel, tile=tile, final_norm=final_norm),
        grid=(n // tile,),
        in_specs=[pl.BlockSpec((1, 1, tile * TOP_K), lambda i: (i, 0, 0), memory_space=pltpu.SMEM),
                  pl.BlockSpec(memory_space=pl.ANY),
                  pl.BlockSpec((tile, LANES), lambda i: (i, 0)),
                  pl.BlockSpec((tile, d), lambda i: (i, 0)),
                  pl.BlockSpec((tile, pd), lambda i: (i, 0)),
                  pl.BlockSpec((1, d), lambda i: (0, 0)),
                  pl.BlockSpec((pd, d), lambda i: (0, 0)),
                  pl.BlockSpec((d, d), lambda i: (0, 0)),
                  pl.BlockSpec((1, d), lambda i: (0, 0))],
        out_specs=pl.BlockSpec((tile, d), lambda i: (i, 0)),
        out_shape=jax.ShapeDtypeStruct((n, d), F32),
        scratch_shapes=[pltpu.VMEM((TOP_K, tile, d), F32), pltpu.SemaphoreType.DMA],
        compiler_params=_cparams(1, 32),
        name="moe_combine_embed",
    )(dest_tiles, ys, gates, r, p, g_ple.reshape(1, d), w_ple, w_ple_gate, g_final.reshape(1, d))


def _slot_layout(idx, rank, counts, n_blocks):
    padded = (counts + MOE_ROWS - 1) // MOE_ROWS * MOE_ROWS
    pad_end = jnp.cumsum(padded)
    pad_start = pad_end - padded
    dest = (pad_start[idx] + rank).astype(I32).reshape(-1)
    block_expert = jnp.minimum(
        jnp.searchsorted(pad_end, jnp.arange(n_blocks) * MOE_ROWS, side='right'),
        counts.shape[0] - 1).astype(I32)
    n_active = (pad_end[-1:] // MOE_ROWS).astype(I32)
    return dest, block_expert, n_active


def moe_and_embed(r, p, g_ffn, router_w, router_b, w_gate, b_gate, w_up, b_up, w_down, b_down,
                  g_ple, w_ple, w_ple_gate, g_final, final_norm):
    n, _ = r.shape
    n_blocks = n * TOP_K // MOE_ROWS + router_w.shape[1]
    idx, gates, rank, counts = moe_router(r, g_ffn, router_w, router_b)
    dest, block_expert, n_active = _slot_layout(idx, rank, counts, n_blocks)
    xs = moe_dispatch(r, dest, n_blocks * MOE_ROWS)
    ys = moe_experts(xs, g_ffn, block_expert, n_active, w_gate.astype(BF16), b_gate.astype(F32),
                     w_up.astype(BF16), b_up.astype(F32), w_down.astype(BF16), b_down.astype(F32))
    return moe_combine_embed(ys, dest, gates, r, p, g_ple, w_ple.astype(BF16), w_ple_gate.astype(BF16),
                             g_final, final_norm)


def kernel(x, p, norm_mix, norm_ffn, norm_ple, norm_final, w_in_ab, ssm_a_re, ssm_a_im, ssm_b_re, ssm_b_im, ssm_c_re, ssm_c_im, ssm_d, ssm_log_step, glu_w, glu_b, w_out_ab, rel_bias, w_in_c, w_out_c, router_w, router_b, w_gate, b_gate, w_up, b_up, w_down, b_down, w_ple, w_ple_gate):
    bsz, seq, d = x.shape
    n = bsz * seq
    depth = p.shape[0]
    assert bsz == SUBLANES
    r = x.reshape(n, d).astype(F32)
    for i in range(depth):
        j = i // 2
        if i % 2 == 0:
            ssm_w = ssm_d.shape[1] * ssm_d.shape[2]
            moba_w = (w_in_ab.shape[2] - ssm_w) // 3
            u, qkv = norm_matmul(r, norm_mix[i], w_in_ab[j].astype(BF16),
                                 (ssm_w, 3 * moba_w), (F32, BF16))
            a_mat, b_mat, c_mat = _s5_params(ssm_a_re[j], ssm_a_im[j], ssm_b_re[j], ssm_b_im[j],
                                             ssm_c_re[j], ssm_c_im[j], ssm_log_step[j])
            u_tb = u.reshape(bsz, seq, ssm_w).transpose(1, 0, 2).reshape(n, ssm_w)
            y_a = s5_mixer(u_tb, a_mat, b_mat, c_mat, ssm_d[j].reshape(1, ssm_w).astype(F32),
                           glu_w[j].astype(BF16), glu_b[j].reshape(1, ssm_w).astype(F32))
            y_a = y_a.reshape(seq, bsz, ssm_w).transpose(1, 0, 2).reshape(n, ssm_w)
            y_b = moba_mixer(qkv.reshape(bsz, seq, 3 * moba_w), rel_bias, moba_w).reshape(n, moba_w)
            w_out = w_out_ab[j].astype(BF16)
            r = matmul_residual(r, [y_a, y_b], [w_out[:ssm_w], w_out[ssm_w:]])
        else:
            (qkv,) = norm_matmul(r, norm_mix[i], w_in_c[j].astype(BF16), (3 * d,), (BF16,))
            y_c = stick_breaking_mixer(qkv.reshape(bsz, seq, 3 * d), d).reshape(n, d)
            r = matmul_residual(r, [y_c], [w_out_c[j].astype(BF16)])
        r = moe_and_embed(r, p[i].reshape(n, -1), norm_ffn[i], router_w[i], router_b[i],
                          w_gate[i], b_gate[i], w_up[i], b_up[i], w_down[i], b_down[i],
                          norm_ple[i], w_ple[i], w_ple_gate[i], norm_final, i == depth - 1)
    return r.reshape(bsz, seq, d).astype(x.dtype)
```

```python
import functools
import math

import jax
import jax.numpy as jnp
from jax import lax
from jax.experimental import pallas as pl
from jax.experimental.pallas import tpu as pltpu

F32 = jnp.float32
BF16 = jnp.bfloat16
I32 = jnp.int32

RMS_EPS = 1e-6
HEAD_DIM = 64
LANES = 128
SUBLANES = 8
SSM_GROUP = 16
SSM_STATE = 64
SSM_SLAB_GROUPS = LANES // SSM_GROUP
MOBA_BLOCK = 256
MOBA_TOPK = 3
REL_BUCKETS = 32
REL_MAX_DIST = 128
SB_BLOCK = 256
N_EXPERTS = 32
TOP_K = 4
SWIGLU_LIMIT = 7.0
SWIGLU_ALPHA = 1.702
MOE_ROWS = 256
NEG = -1e30
MIB = 1024 * 1024

_NT = (((1,), (1,)), ((), ()))


def _cparams(n_axes, vmem_mib):
    return pltpu.CompilerParams(
        dimension_semantics=("arbitrary",) * n_axes, vmem_limit_bytes=vmem_mib * MIB)


def _rms(x, g):
    ms = jnp.mean(x * x, axis=-1, keepdims=True)
    return x * lax.rsqrt(ms + RMS_EPS) * g


def _dot(a, b):
    return jnp.dot(a, b, preferred_element_type=F32)


def _dot_nt(a, b):
    return lax.dot_general(a, b, _NT, preferred_element_type=F32)


def _norm_matmul_kernel(x_ref, g_ref, w_ref, *o_refs, splits):
    h = _rms(x_ref[...], g_ref[...]).astype(BF16)
    y = _dot(h, w_ref[...])
    off = 0
    for o_ref, s in zip(o_refs, splits):
        o_ref[...] = y[:, off:off + s].astype(o_ref.dtype)
        off += s


def norm_matmul(x, g, w, splits, dtypes, tm=512):
    n, d = x.shape
    nout = w.shape[1]
    return pl.pallas_call(
        functools.partial(_norm_matmul_kernel, splits=splits),
        grid=(n // tm,),
        in_specs=[pl.BlockSpec((tm, d), lambda i: (i, 0)),
                  pl.BlockSpec((1, d), lambda i: (0, 0)),
                  pl.BlockSpec((d, nout), lambda i: (0, 0))],
        out_specs=[pl.BlockSpec((tm, s), lambda i: (i, 0)) for s in splits],
        out_shape=[jax.ShapeDtypeStruct((n, s), dt) for s, dt in zip(splits, dtypes)],
        compiler_params=_cparams(1, 48),
        name="norm_matmul",
    )(x, g.reshape(1, d), w)


def _matmul_residual_kernel(r_ref, *refs, n_in):
    acc = r_ref[...]
    for a_ref, w_ref in zip(refs[:n_in], refs[n_in:2 * n_in]):
        acc = acc + _dot(a_ref[...], w_ref[...])
    refs[2 * n_in][...] = acc


def matmul_residual(r, a_list, w_list, tm=512):
    n, d = r.shape
    n_in = len(a_list)
    in_specs = [pl.BlockSpec((tm, d), lambda i: (i, 0))]
    in_specs += [pl.BlockSpec((tm, a.shape[1]), lambda i: (i, 0)) for a in a_list]
    in_specs += [pl.BlockSpec(w.shape, lambda i: (0, 0)) for w in w_list]
    return pl.pallas_call(
        functools.partial(_matmul_residual_kernel, n_in=n_in),
        grid=(n // tm,),
        in_specs=in_specs,
        out_specs=pl.BlockSpec((tm, d), lambda i: (i, 0)),
        out_shape=jax.ShapeDtypeStruct((n, d), F32),
        compiler_params=_cparams(1, 32),
        name="matmul_residual",
    )(r, *a_list, *w_list)


def _s5_kernel(u_ref, a_ref, b_ref, c_ref, d_ref, gw_ref, gb_ref, o_ref, s_ref, x_ref,
               *, steps, n_slab):
    half = SSM_SLAB_GROUPS * SSM_STATE
    slab = 2 * half

    @pl.when(pl.program_id(0) == 0)
    def _():
        x_ref[...] = jnp.zeros_like(x_ref)

    u = u_ref[...]
    ub = u.astype(BF16)
    for s in range(n_slab):
        s_ref[:, s * slab:(s + 1) * slab] = _dot(ub[:, s * LANES:(s + 1) * LANES], b_ref[s])

    for s in range(n_slab):
        re = slice(s * slab, s * slab + half)
        im = slice(s * slab + half, (s + 1) * slab)
        ar = a_ref[:, re]
        ai = a_ref[:, im]

        def step(t, carry, re=re, im=im, ar=ar, ai=ai):
            xr, xi = carry
            rows = pl.ds(pl.multiple_of(t * SUBLANES, SUBLANES), SUBLANES)
            nr = ar * xr - ai * xi + s_ref[rows, re]
            ni = ar * xi + ai * xr + s_ref[rows, im]
            s_ref[rows, re] = nr
            s_ref[rows, im] = ni
            return nr, ni

        xr, xi = lax.fori_loop(0, steps, step, (x_ref[:, re], x_ref[:, im]), unroll=8)
        x_ref[:, re] = xr
        x_ref[:, im] = xi

    y = jnp.concatenate(
        [_dot(s_ref[:, s * slab:(s + 1) * slab].astype(BF16), c_ref[s]) for s in range(n_slab)],
        axis=1)
    y = y + d_ref[...] * u
    z = jax.nn.gelu(y)
    gate = jax.nn.sigmoid(_dot(z.astype(BF16), gw_ref[...]) + gb_ref[...])
    o_ref[...] = (z * gate).astype(o_ref.dtype)


def s5_mixer(u_tb, a_mat, b_mat, c_mat, d_vec, glu_w, glu_b, steps=64):
    rows, width = u_tb.shape
    n_slab = width // LANES
    n_state = 2 * n_slab * SSM_SLAB_GROUPS * SSM_STATE
    tm = steps * SUBLANES
    return pl.pallas_call(
        functools.partial(_s5_kernel, steps=steps, n_slab=n_slab),
        grid=(rows // tm,),
        in_specs=[pl.BlockSpec((tm, width), lambda i: (i, 0)),
                  pl.BlockSpec(a_mat.shape, lambda i: (0, 0)),
                  pl.BlockSpec(b_mat.shape, lambda i: (0, 0, 0)),
                  pl.BlockSpec(c_mat.shape, lambda i: (0, 0, 0)),
                  pl.BlockSpec((1, width), lambda i: (0, 0)),
                  pl.BlockSpec(glu_w.shape, lambda i: (0, 0)),
                  pl.BlockSpec((1, width), lambda i: (0, 0))],
        out_specs=pl.BlockSpec((tm, width), lambda i: (i, 0)),
        out_shape=jax.ShapeDtypeStruct((rows, width), BF16),
        scratch_shapes=[pltpu.VMEM((tm, n_state), F32), pltpu.VMEM((SUBLANES, n_state), F32)],
        compiler_params=_cparams(1, 40),
        name="s5_mixer",
    )(u_tb, a_mat, b_mat, c_mat, d_vec, glu_w, glu_b)


def _s5_params(a_re, a_im, b_re, b_im, c_re, c_im, log_step):
    n_grp = a_re.shape[0]
    n_slab = n_grp // SSM_SLAB_GROUPS
    lam = lax.complex(a_re.astype(F32), a_im.astype(F32))
    step = jnp.exp(log_step.astype(F32))[:, None]
    lam_bar = jnp.exp(lam * step)
    b_bar = ((lam_bar - 1.0) / lam)[:, :, None] * lax.complex(b_re.astype(F32), b_im.astype(F32))
    eye = jnp.eye(SSM_SLAB_GROUPS, dtype=F32)

    def slabbed(t):
        return t.reshape((n_slab, SSM_SLAB_GROUPS) + t.shape[1:])

    a_mat = jnp.concatenate(
        [slabbed(jnp.real(lam_bar)).reshape(n_slab, -1), slabbed(jnp.imag(lam_bar)).reshape(n_slab, -1)],
        axis=1).reshape(1, -1)
    a_mat = jnp.broadcast_to(a_mat, (SUBLANES, a_mat.shape[1]))
    b_parts = [jnp.einsum('sgnh,gk->sghkn', slabbed(part(b_bar)), eye)
               for part in (jnp.real, jnp.imag)]
    b_mat = jnp.stack(b_parts, axis=3).reshape(n_slab, LANES, -1)
    c_parts = [jnp.einsum('sghn,gk->sgnkh', slabbed(part), eye)
               for part in (c_re.astype(F32), -c_im.astype(F32))]
    c_mat = jnp.stack(c_parts, axis=1).reshape(n_slab, -1, LANES)
    return a_mat, b_mat.astype(BF16), c_mat.astype(BF16)


def _head_masks(shape):
    lane = lax.broadcasted_iota(I32, shape, 1)
    return lane < HEAD_DIM, lane >= HEAD_DIM


def _moba_kernel(q_ref, k_ref, v_ref, tb_ref, o_ref, km_ref, *, n_blk):
    blk = MOBA_BLOCK
    qb = pl.program_id(2)
    lane = lax.broadcasted_iota(I32, (blk, LANES), 1)
    row = lax.broadcasted_iota(I32, (blk, blk), 0)
    col = lax.broadcasted_iota(I32, (blk, blk), 1)

    @pl.when(qb == 0)
    def _():
        km_ref[...] = jnp.zeros_like(km_ref)
        for n in range(n_blk):
            kb = k_ref[0, n * blk:(n + 1) * blk, :].astype(F32)
            km_ref[n:n + 1, :] = jnp.sum(kb, axis=0, keepdims=True) / blk

    q = q_ref[0]
    own = pl.ds(pl.multiple_of(qb * blk, blk), blk)
    k_own = k_ref[0, own, :]
    v_own = v_ref[0, own, :]
    qss, sels, init = [], [], []
    for hh, hmask in enumerate(_head_masks((blk, LANES))):
        qm = jnp.where(hmask, q, jnp.zeros_like(q))
        gate = lax.dot_general(qm.astype(F32), km_ref[...], _NT, preferred_element_type=F32,
                               precision=lax.Precision.HIGHEST)
        gate = jnp.where(lane < qb, gate, -jnp.inf)
        sel = jnp.zeros((blk, LANES), F32)
        for r in range(MOBA_TOPK):
            top = jnp.max(gate, axis=1, keepdims=True)
            idx = jnp.min(jnp.where(gate == top, lane, LANES), axis=1, keepdims=True)
            hit = lane == idx
            sel = jnp.where(jnp.logical_and(hit, r < qb), 1.0, sel)
            gate = jnp.where(hit, -jnp.inf, gate)

        qs = qm * (HEAD_DIM ** -0.5)
        s = _dot_nt(qs, k_own) + tb_ref[hh, 0]
        s = jnp.where(col <= row, s, NEG)
        m = jnp.max(s, axis=1, keepdims=True)
        p = jnp.exp(s - m)
        l = jnp.sum(p, axis=1, keepdims=True)
        qss.append(qs)
        sels.append(sel)
        init.append((m, l, _dot(p.astype(BF16), v_own)))

    def body(n, carry):
        rows = pl.ds(pl.multiple_of(n * blk, blk), blk)
        kn = k_ref[0, rows, :]
        vn = v_ref[0, rows, :]
        behind = jnp.minimum(qb - n, 2)
        out = []
        for hh, (m, l, acc) in enumerate(carry):
            chosen = jnp.sum(jnp.where(lane == n, sels[hh], 0.0), axis=1, keepdims=True) > 0.0
            s = _dot_nt(qss[hh], kn) + tb_ref[hh, behind]
            s = jnp.where(chosen, s, NEG)
            m_new = jnp.maximum(m, jnp.max(s, axis=1, keepdims=True))
            alpha = jnp.exp(m - m_new)
            p = jnp.exp(s - m_new)
            l = alpha * l + jnp.sum(p, axis=1, keepdims=True)
            acc = alpha * acc + _dot(p.astype(BF16), vn)
            out.append((m_new, l, acc))
        return tuple(out)

    (_, l0, acc0), (_, l1, acc1) = lax.fori_loop(0, qb, body, tuple(init))
    o_ref[0] = jnp.where(lane < HEAD_DIM, acc0 / l0, acc1 / l1).astype(o_ref.dtype)


def _rel_bucket(dist):
    exact = REL_BUCKETS // 2
    n = jnp.maximum(dist, 0)
    nf = jnp.maximum(n, 1).astype(F32)
    log_ratio = jnp.log(nf / exact) / math.log(REL_MAX_DIST / exact)
    large = exact + (log_ratio * (REL_BUCKETS - exact)).astype(I32)
    large = jnp.minimum(large, REL_BUCKETS - 1)
    return jnp.where(n < exact, n, large)


def _moba_bias_tables(rel_bias):
    assert REL_MAX_DIST <= MOBA_BLOCK + 1
    i = jnp.arange(MOBA_BLOCK)[:, None]
    j = jnp.arange(MOBA_BLOCK)[None, :]
    dist = jnp.arange(3)[:, None, None] * MOBA_BLOCK + (i - j)[None]
    onehot = (_rel_bucket(dist)[..., None] == jnp.arange(REL_BUCKETS)).astype(F32)
    return jnp.einsum('oijb,bh->hoij', onehot, rel_bias.astype(F32), precision=lax.Precision.HIGHEST)


def moba_mixer(qkv, rel_bias, width):
    bsz, seq, _ = qkv.shape
    blk = MOBA_BLOCK
    n_blk = seq // blk
    n_pair = width // LANES
    assert seq % blk == 0 and n_blk <= LANES
    tables = _moba_bias_tables(rel_bias)
    return pl.pallas_call(
        functools.partial(_moba_kernel, n_blk=n_blk),
        grid=(bsz, n_pair, n_blk),
        in_specs=[pl.BlockSpec((1, blk, LANES), lambda b, p, i: (b, i, p)),
                  pl.BlockSpec((1, seq, LANES), lambda b, p, i: (b, 0, n_pair + p)),
                  pl.BlockSpec((1, seq, LANES), lambda b, p, i: (b, 0, 2 * n_pair + p)),
                  pl.BlockSpec((2, 3, blk, blk), lambda b, p, i: (p, 0, 0, 0))],
        out_specs=pl.BlockSpec((1, blk, LANES), lambda b, p, i: (b, i, p)),
        out_shape=jax.ShapeDtypeStruct((bsz, seq, width), BF16),
        scratch_shapes=[pltpu.VMEM((LANES, LANES), F32)],
        compiler_params=_cparams(3, 32),
        name="moba_mixer",
    )(qkv, qkv, qkv, tables)


def _sb_kernel(q_ref, k_ref, v_ref, tri_ref, o_ref):
    blk = SB_BLOCK
    qb = pl.program_id(2)
    lane = lax.broadcasted_iota(I32, (blk, LANES), 1)
    row = lax.broadcasted_iota(I32, (blk, blk), 0)
    col = lax.broadcasted_iota(I32, (blk, blk), 1)
    past = col < row
    q = q_ref[0]
    own = pl.ds(pl.multiple_of(qb * blk, blk), blk)

    def block(qs, kn, vn, carried, diagonal):
        z = _dot_nt(qs, kn)
        softplus = jnp.maximum(z, 0.0) + jnp.log(1.0 + jnp.exp(-jnp.abs(z)))
        log_keep = -softplus
        if diagonal:
            log_keep = jnp.where(past, log_keep, 0.0)
        hi = log_keep.astype(BF16)
        lo = (log_keep - hi.astype(F32)).astype(BF16)
        sums = _dot(jnp.concatenate([hi, lo], axis=1), tri_ref[...])
        later = sums[:, :blk]
        total = sums[:, blk:]
        if carried is not None:
            later = later + jnp.concatenate([carried] * (blk // LANES), axis=1)
        w = jnp.exp(z - softplus + later)
        if diagonal:
            w = jnp.where(past, w, 0.0)
        return _dot(w.astype(BF16), vn), total

    qss = [jnp.where(hmask, q, jnp.zeros_like(q)) * (HEAD_DIM ** -0.5)
           for hmask in _head_masks((blk, LANES))]
    k_own = k_ref[0, own, :]
    v_own = v_ref[0, own, :]
    init = tuple(block(qs, k_own, v_own, None, True) for qs in qss)

    def body(i, carry):
        rows = pl.ds(pl.multiple_of((qb - 1 - i) * blk, blk), blk)
        kn = k_ref[0, rows, :]
        vn = v_ref[0, rows, :]
        out = []
        for qs, (acc, carried) in zip(qss, carry):
            pv, total = block(qs, kn, vn, carried, False)
            out.append((acc + pv, carried + total))
        return tuple(out)

    (acc0, _), (acc1, _) = lax.fori_loop(0, qb, body, init)
    o_ref[0] = jnp.where(lane < HEAD_DIM, acc0, acc1).astype(o_ref.dtype)


def stick_breaking_mixer(qkv, width):
    bsz, seq, _ = qkv.shape
    blk = SB_BLOCK
    n_pair = width // LANES
    tri = (jnp.arange(blk)[:, None] > jnp.arange(blk)[None, :]).astype(BF16)
    tri = jnp.concatenate([tri, jnp.ones((blk, LANES), BF16)], axis=1)
    tri = jnp.concatenate([tri, tri], axis=0)
    return pl.pallas_call(
        _sb_kernel,
        grid=(bsz, n_pair, seq // blk),
        in_specs=[pl.BlockSpec((1, blk, LANES), lambda b, p, i: (b, i, p)),
                  pl.BlockSpec((1, seq, LANES), lambda b, p, i: (b, 0, n_pair + p)),
                  pl.BlockSpec((1, seq, LANES), lambda b, p, i: (b, 0, 2 * n_pair + p)),
                  pl.BlockSpec(tri.shape, lambda b, p, i: (0, 0))],
        out_specs=pl.BlockSpec((1, blk, LANES), lambda b, p, i: (b, i, p)),
        out_shape=jax.ShapeDtypeStruct((bsz, seq, width), BF16),
        compiler_params=_cparams(3, 32),
        name="stick_breaking",
    )(qkv, qkv, qkv, tri)


def _router_kernel(r_ref, g_ref, w_ref, b_ref, idx_ref, gate_ref, rank_ref, cnt_ref, run_ref):
    tm = r_ref.shape[0]

    @pl.when(pl.program_id(0) == 0)
    def _():
        run_ref[...] = jnp.zeros_like(run_ref)

    h = _rms(r_ref[...], g_ref[...])
    logits = jnp.dot(h, w_ref[...], preferred_element_type=F32,
                     precision=lax.Precision.HIGHEST) + b_ref[...]
    lane = lax.broadcasted_iota(I32, (tm, LANES), 1)
    tops, hits = [], []
    for _ in range(TOP_K):
        top = jnp.max(logits, axis=1, keepdims=True)
        idx = jnp.min(jnp.where(logits == top, lane, LANES), axis=1, keepdims=True)
        hit = lane == idx
        logits = jnp.where(hit, -jnp.inf, logits)
        tops.append(top)
        hits.append(hit)
    exps = [jnp.exp(t - tops[0]) for t in tops]
    denom = exps[0]
    for e in exps[1:]:
        denom = denom + e

    member = jnp.zeros((tm, LANES), F32)
    for hit in hits:
        member = jnp.where(hit, 1.0, member)
    before = (lax.broadcasted_iota(I32, (tm, tm), 1) < lax.broadcasted_iota(I32, (tm, tm), 0))
    ahead = _dot(before.astype(BF16), member.astype(BF16)) + run_ref[...]

    idx_out = jnp.zeros((tm, LANES), I32)
    gate_out = jnp.zeros((tm, LANES), F32)
    rank_out = jnp.zeros((tm, LANES), F32)
    for k in range(TOP_K):
        idx_k = jnp.sum(jnp.where(hits[k], lane, 0), axis=1, keepdims=True)
        rank_k = jnp.sum(jnp.where(hits[k], ahead, 0.0), axis=1, keepdims=True)
        idx_out = jnp.where(lane == k, idx_k, idx_out)
        gate_out = jnp.where(lane == k, exps[k] / denom, gate_out)
        rank_out = jnp.where(lane == k, rank_k, rank_out)
    idx_ref[...] = idx_out
    gate_ref[...] = gate_out
    rank_ref[...] = rank_out.astype(I32)
    run_ref[...] = run_ref[...] + jnp.sum(member, axis=0, keepdims=True)
    cnt_ref[...] = run_ref[...].astype(I32)


def moe_router(r, g, router_w, router_b, tm=512):
    n, d = r.shape
    n_exp = router_w.shape[1]
    w_pad = jnp.pad(router_w.astype(F32), ((0, 0), (0, LANES - n_exp)))
    b_pad = jnp.pad(router_b.astype(F32).reshape(1, n_exp), ((0, 0), (0, LANES - n_exp)),
                    constant_values=-jnp.inf)
    tile = pl.BlockSpec((tm, LANES), lambda i: (i, 0))
    idx, gates, rank, counts = pl.pallas_call(
        _router_kernel,
        grid=(n // tm,),
        in_specs=[pl.BlockSpec((tm, d), lambda i: (i, 0)),
                  pl.BlockSpec((1, d), lambda i: (0, 0)),
                  pl.BlockSpec((d, LANES), lambda i: (0, 0)),
                  pl.BlockSpec((1, LANES), lambda i: (0, 0))],
        out_specs=[tile, tile, tile, pl.BlockSpec((1, LANES), lambda i: (0, 0))],
        out_shape=[jax.ShapeDtypeStruct((n, LANES), I32), jax.ShapeDtypeStruct((n, LANES), F32),
                   jax.ShapeDtypeStruct((n, LANES), I32), jax.ShapeDtypeStruct((1, LANES), I32)],
        scratch_shapes=[pltpu.VMEM((1, LANES), F32)],
        compiler_params=_cparams(1, 32),
        name="moe_router",
    )(r, g.reshape(1, d), w_pad, b_pad)
    return idx[:, :TOP_K], gates, rank[:, :TOP_K], counts[0, :n_exp]


def _dispatch_kernel(dest_ref, r_ref, init_hbm, xs_hbm, sem, *, tile):
    del init_hbm

    def row_copy(j, slot):
        return pltpu.make_async_copy(r_ref.at[pl.ds(j, 1)], xs_hbm.at[pl.ds(slot, 1)], sem)

    def start(j, c):
        for k in range(TOP_K):
            row_copy(j, dest_ref[0, 0, j * TOP_K + k]).start()
        return c

    def wait(j, c):
        for k in range(TOP_K):
            row_copy(0, 0).wait()
        return c

    lax.fori_loop(0, tile, start, 0)
    lax.fori_loop(0, tile, wait, 0)


def moe_dispatch(r, dest, n_slots, tile=256):
    n, d = r.shape
    dest_tiles = dest.reshape(n // tile, 1, tile * TOP_K)
    return pl.pallas_call(
        functools.partial(_dispatch_kernel, tile=tile),
        grid=(n // tile,),
        in_specs=[pl.BlockSpec((1, 1, tile * TOP_K), lambda i: (i, 0, 0), memory_space=pltpu.SMEM),
                  pl.BlockSpec((tile, d), lambda i: (i, 0)),
                  pl.BlockSpec(memory_space=pl.ANY)],
        out_specs=pl.BlockSpec(memory_space=pl.ANY),
        out_shape=jax.ShapeDtypeStruct((n_slots, d), F32),
        scratch_shapes=[pltpu.SemaphoreType.DMA],
        input_output_aliases={2: 0},
        compiler_params=_cparams(1, 16),
        name="moe_dispatch",
    )(dest_tiles, r, jnp.zeros((n_slots, d), F32))


def _expert_kernel(be_ref, na_ref, x_ref, g_ref, wg_ref, bg_ref, wu_ref, bu_ref, wd_ref, bd_ref, y_ref):
    del be_ref
    active = pl.program_id(0) < na_ref[0]

    @pl.when(jnp.logical_not(active))
    def _():
        y_ref[...] = jnp.zeros_like(y_ref)

    @pl.when(active)
    def _():
        h = _rms(x_ref[...], g_ref[...]).astype(BF16)
        gate = jnp.minimum(_dot(h, wg_ref[0]) + bg_ref[0], SWIGLU_LIMIT)
        up = jnp.clip(_dot(h, wu_ref[0]) + bu_ref[0], -SWIGLU_LIMIT, SWIGLU_LIMIT)
        act = gate * jax.nn.sigmoid(SWIGLU_ALPHA * gate) * (up + 1.0)
        y_ref[...] = _dot(act.astype(BF16), wd_ref[0]) + bd_ref[0]


def moe_experts(xs, g, block_expert, n_active, w_gate, b_gate, w_up, b_up, w_down, b_down):
    n_slots, d = xs.shape
    n_exp, _, d_ff = w_gate.shape
    n_blocks = n_slots // MOE_ROWS

    def rows(i, be, na):
        return (jnp.minimum(i, na[0] - 1), 0)

    def expert3(i, be, na):
        return (be[i], 0, 0)

    grid_spec = pltpu.PrefetchScalarGridSpec(
        num_scalar_prefetch=2,
        grid=(n_blocks,),
        in_specs=[pl.BlockSpec((MOE_ROWS, d), rows),
                  pl.BlockSpec((1, d), lambda i, be, na: (0, 0)),
                  pl.BlockSpec((1, d, d_ff), expert3), pl.BlockSpec((1, 1, d_ff), expert3),
                  pl.BlockSpec((1, d, d_ff), expert3), pl.BlockSpec((1, 1, d_ff), expert3),
                  pl.BlockSpec((1, d_ff, d), expert3), pl.BlockSpec((1, 1, d), expert3)],
        out_specs=pl.BlockSpec((MOE_ROWS, d), lambda i, be, na: (i, 0)))
    return pl.pallas_call(
        _expert_kernel,
        grid_spec=grid_spec,
        out_shape=jax.ShapeDtypeStruct((n_slots, d), F32),
        compiler_params=_cparams(1, 48),
        name="moe_experts",
    )(block_expert, n_active, xs, g.reshape(1, d),
      w_gate, b_gate.reshape(n_exp, 1, d_ff), w_up, b_up.reshape(n_exp, 1, d_ff),
      w_down, b_down.reshape(n_exp, 1, d))


def _combine_kernel(dest_ref, ys_hbm, gate_ref, r_ref, p_ref, gp_ref, wp_ref, wpg_ref, gf_ref,
                    o_ref, buf_ref, sem, *, tile, final_norm):
    def row_copy(slot, k, j):
        return pltpu.make_async_copy(ys_hbm.at[pl.ds(slot, 1)], buf_ref.at[k, pl.ds(j, 1)], sem)

    def start(j, c):
        for k in range(TOP_K):
            row_copy(dest_ref[0, 0, j * TOP_K + k], k, j).start()
        return c

    def wait(j, c):
        for k in range(TOP_K):
            row_copy(0, k, 0).wait()
        return c

    lax.fori_loop(0, tile, start, 0)
    lax.fori_loop(0, tile, wait, 0)

    gates = gate_ref[...]
    r = r_ref[...]
    for k in range(TOP_K):
        r = r + buf_ref[k] * gates[:, k:k + 1]
    h = _rms(r, gp_ref[...]).astype(BF16)
    gate = jax.nn.sigmoid(_dot(h, wpg_ref[...]))
    r = r + _dot(p_ref[...].astype(BF16), wp_ref[...]) * gate
    if final_norm:
        r = _rms(r, gf_ref[...])
    o_ref[...] = r


def moe_combine_embed(ys, dest, gates, r, p, g_ple, w_ple, w_ple_gate, g_final, final_norm, tile=256):
    n, d = r.shape
    pd = p.shape[1]
    dest_tiles = dest.reshape(n // tile, 1, tile * TOP_K)
    return pl.pallas_call(
        functools.partial(_combine_kernel, tile=tile, final_norm=final_norm),
        grid=(n // tile,),
        in_specs=[pl.BlockSpec((1, 1, tile * TOP_K), lambda i: (i, 0, 0), memory_space=pltpu.SMEM),
                  pl.BlockSpec(memory_space=pl.ANY),
                  pl.BlockSpec((tile, LANES), lambda i: (i, 0)),
                  pl.BlockSpec((tile, d), lambda i: (i, 0)),
                  pl.BlockSpec((tile, pd), lambda i: (i, 0)),
                  pl.BlockSpec((1, d), lambda i: (0, 0)),
                  pl.BlockSpec((pd, d), lambda i: (0, 0)),
                  pl.BlockSpec((d, d), lambda i: (0, 0)),
                  pl.BlockSpec((1, d), lambda i: (0, 0))],
        out_specs=pl.BlockSpec((tile, d), lambda i: (i, 0)),
        out_shape=jax.ShapeDtypeStruct((n, d), F32),
        scratch_shapes=[pltpu.VMEM((TOP_K, tile, d), F32), pltpu.SemaphoreType.DMA],
        compiler_params=_cparams(1, 32),
        name="moe_combine_embed",
    )(dest_tiles, ys, gates, r, p, g_ple.reshape(1, d), w_ple, w_ple_gate, g_final.reshape(1, d))


def _slot_layout(idx, rank, counts, n_blocks):
    n_exp = counts.shape[0]
    padded = (counts + MOE_ROWS - 1) // MOE_ROWS * MOE_ROWS
    pad_end = jnp.cumsum(padded)
    pad_start = pad_end - padded
    start_of = jnp.sum(jnp.where(idx[..., None] == jnp.arange(n_exp), pad_start, 0), axis=-1)
    dest = (start_of + rank).astype(I32).reshape(-1)
    block_first_row = jnp.arange(n_blocks) * MOE_ROWS
    block_expert = jnp.minimum(
        jnp.sum(pad_end[None, :] <= block_first_row[:, None], axis=1), n_exp - 1).astype(I32)
    n_active = (pad_end[-1:] // MOE_ROWS).astype(I32)
    return dest, block_expert, n_active


def moe_and_embed(r, p, g_ffn, router_w, router_b, w_gate, b_gate, w_up, b_up, w_down, b_down,
                  g_ple, w_ple, w_ple_gate, g_final, final_norm):
    n, _ = r.shape
    n_blocks = n * TOP_K // MOE_ROWS + router_w.shape[1]
    idx, gates, rank, counts = moe_router(r, g_ffn, router_w, router_b)
    dest, block_expert, n_active = _slot_layout(idx, rank, counts, n_blocks)
    xs = moe_dispatch(r, dest, n_blocks * MOE_ROWS)
    ys = moe_experts(xs, g_ffn, block_expert, n_active, w_gate.astype(BF16), b_gate.astype(F32),
                     w_up.astype(BF16), b_up.astype(F32), w_down.astype(BF16), b_down.astype(F32))
    return moe_combine_embed(ys, dest, gates, r, p, g_ple, w_ple.astype(BF16), w_ple_gate.astype(BF16),
                             g_final, final_norm)


def kernel(x, p, norm_mix, norm_ffn, norm_ple, norm_final, w_in_ab, ssm_a_re, ssm_a_im, ssm_b_re, ssm_b_im, ssm_c_re, ssm_c_im, ssm_d, ssm_log_step, glu_w, glu_b, w_out_ab, rel_bias, w_in_c, w_out_c, router_w, router_b, w_gate, b_gate, w_up, b_up, w_down, b_down, w_ple, w_ple_gate):
    bsz, seq, d = x.shape
    n = bsz * seq
    depth = p.shape[0]
    assert bsz == SUBLANES
    r = x.reshape(n, d).astype(F32)
    for i in range(depth):
        j = i // 2
        if i % 2 == 0:
            ssm_w = ssm_d.shape[1] * ssm_d.shape[2]
            moba_w = (w_in_ab.shape[2] - ssm_w) // 3
            u, qkv = norm_matmul(r, norm_mix[i], w_in_ab[j].astype(BF16),
                                 (ssm_w, 3 * moba_w), (F32, BF16))
            a_mat, b_mat, c_mat = _s5_params(ssm_a_re[j], ssm_a_im[j], ssm_b_re[j], ssm_b_im[j],
                                             ssm_c_re[j], ssm_c_im[j], ssm_log_step[j])
            u_tb = u.reshape(bsz, seq, ssm_w).transpose(1, 0, 2).reshape(n, ssm_w)
            y_a = s5_mixer(u_tb, a_mat, b_mat, c_mat, ssm_d[j].reshape(1, ssm_w).astype(F32),
                           glu_w[j].astype(BF16), glu_b[j].reshape(1, ssm_w).astype(F32))
            y_a = y_a.reshape(seq, bsz, ssm_w).transpose(1, 0, 2).reshape(n, ssm_w)
            y_b = moba_mixer(qkv.reshape(bsz, seq, 3 * moba_w), rel_bias, moba_w).reshape(n, moba_w)
            w_out = w_out_ab[j].astype(BF16)
            r = matmul_residual(r, [y_a, y_b], [w_out[:ssm_w], w_out[ssm_w:]])
        else:
            (qkv,) = norm_matmul(r, norm_mix[i], w_in_c[j].astype(BF16), (3 * d,), (BF16,))
            y_c = stick_breaking_mixer(qkv.reshape(bsz, seq, 3 * d), d).reshape(n, d)
            r = matmul_residual(r, [y_c], [w_out_c[j].astype(BF16)])
        r = moe_and_embed(r, p[i].reshape(n, -1), norm_ffn[i], router_w[i], router_b[i],
                          w_gate[i], b_gate[i], w_up[i], b_up[i], w_down[i], b_down[i],
                          norm_ple[i], w_ple[i], w_ple_gate[i], norm_final, i == depth - 1)
    return r.reshape(bsz, seq, d).astype(x.dtype)
```

```python
import functools
import math

import jax
import jax.numpy as jnp
from jax import lax
from jax.experimental import pallas as pl
from jax.experimental.pallas import tpu as pltpu

F32 = jnp.float32
BF16 = jnp.bfloat16
I32 = jnp.int32

RMS_EPS = 1e-6
HEAD_DIM = 64
LANES = 128
SUBLANES = 8
SSM_GROUP = 16
SSM_STATE = 64
SSM_SLAB_GROUPS = LANES // SSM_GROUP
MOBA_BLOCK = 256
MOBA_TOPK = 3
MOBA_HEADS = 4
REL_BUCKETS = 32
REL_MAX_DIST = 128
SB_BLOCK = 256
SB_HEADS = 4
N_EXPERTS = 32
TOP_K = 4
SWIGLU_LIMIT = 7.0
SWIGLU_ALPHA = 1.702
MOE_ROWS = 256
NEG = -1e30
MIB = 1024 * 1024

_NT = (((1,), (1,)), ((), ()))


def _cparams(n_axes, vmem_mib):
    return pltpu.CompilerParams(
        dimension_semantics=("arbitrary",) * n_axes, vmem_limit_bytes=vmem_mib * MIB)


def _rms(x, g):
    ms = jnp.mean(x * x, axis=-1, keepdims=True)
    return x * lax.rsqrt(ms + RMS_EPS) * g


def _dot(a, b):
    return jnp.dot(a, b, preferred_element_type=F32)


def _dot_nt(a, b):
    return lax.dot_general(a, b, _NT, preferred_element_type=F32)


def _norm_matmul_kernel(x_ref, g_ref, w_ref, *o_refs, splits):
    h = _rms(x_ref[...], g_ref[...]).astype(BF16)
    y = _dot(h, w_ref[...])
    off = 0
    for o_ref, s in zip(o_refs, splits):
        o_ref[...] = y[:, off:off + s].astype(o_ref.dtype)
        off += s


def norm_matmul(x, g, w, splits, dtypes, tm=512):
    n, d = x.shape
    nout = w.shape[1]
    return pl.pallas_call(
        functools.partial(_norm_matmul_kernel, splits=splits),
        grid=(n // tm,),
        in_specs=[pl.BlockSpec((tm, d), lambda i: (i, 0)),
                  pl.BlockSpec((1, d), lambda i: (0, 0)),
                  pl.BlockSpec((d, nout), lambda i: (0, 0))],
        out_specs=[pl.BlockSpec((tm, s), lambda i: (i, 0)) for s in splits],
        out_shape=[jax.ShapeDtypeStruct((n, s), dt) for s, dt in zip(splits, dtypes)],
        compiler_params=_cparams(1, 48),
        name="norm_matmul",
    )(x, g.reshape(1, d), w)


def _matmul_residual_kernel(r_ref, *refs, n_in):
    acc = r_ref[...]
    for a_ref, w_ref in zip(refs[:n_in], refs[n_in:2 * n_in]):
        acc = acc + _dot(a_ref[...], w_ref[...])
    refs[2 * n_in][...] = acc


def matmul_residual(r, a_list, w_list, tm=512):
    n, d = r.shape
    n_in = len(a_list)
    in_specs = [pl.BlockSpec((tm, d), lambda i: (i, 0))]
    in_specs += [pl.BlockSpec((tm, a.shape[1]), lambda i: (i, 0)) for a in a_list]
    in_specs += [pl.BlockSpec(w.shape, lambda i: (0, 0)) for w in w_list]
    return pl.pallas_call(
        functools.partial(_matmul_residual_kernel, n_in=n_in),
        grid=(n // tm,),
        in_specs=in_specs,
        out_specs=pl.BlockSpec((tm, d), lambda i: (i, 0)),
        out_shape=jax.ShapeDtypeStruct((n, d), F32),
        compiler_params=_cparams(1, 32),
        name="matmul_residual",
    )(r, *a_list, *w_list)


def _s5_kernel(u_ref, a_ref, b_ref, c_ref, d_ref, gw_ref, gb_ref, o_ref, s_ref, x_ref,
               *, steps, n_slab):
    half = SSM_SLAB_GROUPS * SSM_STATE
    slab = 2 * half

    @pl.when(pl.program_id(0) == 0)
    def _():
        x_ref[...] = jnp.zeros_like(x_ref)

    u = u_ref[...]
    ub = u.astype(BF16)
    for s in range(n_slab):
        s_ref[:, s * slab:(s + 1) * slab] = _dot(ub[:, s * LANES:(s + 1) * LANES], b_ref[s])

    for s in range(n_slab):
        re = slice(s * slab, s * slab + half)
        im = slice(s * slab + half, (s + 1) * slab)
        ar = a_ref[:, re]
        ai = a_ref[:, im]

        def step(t, carry, re=re, im=im, ar=ar, ai=ai):
            xr, xi = carry
            rows = pl.ds(pl.multiple_of(t * SUBLANES, SUBLANES), SUBLANES)
            nr = ar * xr - ai * xi + s_ref[rows, re]
            ni = ar * xi + ai * xr + s_ref[rows, im]
            s_ref[rows, re] = nr
            s_ref[rows, im] = ni
            return nr, ni

        xr, xi = lax.fori_loop(0, steps, step, (x_ref[:, re], x_ref[:, im]), unroll=8)
        x_ref[:, re] = xr
        x_ref[:, im] = xi

    y = jnp.concatenate(
        [_dot(s_ref[:, s * slab:(s + 1) * slab].astype(BF16), c_ref[s]) for s in range(n_slab)],
        axis=1)
    y = y + d_ref[...] * u
    z = jax.nn.gelu(y)
    gate = jax.nn.sigmoid(_dot(z.astype(BF16), gw_ref[...]) + gb_ref[...])
    o_ref[...] = (z * gate).astype(o_ref.dtype)


def s5_mixer(u_tb, a_mat, b_mat, c_mat, d_vec, glu_w, glu_b, steps=64):
    rows, width = u_tb.shape
    n_slab = width // LANES
    n_state = 2 * n_slab * SSM_SLAB_GROUPS * SSM_STATE
    tm = steps * SUBLANES
    return pl.pallas_call(
        functools.partial(_s5_kernel, steps=steps, n_slab=n_slab),
        grid=(rows // tm,),
        in_specs=[pl.BlockSpec((tm, width), lambda i: (i, 0)),
                  pl.BlockSpec(a_mat.shape, lambda i: (0, 0)),
                  pl.BlockSpec(b_mat.shape, lambda i: (0, 0, 0)),
                  pl.BlockSpec(c_mat.shape, lambda i: (0, 0, 0)),
                  pl.BlockSpec((1, width), lambda i: (0, 0)),
                  pl.BlockSpec(glu_w.shape, lambda i: (0, 0)),
                  pl.BlockSpec((1, width), lambda i: (0, 0))],
        out_specs=pl.BlockSpec((tm, width), lambda i: (i, 0)),
        out_shape=jax.ShapeDtypeStruct((rows, width), BF16),
        scratch_shapes=[pltpu.VMEM((tm, n_state), F32), pltpu.VMEM((SUBLANES, n_state), F32)],
        compiler_params=_cparams(1, 40),
        name="s5_mixer",
    )(u_tb, a_mat, b_mat, c_mat, d_vec, glu_w, glu_b)


def _s5_params(a_re, a_im, b_re, b_im, c_re, c_im, log_step):
    n_grp = a_re.shape[0]
    n_slab = n_grp // SSM_SLAB_GROUPS
    lam = lax.complex(a_re.astype(F32), a_im.astype(F32))
    step = jnp.exp(log_step.astype(F32))[:, None]
    lam_bar = jnp.exp(lam * step)
    b_bar = ((lam_bar - 1.0) / lam)[:, :, None] * lax.complex(b_re.astype(F32), b_im.astype(F32))
    eye = jnp.eye(SSM_SLAB_GROUPS, dtype=F32)

    def slabbed(t):
        return t.reshape((n_slab, SSM_SLAB_GROUPS) + t.shape[1:])

    a_mat = jnp.concatenate(
        [slabbed(jnp.real(lam_bar)).reshape(n_slab, -1), slabbed(jnp.imag(lam_bar)).reshape(n_slab, -1)],
        axis=1).reshape(1, -1)
    a_mat = jnp.broadcast_to(a_mat, (SUBLANES, a_mat.shape[1]))
    b_parts = [jnp.einsum('sgnh,gk->sghkn', slabbed(part(b_bar)), eye)
               for part in (jnp.real, jnp.imag)]
    b_mat = jnp.stack(b_parts, axis=3).reshape(n_slab, LANES, -1)
    c_parts = [jnp.einsum('sghn,gk->sgnkh', slabbed(part), eye)
               for part in (c_re.astype(F32), -c_im.astype(F32))]
    c_mat = jnp.stack(c_parts, axis=1).reshape(n_slab, -1, LANES)
    return a_mat, b_mat.astype(BF16), c_mat.astype(BF16)


def _by_head(lane, cols):
    out = cols[-1]
    for h in reversed(range(len(cols) - 1)):
        out = jnp.where(lane // HEAD_DIM == h, cols[h], out)
    return out


def _moba_kernel(q_ref, k_ref, v_ref, tb_ref, o_ref, km_ref, *, n_blk):
    blk = MOBA_BLOCK
    heads = MOBA_HEADS
    width = heads * HEAD_DIM
    rows_all = heads * blk
    qb = pl.program_id(2)

    @pl.when(qb == 0)
    def _():
        km_ref[...] = jnp.zeros_like(km_ref)
        for n in range(n_blk):
            kb = k_ref[0, n * blk:(n + 1) * blk, :].astype(F32)
            km_ref[n:n + 1, :] = jnp.sum(kb, axis=0, keepdims=True) / blk

    q = q_ref[0]
    lane = lax.broadcasted_iota(I32, (blk, width), 1)
    q_stack = jnp.concatenate(
        [jnp.where(lane // HEAD_DIM == h, q, jnp.zeros_like(q)) for h in range(heads)], axis=0)

    blane = lax.broadcasted_iota(I32, (rows_all, LANES), 1)
    gate = lax.dot_general(q_stack.astype(F32), km_ref[...], _NT, preferred_element_type=F32,
                           precision=lax.Precision.HIGHEST)
    gate = jnp.where(blane < qb, gate, -jnp.inf)
    sel = jnp.zeros((rows_all, LANES), F32)
    for r in range(MOBA_TOPK):
        top = jnp.max(gate, axis=1, keepdims=True)
        idx = jnp.min(jnp.where(gate == top, blane, LANES), axis=1, keepdims=True)
        hit = blane == idx
        sel = jnp.where(jnp.logical_and(hit, r < qb), 1.0, sel)
        gate = jnp.where(hit, -jnp.inf, gate)

    qs = q_stack * (HEAD_DIM ** -0.5)
    row = lax.broadcasted_iota(I32, (rows_all, blk), 0)
    col = lax.broadcasted_iota(I32, (rows_all, blk), 1)

    def rows_of(n):
        return pl.ds(pl.multiple_of(n * blk, blk), blk)

    def weighted_values(p, n):
        vn = v_ref[0, rows_of(n), :]
        p = p.astype(BF16)
        p_cat = jnp.concatenate([p[h * blk:(h + 1) * blk] for h in range(heads)], axis=1)
        v_stack = jnp.concatenate(
            [jnp.where(lane // HEAD_DIM == h, vn, jnp.zeros_like(vn)) for h in range(heads)], axis=0)
        return _dot(p_cat, v_stack)

    def per_head(x):
        return _by_head(lane, [x[h * blk:(h + 1) * blk] for h in range(heads)])

    s = _dot_nt(qs, k_ref[0, rows_of(qb), :]) + tb_ref[0, 0]
    s = jnp.where(col <= (row & (blk - 1)), s, NEG)
    m = jnp.max(s, axis=1, keepdims=True)
    p = jnp.exp(s - m)
    l = jnp.sum(p, axis=1, keepdims=True)
    acc = weighted_values(p, qb)

    def body(n, carry):
        m, l, acc = carry
        chosen = jnp.sum(jnp.where(blane == n, sel, 0.0), axis=1, keepdims=True) > 0.0
        s = _dot_nt(qs, k_ref[0, rows_of(n), :]) + tb_ref[0, jnp.minimum(qb - n, 2)]
        s = jnp.where(chosen, s, NEG)
        m_new = jnp.maximum(m, jnp.max(s, axis=1, keepdims=True))
        alpha = jnp.exp(m - m_new)
        p = jnp.exp(s - m_new)
        l = alpha * l + jnp.sum(p, axis=1, keepdims=True)
        acc = per_head(alpha) * acc + weighted_values(p, n)
        return m_new, l, acc

    _, l, acc = lax.fori_loop(0, qb, body, (m, l, acc))
    o_ref[0] = (acc / per_head(l)).astype(o_ref.dtype)


def _rel_bucket(dist):
    exact = REL_BUCKETS // 2
    n = jnp.maximum(dist, 0)
    nf = jnp.maximum(n, 1).astype(F32)
    log_ratio = jnp.log(nf / exact) / math.log(REL_MAX_DIST / exact)
    large = exact + (log_ratio * (REL_BUCKETS - exact)).astype(I32)
    large = jnp.minimum(large, REL_BUCKETS - 1)
    return jnp.where(n < exact, n, large)


def _moba_bias_tables(rel_bias):
    assert REL_MAX_DIST <= MOBA_BLOCK + 1
    i = jnp.arange(MOBA_BLOCK)[:, None]
    j = jnp.arange(MOBA_BLOCK)[None, :]
    dist = jnp.arange(3)[:, None, None] * MOBA_BLOCK + (i - j)[None]
    onehot = (_rel_bucket(dist)[..., None] == jnp.arange(REL_BUCKETS)).astype(F32)
    return jnp.einsum('oijb,bh->hoij', onehot, rel_bias.astype(F32), precision=lax.Precision.HIGHEST)


def moba_mixer(qkv, rel_bias, width):
    bsz, seq, _ = qkv.shape
    blk = MOBA_BLOCK
    n_blk = seq // blk
    gw = MOBA_HEADS * HEAD_DIM
    n_grp = width // gw
    assert seq % blk == 0 and n_blk <= LANES
    tables = _moba_bias_tables(rel_bias).reshape(n_grp, MOBA_HEADS, 3, blk, blk)
    tables = tables.transpose(0, 2, 1, 3, 4).reshape(n_grp, 3, MOBA_HEADS * blk, blk)
    return pl.pallas_call(
        functools.partial(_moba_kernel, n_blk=n_blk),
        grid=(bsz, n_grp, n_blk),
        in_specs=[pl.BlockSpec((1, blk, gw), lambda b, p, i: (b, i, p)),
                  pl.BlockSpec((1, seq, gw), lambda b, p, i: (b, 0, n_grp + p)),
                  pl.BlockSpec((1, seq, gw), lambda b, p, i: (b, 0, 2 * n_grp + p)),
                  pl.BlockSpec((1, 3, MOBA_HEADS * blk, blk), lambda b, p, i: (p, 0, 0, 0))],
        out_specs=pl.BlockSpec((1, blk, gw), lambda b, p, i: (b, i, p)),
        out_shape=jax.ShapeDtypeStruct((bsz, seq, width), BF16),
        scratch_shapes=[pltpu.VMEM((LANES, gw), F32)],
        compiler_params=_cparams(3, 48),
        name="moba_mixer",
    )(qkv, qkv, qkv, tables)


def _sb_kernel(q_ref, k_ref, v_ref, tri_ref, o_ref):
    blk = SB_BLOCK
    width = SB_HEADS * HEAD_DIM
    n_sub = blk // LANES
    rows_all = SB_HEADS * blk
    qb = pl.program_id(2)
    q = q_ref[0]
    lane = lax.broadcasted_iota(I32, (blk, width), 1)
    q_stack = jnp.concatenate(
        [jnp.where(lane // HEAD_DIM == h, q, jnp.zeros_like(q)) for h in range(SB_HEADS)],
        axis=0)
    row = lax.broadcasted_iota(I32, (rows_all, blk), 0)
    col = lax.broadcasted_iota(I32, (rows_all, blk), 1)
    past = col < (row & (blk - 1))

    def rows_of(n):
        return pl.ds(pl.multiple_of(n * blk, blk), blk)

    def logits(n):
        return _dot_nt(q_stack, k_ref[0, rows_of(n), :])

    def weights(z, carried, diagonal):
        neg_abs = lax.bitcast_convert_type(
            lax.bitcast_convert_type(z, jnp.uint32) | jnp.uint32(0x80000000), F32)
        drop = jnp.maximum(z, 0.0) + jnp.log2(1.0 + jnp.exp2(neg_abs))
        if diagonal:
            drop = jnp.where(past, drop, 0.0)
        hi32 = lax.bitcast_convert_type(
            lax.bitcast_convert_type(drop, jnp.uint32) & jnp.uint32(0xFFFF0000), F32)
        hi = hi32.astype(BF16)
        lo = (drop - hi32).astype(BF16)
        lhs = jnp.concatenate(
            [jnp.concatenate([hi[:, c * LANES:(c + 1) * LANES], lo[:, c * LANES:(c + 1) * LANES]], axis=1)
             for c in range(n_sub)], axis=0)
        sums = _dot(lhs, tri_ref[...])
        newer = carried
        later = [None] * n_sub
        for c in reversed(range(n_sub)):
            within = sums[c * rows_all:(c + 1) * rows_all, :LANES]
            total = sums[c * rows_all:(c + 1) * rows_all, LANES:]
            later[c] = within if newer is None else within + newer
            newer = total if newer is None else newer + total
        w = jnp.exp2(z - drop - jnp.concatenate(later, axis=1))
        if diagonal:
            w = jnp.where(past, w, 0.0)
        w = w.astype(BF16)
        return jnp.concatenate([w[h * blk:(h + 1) * blk] for h in range(SB_HEADS)], axis=1), newer

    def weighted_values(w_cat, n):
        vn = v_ref[0, rows_of(n), :]
        v_stack = jnp.concatenate(
            [jnp.where(lane // HEAD_DIM == h, vn, jnp.zeros_like(vn)) for h in range(SB_HEADS)], axis=0)
        return _dot(w_cat, v_stack)

    w_cat, carried = weights(logits(qb), None, True)
    acc = weighted_values(w_cat, qb)

    def body(i, carry):
        acc, carried = carry
        n = qb - 1 - i
        w_cat, carried = weights(logits(n), carried, False)
        return acc + weighted_values(w_cat, n), carried

    acc, _ = lax.fori_loop(0, qb, body, (acc, carried))
    o_ref[0] = acc.astype(o_ref.dtype)


def stick_breaking_mixer(qkv, width):
    bsz, seq, _ = qkv.shape
    blk = SB_BLOCK
    gw = SB_HEADS * HEAD_DIM
    n_grp = width // gw
    tri = (jnp.arange(LANES)[:, None] > jnp.arange(LANES)[None, :]).astype(BF16)
    tri = jnp.concatenate([tri, jnp.ones((LANES, LANES), BF16)], axis=1)
    tri = jnp.concatenate([tri, tri], axis=0)
    return pl.pallas_call(
        _sb_kernel,
        grid=(bsz, n_grp, seq // blk),
        in_specs=[pl.BlockSpec((1, blk, gw), lambda b, p, i: (b, i, p)),
                  pl.BlockSpec((1, seq, gw), lambda b, p, i: (b, 0, n_grp + p)),
                  pl.BlockSpec((1, seq, gw), lambda b, p, i: (b, 0, 2 * n_grp + p)),
                  pl.BlockSpec(tri.shape, lambda b, p, i: (0, 0))],
        out_specs=pl.BlockSpec((1, blk, gw), lambda b, p, i: (b, i, p)),
        out_shape=jax.ShapeDtypeStruct((bsz, seq, width), BF16),
        compiler_params=_cparams(3, 48),
        name="stick_breaking",
    )(qkv, qkv, qkv, tri)


def _router_kernel(r_ref, g_ref, w_ref, b_ref, idx_ref, gate_ref, rank_ref, cnt_ref, run_ref):
    tm = r_ref.shape[0]

    @pl.when(pl.program_id(0) == 0)
    def _():
        run_ref[...] = jnp.zeros_like(run_ref)

    h = _rms(r_ref[...], g_ref[...])
    logits = jnp.dot(h, w_ref[...], preferred_element_type=F32,
                     precision=lax.Precision.HIGHEST) + b_ref[...]
    lane = lax.broadcasted_iota(I32, (tm, LANES), 1)
    tops, hits = [], []
    for _ in range(TOP_K):
        top = jnp.max(logits, axis=1, keepdims=True)
        idx = jnp.min(jnp.where(logits == top, lane, LANES), axis=1, keepdims=True)
        hit = lane == idx
        logits = jnp.where(hit, -jnp.inf, logits)
        tops.append(top)
        hits.append(hit)
    exps = [jnp.exp(t - tops[0]) for t in tops]
    denom = exps[0]
    for e in exps[1:]:
        denom = denom + e

    member = jnp.zeros((tm, LANES), F32)
    for hit in hits:
        member = jnp.where(hit, 1.0, member)
    before = (lax.broadcasted_iota(I32, (tm, tm), 1) < lax.broadcasted_iota(I32, (tm, tm), 0))
    ahead = _dot(before.astype(BF16), member.astype(BF16)) + run_ref[...]

    idx_out = jnp.zeros((tm, LANES), I32)
    gate_out = jnp.zeros((tm, LANES), F32)
    rank_out = jnp.zeros((tm, LANES), F32)
    for k in range(TOP_K):
        idx_k = jnp.sum(jnp.where(hits[k], lane, 0), axis=1, keepdims=True)
        rank_k = jnp.sum(jnp.where(hits[k], ahead, 0.0), axis=1, keepdims=True)
        idx_out = jnp.where(lane == k, idx_k, idx_out)
        gate_out = jnp.where(lane == k, exps[k] / denom, gate_out)
        rank_out = jnp.where(lane == k, rank_k, rank_out)
    idx_ref[...] = idx_out
    gate_ref[...] = gate_out
    rank_ref[...] = rank_out.astype(I32)
    run_ref[...] = run_ref[...] + jnp.sum(member, axis=0, keepdims=True)
    cnt_ref[...] = run_ref[...].astype(I32)


def moe_router(r, g, router_w, router_b, tm=512):
    n, d = r.shape
    n_exp = router_w.shape[1]
    w_pad = jnp.pad(router_w.astype(F32), ((0, 0), (0, LANES - n_exp)))
    b_pad = jnp.pad(router_b.astype(F32).reshape(1, n_exp), ((0, 0), (0, LANES - n_exp)),
                    constant_values=-jnp.inf)
    tile = pl.BlockSpec((tm, LANES), lambda i: (i, 0))
    idx, gates, rank, counts = pl.pallas_call(
        _router_kernel,
        grid=(n // tm,),
        in_specs=[pl.BlockSpec((tm, d), lambda i: (i, 0)),
                  pl.BlockSpec((1, d), lambda i: (0, 0)),
                  pl.BlockSpec((d, LANES), lambda i: (0, 0)),
                  pl.BlockSpec((1, LANES), lambda i: (0, 0))],
        out_specs=[tile, tile, tile, pl.BlockSpec((1, LANES), lambda i: (0, 0))],
        out_shape=[jax.ShapeDtypeStruct((n, LANES), I32), jax.ShapeDtypeStruct((n, LANES), F32),
                   jax.ShapeDtypeStruct((n, LANES), I32), jax.ShapeDtypeStruct((1, LANES), I32)],
        scratch_shapes=[pltpu.VMEM((1, LANES), F32)],
        compiler_params=_cparams(1, 32),
        name="moe_router",
    )(r, g.reshape(1, d), w_pad, b_pad)
    return idx[:, :TOP_K], gates, rank[:, :TOP_K], counts[0, :n_exp]


def _dispatch_kernel(dest_ref, r_ref, init_hbm, xs_hbm, sem, *, tile):
    del init_hbm

    def row_copy(j, slot):
        return pltpu.make_async_copy(r_ref.at[pl.ds(j, 1)], xs_hbm.at[pl.ds(slot, 1)], sem)

    def start(j, c):
        for k in range(TOP_K):
            row_copy(j, dest_ref[0, 0, j * TOP_K + k]).start()
        return c

    def wait(j, c):
        for k in range(TOP_K):
            row_copy(0, 0).wait()
        return c

    lax.fori_loop(0, tile, start, 0)
    lax.fori_loop(0, tile, wait, 0)


def moe_dispatch(r, dest, n_slots, tile=256):
    n, d = r.shape
    dest_tiles = dest.reshape(n // tile, 1, tile * TOP_K)
    return pl.pallas_call(
        functools.partial(_dispatch_kernel, tile=tile),
        grid=(n // tile,),
        in_specs=[pl.BlockSpec((1, 1, tile * TOP_K), lambda i: (i, 0, 0), memory_space=pltpu.SMEM),
                  pl.BlockSpec((tile, d), lambda i: (i, 0)),
                  pl.BlockSpec(memory_space=pl.ANY)],
        out_specs=pl.BlockSpec(memory_space=pl.ANY),
        out_shape=jax.ShapeDtypeStruct((n_slots, d), F32),
        scratch_shapes=[pltpu.SemaphoreType.DMA],
        input_output_aliases={2: 0},
        compiler_params=_cparams(1, 16),
        name="moe_dispatch",
    )(dest_tiles, r, jnp.zeros((n_slots, d), F32))


def _expert_kernel(be_ref, na_ref, x_ref, g_ref, wg_ref, bg_ref, wu_ref, bu_ref, wd_ref, bd_ref, y_ref):
    del be_ref
    active = pl.program_id(0) < na_ref[0]

    @pl.when(jnp.logical_not(active))
    def _():
        y_ref[...] = jnp.zeros_like(y_ref)

    @pl.when(active)
    def _():
        h = _rms(x_ref[...], g_ref[...]).astype(BF16)
        gate = jnp.minimum(_dot(h, wg_ref[0]) + bg_ref[0], SWIGLU_LIMIT)
        up = jnp.clip(_dot(h, wu_ref[0]) + bu_ref[0], -SWIGLU_LIMIT, SWIGLU_LIMIT)
        act = gate * jax.nn.sigmoid(SWIGLU_ALPHA * gate) * (up + 1.0)
        y_ref[...] = _dot(act.astype(BF16), wd_ref[0]) + bd_ref[0]


def moe_experts(xs, g, block_expert, n_active, w_gate, b_gate, w_up, b_up, w_down, b_down):
    n_slots, d = xs.shape
    n_exp, _, d_ff = w_gate.shape
    n_blocks = n_slots // MOE_ROWS

    def rows(i, be, na):
        return (jnp.minimum(i, na[0] - 1), 0)

    def expert3(i, be, na):
        return (be[i], 0, 0)

    grid_spec = pltpu.PrefetchScalarGridSpec(
        num_scalar_prefetch=2,
        grid=(n_blocks,),
        in_specs=[pl.BlockSpec((MOE_ROWS, d), rows),
                  pl.BlockSpec((1, d), lambda i, be, na: (0, 0)),
                  pl.BlockSpec((1, d, d_ff), expert3), pl.BlockSpec((1, 1, d_ff), expert3),
                  pl.BlockSpec((1, d, d_ff), expert3), pl.BlockSpec((1, 1, d_ff), expert3),
                  pl.BlockSpec((1, d_ff, d), expert3), pl.BlockSpec((1, 1, d), expert3)],
        out_specs=pl.BlockSpec((MOE_ROWS, d), lambda i, be, na: (i, 0)))
    return pl.pallas_call(
        _expert_kernel,
        grid_spec=grid_spec,
        out_shape=jax.ShapeDtypeStruct((n_slots, d), F32),
        compiler_params=_cparams(1, 48),
        name="moe_experts",
    )(block_expert, n_active, xs, g.reshape(1, d),
      w_gate, b_gate.reshape(n_exp, 1, d_ff), w_up, b_up.reshape(n_exp, 1, d_ff),
      w_down, b_down.reshape(n_exp, 1, d))


def _combine_kernel(dest_ref, ys_hbm, gate_ref, r_ref, p_ref, gp_ref, wp_ref, wpg_ref, gf_ref,
                    o_ref, buf_ref, sem, *, tile, final_norm):
    def row_copy(slot, k, j):
        return pltpu.make_async_copy(ys_hbm.at[pl.ds(slot, 1)], buf_ref.at[k, pl.ds(j, 1)], sem)

    def start(j, c):
        for k in range(TOP_K):
            row_copy(dest_ref[0, 0, j * TOP_K + k], k, j).start()
        return c

    def wait(j, c):
        for k in range(TOP_K):
            row_copy(0, k, 0).wait()
        return c

    lax.fori_loop(0, tile, start, 0)
    lax.fori_loop(0, tile, wait, 0)

    gates = gate_ref[...]
    r = r_ref[...]
    for k in range(TOP_K):
        r = r + buf_ref[k] * gates[:, k:k + 1]
    h = _rms(r, gp_ref[...]).astype(BF16)
    gate = jax.nn.sigmoid(_dot(h, wpg_ref[...]))
    r = r + _dot(p_ref[...].astype(BF16), wp_ref[...]) * gate
    if final_norm:
        r = _rms(r, gf_ref[...])
    o_ref[...] = r


def moe_combine_embed(ys, dest, gates, r, p, g_ple, w_ple, w_ple_gate, g_final, final_norm, tile=256):
    n, d = r.shape
    pd = p.shape[1]
    dest_tiles = dest.reshape(n // tile, 1, tile * TOP_K)
    return pl.pallas_call(
        functools.partial(_combine_kernel, tile=tile, final_norm=final_norm),
        grid=(n // tile,),
        in_specs=[pl.BlockSpec((1, 1, tile * TOP_K), lambda i: (i, 0, 0), memory_space=pltpu.SMEM),
                  pl.BlockSpec(memory_space=pl.ANY),
                  pl.BlockSpec((tile, LANES), lambda i: (i, 0)),
                  pl.BlockSpec((tile, d), lambda i: (i, 0)),
                  pl.BlockSpec((tile, pd), lambda i: (i, 0)),
                  pl.BlockSpec((1, d), lambda i: (0, 0)),
                  pl.BlockSpec((pd, d), lambda i: (0, 0)),
                  pl.BlockSpec((d, d), lambda i: (0, 0)),
                  pl.BlockSpec((1, d), lambda i: (0, 0))],
        out_specs=pl.BlockSpec((tile, d), lambda i: (i, 0)),
        out_shape=jax.ShapeDtypeStruct((n, d), F32),
        scratch_shapes=[pltpu.VMEM((TOP_K, tile, d), F32), pltpu.SemaphoreType.DMA],
        compiler_params=_cparams(1, 32),
        name="moe_combine_embed",
    )(dest_tiles, ys, gates, r, p, g_ple.reshape(1, d), w_ple, w_ple_gate, g_final.reshape(1, d))


def _slot_layout(idx, rank, counts, n_blocks):
    n_exp = counts.shape[0]
    padded = (counts + MOE_ROWS - 1) // MOE_ROWS * MOE_ROWS
    pad_end = jnp.cumsum(padded)
    pad_start = pad_end - padded
    start_of = jnp.sum(jnp.where(idx[..., None] == jnp.arange(n_exp), pad_start, 0), axis=-1)
    dest = (start_of + rank).astype(I32).reshape(-1)
    block_first_row = jnp.arange(n_blocks) * MOE_ROWS
    block_expert = jnp.minimum(
        jnp.sum(pad_end[None, :] <= block_first_row[:, None], axis=1), n_exp - 1).astype(I32)
    n_active = (pad_end[-1:] // MOE_ROWS).astype(I32)
    return dest, block_expert, n_active


def moe_and_embed(r, p, g_ffn, router_w, router_b, w_gate, b_gate, w_up, b_up, w_down, b_down,
                  g_ple, w_ple, w_ple_gate, g_final, final_norm):
    n, _ = r.shape
    n_blocks = n * TOP_K // MOE_ROWS + router_w.shape[1]
    idx, gates, rank, counts = moe_router(r, g_ffn, router_w, router_b)
    dest, block_expert, n_active = _slot_layout(idx, rank, counts, n_blocks)
    xs = moe_dispatch(r, dest, n_blocks * MOE_ROWS)
    ys = moe_experts(xs, g_ffn, block_expert, n_active, w_gate.astype(BF16), b_gate.astype(F32),
                     w_up.astype(BF16), b_up.astype(F32), w_down.astype(BF16), b_down.astype(F32))
    return moe_combine_embed(ys, dest, gates, r, p, g_ple, w_ple.astype(BF16), w_ple_gate.astype(BF16),
                             g_final, final_norm)


def kernel(x, p, norm_mix, norm_ffn, norm_ple, norm_final, w_in_ab, ssm_a_re, ssm_a_im, ssm_b_re, ssm_b_im, ssm_c_re, ssm_c_im, ssm_d, ssm_log_step, glu_w, glu_b, w_out_ab, rel_bias, w_in_c, w_out_c, router_w, router_b, w_gate, b_gate, w_up, b_up, w_down, b_down, w_ple, w_ple_gate):
    bsz, seq, d = x.shape
    n = bsz * seq
    depth = p.shape[0]
    assert bsz == SUBLANES
    r = x.reshape(n, d).astype(F32)
    for i in range(depth):
        j = i // 2
        if i % 2 == 0:
            ssm_w = ssm_d.shape[1] * ssm_d.shape[2]
            moba_w = (w_in_ab.shape[2] - ssm_w) // 3
            u, qkv = norm_matmul(r, norm_mix[i], w_in_ab[j].astype(BF16),
                                 (ssm_w, 3 * moba_w), (F32, BF16))
            a_mat, b_mat, c_mat = _s5_params(ssm_a_re[j], ssm_a_im[j], ssm_b_re[j], ssm_b_im[j],
                                             ssm_c_re[j], ssm_c_im[j], ssm_log_step[j])
            u_tb = u.reshape(bsz, seq, ssm_w).transpose(1, 0, 2).reshape(n, ssm_w)
            y_a = s5_mixer(u_tb, a_mat, b_mat, c_mat, ssm_d[j].reshape(1, ssm_w).astype(F32),
                           glu_w[j].astype(BF16), glu_b[j].reshape(1, ssm_w).astype(F32))
            y_a = y_a.reshape(seq, bsz, ssm_w).transpose(1, 0, 2).reshape(n, ssm_w)
            y_b = moba_mixer(qkv.reshape(bsz, seq, 3 * moba_w), rel_bias, moba_w).reshape(n, moba_w)
            w_out = w_out_ab[j].astype(BF16)
            r = matmul_residual(r, [y_a, y_b], [w_out[:ssm_w], w_out[ssm_w:]])
        else:
            q_scale = jnp.where(jnp.arange(3 * d) < d, HEAD_DIM ** -0.5 * math.log2(math.e), 1.0)
            w_in = (w_in_c[j].astype(F32) * q_scale).astype(BF16)
            (qkv,) = norm_matmul(r, norm_mix[i], w_in, (3 * d,), (BF16,))
            y_c = stick_breaking_mixer(qkv.reshape(bsz, seq, 3 * d), d).reshape(n, d)
            r = matmul_residual(r, [y_c], [w_out_c[j].astype(BF16)])
        r = moe_and_embed(r, p[i].reshape(n, -1), norm_ffn[i], router_w[i], router_b[i],
                          w_gate[i], b_gate[i], w_up[i], b_up[i], w_down[i], b_down[i],
                          norm_ple[i], w_ple[i], w_ple_gate[i], norm_final, i == depth - 1)
    return r.reshape(bsz, seq, d).astype(x.dtype)
```

```python
import functools
import math

import jax
import jax.numpy as jnp
from jax import lax
from jax.experimental import pallas as pl
from jax.experimental.pallas import tpu as pltpu

F32 = jnp.float32
BF16 = jnp.bfloat16
I32 = jnp.int32

RMS_EPS = 1e-6
HEAD_DIM = 64
LANES = 128
SUBLANES = 8
SSM_GROUP = 16
SSM_STATE = 64
SSM_SLAB_GROUPS = LANES // SSM_GROUP
MOBA_BLOCK = 256
MOBA_TOPK = 3
MOBA_HEADS = 4
REL_BUCKETS = 32
REL_MAX_DIST = 128
SB_BLOCK = 256
SB_HEADS = 4
N_EXPERTS = 32
TOP_K = 4
SWIGLU_LIMIT = 7.0
SWIGLU_ALPHA = 1.702
MOE_ROWS = 256
NEG = -1e30
MIB = 1024 * 1024

_NT = (((1,), (1,)), ((), ()))


def _cparams(n_axes, vmem_mib):
    return pltpu.CompilerParams(
        dimension_semantics=("arbitrary",) * n_axes, vmem_limit_bytes=vmem_mib * MIB)


def _rms(x, g):
    ms = jnp.mean(x * x, axis=-1, keepdims=True)
    return x * lax.rsqrt(ms + RMS_EPS) * g


def _dot(a, b):
    return jnp.dot(a, b, preferred_element_type=F32)


def _dot_nt(a, b):
    return lax.dot_general(a, b, _NT, preferred_element_type=F32)


def _norm_matmul_kernel(x_ref, g_ref, w_ref, *o_refs, splits):
    h = _rms(x_ref[...], g_ref[...]).astype(BF16)
    y = _dot(h, w_ref[...])
    off = 0
    for o_ref, s in zip(o_refs, splits):
        o_ref[...] = y[:, off:off + s].astype(o_ref.dtype)
        off += s


def norm_matmul(x, g, w, splits, dtypes, tm=512):
    n, d = x.shape
    nout = w.shape[1]
    return pl.pallas_call(
        functools.partial(_norm_matmul_kernel, splits=splits),
        grid=(n // tm,),
        in_specs=[pl.BlockSpec((tm, d), lambda i: (i, 0)),
                  pl.BlockSpec((1, d), lambda i: (0, 0)),
                  pl.BlockSpec((d, nout), lambda i: (0, 0))],
        out_specs=[pl.BlockSpec((tm, s), lambda i: (i, 0)) for s in splits],
        out_shape=[jax.ShapeDtypeStruct((n, s), dt) for s, dt in zip(splits, dtypes)],
        compiler_params=_cparams(1, 48),
        name="norm_matmul",
    )(x, g.reshape(1, d), w)


def _matmul_residual_kernel(r_ref, *refs, n_in):
    acc = r_ref[...]
    for a_ref, w_ref in zip(refs[:n_in], refs[n_in:2 * n_in]):
        acc = acc + _dot(a_ref[...], w_ref[...])
    refs[2 * n_in][...] = acc


def matmul_residual(r, a_list, w_list, tm=512):
    n, d = r.shape
    n_in = len(a_list)
    in_specs = [pl.BlockSpec((tm, d), lambda i: (i, 0))]
    in_specs += [pl.BlockSpec((tm, a.shape[1]), lambda i: (i, 0)) for a in a_list]
    in_specs += [pl.BlockSpec(w.shape, lambda i: (0, 0)) for w in w_list]
    return pl.pallas_call(
        functools.partial(_matmul_residual_kernel, n_in=n_in),
        grid=(n // tm,),
        in_specs=in_specs,
        out_specs=pl.BlockSpec((tm, d), lambda i: (i, 0)),
        out_shape=jax.ShapeDtypeStruct((n, d), F32),
        compiler_params=_cparams(1, 32),
        name="matmul_residual",
    )(r, *a_list, *w_list)


def _s5_kernel(u_ref, a_ref, b_ref, c_ref, d_ref, gw_ref, gb_ref, o_ref, s_ref, x_ref,
               *, steps, n_slab):
    half = SSM_SLAB_GROUPS * SSM_STATE
    slab = 2 * half

    @pl.when(pl.program_id(0) == 0)
    def _():
        x_ref[...] = jnp.zeros_like(x_ref)

    u = u_ref[...]
    ub = u.astype(BF16)
    for s in range(n_slab):
        s_ref[:, s * slab:(s + 1) * slab] = _dot(ub[:, s * LANES:(s + 1) * LANES], b_ref[s])

    for s in range(n_slab):
        re = slice(s * slab, s * slab + half)
        im = slice(s * slab + half, (s + 1) * slab)
        ar = a_ref[:, re]
        ai = a_ref[:, im]

        def step(t, carry, re=re, im=im, ar=ar, ai=ai):
            xr, xi = carry
            rows = pl.ds(pl.multiple_of(t * SUBLANES, SUBLANES), SUBLANES)
            nr = ar * xr - ai * xi + s_ref[rows, re]
            ni = ar * xi + ai * xr + s_ref[rows, im]
            s_ref[rows, re] = nr
            s_ref[rows, im] = ni
            return nr, ni

        xr, xi = lax.fori_loop(0, steps, step, (x_ref[:, re], x_ref[:, im]), unroll=8)
        x_ref[:, re] = xr
        x_ref[:, im] = xi

    y = jnp.concatenate(
        [_dot(s_ref[:, s * slab:(s + 1) * slab].astype(BF16), c_ref[s]) for s in range(n_slab)],
        axis=1)
    y = y + d_ref[...] * u
    z = jax.nn.gelu(y)
    gate = jax.nn.sigmoid(_dot(z.astype(BF16), gw_ref[...]) + gb_ref[...])
    o_ref[...] = (z * gate).astype(o_ref.dtype)


def s5_mixer(u_tb, a_mat, b_mat, c_mat, d_vec, glu_w, glu_b, steps=64):
    rows, width = u_tb.shape
    n_slab = width // LANES
    n_state = 2 * n_slab * SSM_SLAB_GROUPS * SSM_STATE
    tm = steps * SUBLANES
    return pl.pallas_call(
        functools.partial(_s5_kernel, steps=steps, n_slab=n_slab),
        grid=(rows // tm,),
        in_specs=[pl.BlockSpec((tm, width), lambda i: (i, 0)),
                  pl.BlockSpec(a_mat.shape, lambda i: (0, 0)),
                  pl.BlockSpec(b_mat.shape, lambda i: (0, 0, 0)),
                  pl.BlockSpec(c_mat.shape, lambda i: (0, 0, 0)),
                  pl.BlockSpec((1, width), lambda i: (0, 0)),
                  pl.BlockSpec(glu_w.shape, lambda i: (0, 0)),
                  pl.BlockSpec((1, width), lambda i: (0, 0))],
        out_specs=pl.BlockSpec((tm, width), lambda i: (i, 0)),
        out_shape=jax.ShapeDtypeStruct((rows, width), BF16),
        scratch_shapes=[pltpu.VMEM((tm, n_state), F32), pltpu.VMEM((SUBLANES, n_state), F32)],
        compiler_params=_cparams(1, 40),
        name="s5_mixer",
    )(u_tb, a_mat, b_mat, c_mat, d_vec, glu_w, glu_b)


def _s5_params(a_re, a_im, b_re, b_im, c_re, c_im, log_step):
    n_grp = a_re.shape[0]
    n_slab = n_grp // SSM_SLAB_GROUPS
    lam = lax.complex(a_re.astype(F32), a_im.astype(F32))
    step = jnp.exp(log_step.astype(F32))[:, None]
    lam_bar = jnp.exp(lam * step)
    b_bar = ((lam_bar - 1.0) / lam)[:, :, None] * lax.complex(b_re.astype(F32), b_im.astype(F32))
    eye = jnp.eye(SSM_SLAB_GROUPS, dtype=F32)

    def slabbed(t):
        return t.reshape((n_slab, SSM_SLAB_GROUPS) + t.shape[1:])

    a_mat = jnp.concatenate(
        [slabbed(jnp.real(lam_bar)).reshape(n_slab, -1), slabbed(jnp.imag(lam_bar)).reshape(n_slab, -1)],
        axis=1).reshape(1, -1)
    a_mat = jnp.broadcast_to(a_mat, (SUBLANES, a_mat.shape[1]))
    b_parts = [jnp.einsum('sgnh,gk->sghkn', slabbed(part(b_bar)), eye)
               for part in (jnp.real, jnp.imag)]
    b_mat = jnp.stack(b_parts, axis=3).reshape(n_slab, LANES, -1)
    c_parts = [jnp.einsum('sghn,gk->sgnkh', slabbed(part), eye)
               for part in (c_re.astype(F32), -c_im.astype(F32))]
    c_mat = jnp.stack(c_parts, axis=1).reshape(n_slab, -1, LANES)
    return a_mat, b_mat.astype(BF16), c_mat.astype(BF16)


def _by_head(lane, cols):
    out = cols[-1]
    for h in reversed(range(len(cols) - 1)):
        out = jnp.where(lane // HEAD_DIM == h, cols[h], out)
    return out


def _moba_kernel(q_ref, k_ref, v_ref, tb_ref, o_ref, km_ref, *, n_blk):
    blk = MOBA_BLOCK
    heads = MOBA_HEADS
    width = heads * HEAD_DIM
    rows_all = heads * blk
    qb = pl.program_id(2)

    @pl.when(qb == 0)
    def _():
        km_ref[...] = jnp.zeros_like(km_ref)
        for n in range(n_blk):
            kb = k_ref[0, n * blk:(n + 1) * blk, :].astype(F32)
            km_ref[n:n + 1, :] = jnp.sum(kb, axis=0, keepdims=True) / blk

    q = q_ref[0]
    lane = lax.broadcasted_iota(I32, (blk, width), 1)
    q_stack = jnp.concatenate(
        [jnp.where(lane // HEAD_DIM == h, q, jnp.zeros_like(q)) for h in range(heads)], axis=0)

    blane = lax.broadcasted_iota(I32, (rows_all, LANES), 1)
    gate = lax.dot_general(q_stack.astype(F32), km_ref[...], _NT, preferred_element_type=F32,
                           precision=lax.Precision.HIGHEST)
    gate = jnp.where(blane < qb, gate, -jnp.inf)
    sel = jnp.zeros((rows_all, LANES), F32)
    for r in range(MOBA_TOPK):
        top = jnp.max(gate, axis=1, keepdims=True)
        idx = jnp.min(jnp.where(gate == top, blane, LANES), axis=1, keepdims=True)
        hit = blane == idx
        sel = jnp.where(jnp.logical_and(hit, r < qb), 1.0, sel)
        gate = jnp.where(hit, -jnp.inf, gate)

    qs = q_stack * (HEAD_DIM ** -0.5)
    row = lax.broadcasted_iota(I32, (rows_all, blk), 0)
    col = lax.broadcasted_iota(I32, (rows_all, blk), 1)

    def rows_of(n):
        return pl.ds(pl.multiple_of(n * blk, blk), blk)

    def weighted_values(p, n):
        vn = v_ref[0, rows_of(n), :]
        p = p.astype(BF16)
        p_cat = jnp.concatenate([p[h * blk:(h + 1) * blk] for h in range(heads)], axis=1)
        v_stack = jnp.concatenate(
            [jnp.where(lane // HEAD_DIM == h, vn, jnp.zeros_like(vn)) for h in range(heads)], axis=0)
        return _dot(p_cat, v_stack)

    def per_head(x):
        return _by_head(lane, [x[h * blk:(h + 1) * blk] for h in range(heads)])

    s = _dot_nt(qs, k_ref[0, rows_of(qb), :]) + tb_ref[0, 0]
    s = jnp.where(col <= (row & (blk - 1)), s, NEG)
    m = jnp.max(s, axis=1, keepdims=True)
    p = jnp.exp(s - m)
    l = jnp.sum(p, axis=1, keepdims=True)
    acc = weighted_values(p, qb)

    def body(n, carry):
        m, l, acc = carry
        chosen = jnp.sum(jnp.where(blane == n, sel, 0.0), axis=1, keepdims=True) > 0.0
        s = _dot_nt(qs, k_ref[0, rows_of(n), :]) + tb_ref[0, jnp.minimum(qb - n, 2)]
        s = jnp.where(chosen, s, NEG)
        m_new = jnp.maximum(m, jnp.max(s, axis=1, keepdims=True))
        alpha = jnp.exp(m - m_new)
        p = jnp.exp(s - m_new)
        l = alpha * l + jnp.sum(p, axis=1, keepdims=True)
        acc = per_head(alpha) * acc + weighted_values(p, n)
        return m_new, l, acc

    _, l, acc = lax.fori_loop(0, qb, body, (m, l, acc))
    o_ref[0] = (acc / per_head(l)).astype(o_ref.dtype)


def _rel_bucket(dist):
    exact = REL_BUCKETS // 2
    n = jnp.maximum(dist, 0)
    nf = jnp.maximum(n, 1).astype(F32)
    log_ratio = jnp.log(nf / exact) / math.log(REL_MAX_DIST / exact)
    large = exact + (log_ratio * (REL_BUCKETS - exact)).astype(I32)
    large = jnp.minimum(large, REL_BUCKETS - 1)
    return jnp.where(n < exact, n, large)


def _moba_bias_tables(rel_bias):
    assert REL_MAX_DIST <= MOBA_BLOCK + 1
    i = jnp.arange(MOBA_BLOCK)[:, None]
    j = jnp.arange(MOBA_BLOCK)[None, :]
    dist = jnp.arange(3)[:, None, None] * MOBA_BLOCK + (i - j)[None]
    onehot = (_rel_bucket(dist)[..., None] == jnp.arange(REL_BUCKETS)).astype(F32)
    return jnp.einsum('oijb,bh->hoij', onehot, rel_bias.astype(F32), precision=lax.Precision.HIGHEST)


def moba_mixer(qkv, rel_bias, width):
    bsz, seq, _ = qkv.shape
    blk = MOBA_BLOCK
    n_blk = seq // blk
    gw = MOBA_HEADS * HEAD_DIM
    n_grp = width // gw
    assert seq % blk == 0 and n_blk <= LANES
    tables = _moba_bias_tables(rel_bias).reshape(n_grp, MOBA_HEADS, 3, blk, blk)
    tables = tables.transpose(0, 2, 1, 3, 4).reshape(n_grp, 3, MOBA_HEADS * blk, blk)
    return pl.pallas_call(
        functools.partial(_moba_kernel, n_blk=n_blk),
        grid=(bsz, n_grp, n_blk),
        in_specs=[pl.BlockSpec((1, blk, gw), lambda b, p, i: (b, i, p)),
                  pl.BlockSpec((1, seq, gw), lambda b, p, i: (b, 0, n_grp + p)),
                  pl.BlockSpec((1, seq, gw), lambda b, p, i: (b, 0, 2 * n_grp + p)),
                  pl.BlockSpec((1, 3, MOBA_HEADS * blk, blk), lambda b, p, i: (p, 0, 0, 0))],
        out_specs=pl.BlockSpec((1, blk, gw), lambda b, p, i: (b, i, p)),
        out_shape=jax.ShapeDtypeStruct((bsz, seq, width), BF16),
        scratch_shapes=[pltpu.VMEM((LANES, gw), F32)],
        compiler_params=_cparams(3, 48),
        name="moba_mixer",
    )(qkv, qkv, qkv, tables)


def _sb_kernel(q_ref, k_ref, v_ref, tri_ref, o_ref):
    blk = SB_BLOCK
    width = SB_HEADS * HEAD_DIM
    n_sub = blk // LANES
    rows_all = SB_HEADS * blk
    qb = pl.program_id(2)
    q = q_ref[0]
    lane = lax.broadcasted_iota(I32, (blk, width), 1)
    q_stack = jnp.concatenate(
        [jnp.where(lane // HEAD_DIM == h, q, jnp.zeros_like(q)) for h in range(SB_HEADS)],
        axis=0)
    row = lax.broadcasted_iota(I32, (rows_all, blk), 0)
    col = lax.broadcasted_iota(I32, (rows_all, blk), 1)
    past = col < (row & (blk - 1))

    def rows_of(n):
        return pl.ds(pl.multiple_of(n * blk, blk), blk)

    def logits(n):
        return _dot_nt(q_stack, k_ref[0, rows_of(n), :])

    def weights(z, carried, diagonal):
        neg_abs = lax.bitcast_convert_type(
            lax.bitcast_convert_type(z, jnp.uint32) | jnp.uint32(0x80000000), F32)
        drop = jnp.maximum(z, 0.0) + jnp.log2(1.0 + jnp.exp2(neg_abs))
        if diagonal:
            drop = jnp.where(past, drop, 0.0)
        hi32 = lax.bitcast_convert_type(
            lax.bitcast_convert_type(drop, jnp.uint32) & jnp.uint32(0xFFFF0000), F32)
        hi = hi32.astype(BF16)
        lo = (drop - hi32).astype(BF16)
        lhs = jnp.concatenate(
            [jnp.concatenate([hi[:, c * LANES:(c + 1) * LANES], lo[:, c * LANES:(c + 1) * LANES]], axis=1)
             for c in range(n_sub)], axis=0)
        sums = _dot(lhs, tri_ref[...])
        newer = carried
        later = [None] * n_sub
        for c in reversed(range(n_sub)):
            within = sums[c * rows_all:(c + 1) * rows_all, :LANES]
            total = sums[c * rows_all:(c + 1) * rows_all, LANES:]
            later[c] = within if newer is None else within + newer
            newer = total if newer is None else newer + total
        w = jnp.exp2(z - drop - jnp.concatenate(later, axis=1))
        if diagonal:
            w = jnp.where(past, w, 0.0)
        w = w.astype(BF16)
        return jnp.concatenate([w[h * blk:(h + 1) * blk] for h in range(SB_HEADS)], axis=1), newer

    def weighted_values(w_cat, n):
        vn = v_ref[0, rows_of(n), :]
        v_stack = jnp.concatenate(
            [jnp.where(lane // HEAD_DIM == h, vn, jnp.zeros_like(vn)) for h in range(SB_HEADS)], axis=0)
        return _dot(w_cat, v_stack)

    w_cat, carried = weights(logits(qb), None, True)
    acc = weighted_values(w_cat, qb)

    def body(i, carry):
        acc, carried = carry
        n = qb - 1 - i
        w_cat, carried = weights(logits(n), carried, False)
        return acc + weighted_values(w_cat, n), carried

    acc, _ = lax.fori_loop(0, qb, body, (acc, carried))
    o_ref[0] = acc.astype(o_ref.dtype)


def stick_breaking_mixer(qkv, width):
    bsz, seq, _ = qkv.shape
    blk = SB_BLOCK
    gw = SB_HEADS * HEAD_DIM
    n_grp = width // gw
    tri = (jnp.arange(LANES)[:, None] > jnp.arange(LANES)[None, :]).astype(BF16)
    tri = jnp.concatenate([tri, jnp.ones((LANES, LANES), BF16)], axis=1)
    tri = jnp.concatenate([tri, tri], axis=0)
    return pl.pallas_call(
        _sb_kernel,
        grid=(bsz, n_grp, seq // blk),
        in_specs=[pl.BlockSpec((1, blk, gw), lambda b, p, i: (b, i, p)),
                  pl.BlockSpec((1, seq, gw), lambda b, p, i: (b, 0, n_grp + p)),
                  pl.BlockSpec((1, seq, gw), lambda b, p, i: (b, 0, 2 * n_grp + p)),
                  pl.BlockSpec(tri.shape, lambda b, p, i: (0, 0))],
        out_specs=pl.BlockSpec((1, blk, gw), lambda b, p, i: (b, i, p)),
        out_shape=jax.ShapeDtypeStruct((bsz, seq, width), BF16),
        compiler_params=_cparams(3, 48),
        name="stick_breaking",
    )(qkv, qkv, qkv, tri)


def _router_kernel(r_ref, g_ref, w_ref, b_ref, idx_ref, gate_ref, rank_ref, cnt_ref, run_ref):
    tm = r_ref.shape[0]

    @pl.when(pl.program_id(0) == 0)
    def _():
        run_ref[...] = jnp.zeros_like(run_ref)

    h = _rms(r_ref[...], g_ref[...])
    logits = jnp.dot(h, w_ref[...], preferred_element_type=F32,
                     precision=lax.Precision.HIGHEST) + b_ref[...]
    lane = lax.broadcasted_iota(I32, (tm, LANES), 1)
    tops, hits = [], []
    for _ in range(TOP_K):
        top = jnp.max(logits, axis=1, keepdims=True)
        idx = jnp.min(jnp.where(logits == top, lane, LANES), axis=1, keepdims=True)
        hit = lane == idx
        logits = jnp.where(hit, -jnp.inf, logits)
        tops.append(top)
        hits.append(hit)
    exps = [jnp.exp(t - tops[0]) for t in tops]
    denom = exps[0]
    for e in exps[1:]:
        denom = denom + e

    member = jnp.zeros((tm, LANES), F32)
    for hit in hits:
        member = jnp.where(hit, 1.0, member)
    before = (lax.broadcasted_iota(I32, (tm, tm), 1) < lax.broadcasted_iota(I32, (tm, tm), 0))
    ahead = _dot(before.astype(BF16), member.astype(BF16)) + run_ref[...]

    idx_out = jnp.zeros((tm, LANES), I32)
    gate_out = jnp.zeros((tm, LANES), F32)
    rank_out = jnp.zeros((tm, LANES), F32)
    for k in range(TOP_K):
        idx_k = jnp.sum(jnp.where(hits[k], lane, 0), axis=1, keepdims=True)
        rank_k = jnp.sum(jnp.where(hits[k], ahead, 0.0), axis=1, keepdims=True)
        idx_out = jnp.where(lane == k, idx_k, idx_out)
        gate_out = jnp.where(lane == k, exps[k] / denom, gate_out)
        rank_out = jnp.where(lane == k, rank_k, rank_out)
    idx_ref[...] = idx_out
    gate_ref[...] = gate_out
    rank_ref[...] = rank_out.astype(I32)
    run_ref[...] = run_ref[...] + jnp.sum(member, axis=0, keepdims=True)
    cnt_ref[...] = run_ref[...].astype(I32)


def moe_router(r, g, router_w, router_b, tm=512):
    n, d = r.shape
    n_exp = router_w.shape[1]
    w_pad = jnp.pad(router_w.astype(F32), ((0, 0), (0, LANES - n_exp)))
    b_pad = jnp.pad(router_b.astype(F32).reshape(1, n_exp), ((0, 0), (0, LANES - n_exp)),
                    constant_values=-jnp.inf)
    tile = pl.BlockSpec((tm, LANES), lambda i: (i, 0))
    idx, gates, rank, counts = pl.pallas_call(
        _router_kernel,
        grid=(n // tm,),
        in_specs=[pl.BlockSpec((tm, d), lambda i: (i, 0)),
                  pl.BlockSpec((1, d), lambda i: (0, 0)),
                  pl.BlockSpec((d, LANES), lambda i: (0, 0)),
                  pl.BlockSpec((1, LANES), lambda i: (0, 0))],
        out_specs=[tile, tile, tile, pl.BlockSpec((1, LANES), lambda i: (0, 0))],
        out_shape=[jax.ShapeDtypeStruct((n, LANES), I32), jax.ShapeDtypeStruct((n, LANES), F32),
                   jax.ShapeDtypeStruct((n, LANES), I32), jax.ShapeDtypeStruct((1, LANES), I32)],
        scratch_shapes=[pltpu.VMEM((1, LANES), F32)],
        compiler_params=_cparams(1, 32),
        name="moe_router",
    )(r, g.reshape(1, d), w_pad, b_pad)
    return idx[:, :TOP_K], gates, rank[:, :TOP_K], counts[0, :n_exp]


def _expert_kernel(be_ref, na_ref, tok_ref, tok_next_ref, r_hbm, g_ref, wg_ref, bg_ref, wu_ref, bu_ref,
                   wd_ref, bd_ref, y_ref, buf_ref, wg_bf, wu_bf, wd_bf, sem):
    b = pl.program_id(0)
    n_active = na_ref[0]
    cur = lax.rem(b, 2)

    def row_copy(tok, buf, j):
        return pltpu.make_async_copy(r_hbm.at[pl.ds(tok, 1)], buf_ref.at[buf, pl.ds(j, 1)], sem.at[buf])

    def start_gather(toks, buf):
        def start(j, c):
            row_copy(toks[0, 0, j], buf, j).start()
            return c
        lax.fori_loop(0, MOE_ROWS, start, 0, unroll=8)

    @pl.when(jnp.logical_and(b == 0, n_active > 0))
    def _():
        start_gather(tok_ref, 0)

    @pl.when(b + 1 < n_active)
    def _():
        start_gather(tok_next_ref, 1 - cur)

    @pl.when(b >= n_active)
    def _():
        y_ref[...] = jnp.zeros_like(y_ref)

    @pl.when(b < n_active)
    def _():
        @pl.when(jnp.logical_or(b == 0, be_ref[b] != be_ref[jnp.maximum(b - 1, 0)]))
        def _():
            wg_bf[...] = wg_ref[0].astype(BF16)
            wu_bf[...] = wu_ref[0].astype(BF16)
            wd_bf[...] = wd_ref[0].astype(BF16)

        def wait(j, c):
            row_copy(0, cur, 0).wait()
            return c
        lax.fori_loop(0, MOE_ROWS, wait, 0, unroll=8)

        h = _rms(buf_ref[cur], g_ref[...]).astype(BF16)
        gate = jnp.minimum(_dot(h, wg_bf[...]) + bg_ref[0], SWIGLU_LIMIT)
        up = jnp.clip(_dot(h, wu_bf[...]) + bu_ref[0], -SWIGLU_LIMIT, SWIGLU_LIMIT)
        act = gate * jax.nn.sigmoid(SWIGLU_ALPHA * gate) * (up + 1.0)
        y_ref[...] = _dot(act.astype(BF16), wd_bf[...]) + bd_ref[0]


def moe_experts(r, g, slot_tok, block_expert, n_active, w_gate, b_gate, w_up, b_up, w_down, b_down):
    n, d = r.shape
    n_exp, _, d_ff = w_gate.shape
    n_blocks = block_expert.shape[0]
    toks = slot_tok.reshape(n_blocks, 1, MOE_ROWS)

    def expert3(i, be, na):
        return (be[i], 0, 0)

    grid_spec = pltpu.PrefetchScalarGridSpec(
        num_scalar_prefetch=2,
        grid=(n_blocks,),
        in_specs=[pl.BlockSpec((1, 1, MOE_ROWS), lambda i, be, na: (i, 0, 0), memory_space=pltpu.SMEM),
                  pl.BlockSpec((1, 1, MOE_ROWS), lambda i, be, na: (jnp.minimum(i + 1, n_blocks - 1), 0, 0),
                               memory_space=pltpu.SMEM),
                  pl.BlockSpec(memory_space=pl.ANY),
                  pl.BlockSpec((1, d), lambda i, be, na: (0, 0)),
                  pl.BlockSpec((1, d, d_ff), expert3), pl.BlockSpec((1, 1, d_ff), expert3),
                  pl.BlockSpec((1, d, d_ff), expert3), pl.BlockSpec((1, 1, d_ff), expert3),
                  pl.BlockSpec((1, d_ff, d), expert3), pl.BlockSpec((1, 1, d), expert3)],
        out_specs=pl.BlockSpec((MOE_ROWS, d), lambda i, be, na: (i, 0)),
        scratch_shapes=[pltpu.VMEM((2, MOE_ROWS, d), F32),
                        pltpu.VMEM((d, d_ff), BF16), pltpu.VMEM((d, d_ff), BF16), pltpu.VMEM((d_ff, d), BF16),
                        pltpu.SemaphoreType.DMA((2,))])
    return pl.pallas_call(
        _expert_kernel,
        grid_spec=grid_spec,
        out_shape=jax.ShapeDtypeStruct((n_blocks * MOE_ROWS, d), F32),
        compiler_params=_cparams(1, 56),
        name="moe_experts",
    )(block_expert, n_active, toks, toks, r, g.reshape(1, d),
      w_gate, b_gate.reshape(n_exp, 1, d_ff), w_up, b_up.reshape(n_exp, 1, d_ff),
      w_down, b_down.reshape(n_exp, 1, d))


def _combine_kernel(dest_ref, dest_next_ref, ys_hbm, gate_ref, r_ref, p_ref, gp_ref, wp_ref, wpg_ref, gf_ref,
                    o_ref, buf_ref, sem, *, tile, final_norm):
    i = pl.program_id(0)
    cur = lax.rem(i, 2)

    def row_copy(slot, buf, k, j):
        return pltpu.make_async_copy(ys_hbm.at[pl.ds(slot, 1)], buf_ref.at[buf, k, pl.ds(j, 1)], sem.at[buf])

    def start_gather(dests, buf):
        def start(j, c):
            for k in range(TOP_K):
                row_copy(dests[0, 0, j * TOP_K + k], buf, k, j).start()
            return c
        lax.fori_loop(0, tile, start, 0, unroll=2)

    @pl.when(i == 0)
    def _():
        start_gather(dest_ref, 0)

    @pl.when(i + 1 < pl.num_programs(0))
    def _():
        start_gather(dest_next_ref, 1 - cur)

    def wait(j, c):
        for k in range(TOP_K):
            row_copy(0, cur, k, 0).wait()
        return c
    lax.fori_loop(0, tile, wait, 0, unroll=2)

    gates = gate_ref[...]
    r = r_ref[...]
    for k in range(TOP_K):
        r = r + buf_ref[cur, k] * gates[:, k:k + 1]
    h = _rms(r, gp_ref[...]).astype(BF16)
    gate = jax.nn.sigmoid(_dot(h, wpg_ref[...]))
    r = r + _dot(p_ref[...].astype(BF16), wp_ref[...]) * gate
    if final_norm:
        r = _rms(r, gf_ref[...])
    o_ref[...] = r


def moe_combine_embed(ys, dest, gates, r, p, g_ple, w_ple, w_ple_gate, g_final, final_norm, tile=256):
    n, d = r.shape
    pd = p.shape[1]
    n_tiles = n // tile
    dest_tiles = dest.reshape(n_tiles, 1, tile * TOP_K)
    return pl.pallas_call(
        functools.partial(_combine_kernel, tile=tile, final_norm=final_norm),
        grid=(n_tiles,),
        in_specs=[pl.BlockSpec((1, 1, tile * TOP_K), lambda i: (i, 0, 0), memory_space=pltpu.SMEM),
                  pl.BlockSpec((1, 1, tile * TOP_K), lambda i: (jnp.minimum(i + 1, n_tiles - 1), 0, 0),
                               memory_space=pltpu.SMEM),
                  pl.BlockSpec(memory_space=pl.ANY),
                  pl.BlockSpec((tile, LANES), lambda i: (i, 0)),
                  pl.BlockSpec((tile, d), lambda i: (i, 0)),
                  pl.BlockSpec((tile, pd), lambda i: (i, 0)),
                  pl.BlockSpec((1, d), lambda i: (0, 0)),
                  pl.BlockSpec((pd, d), lambda i: (0, 0)),
                  pl.BlockSpec((d, d), lambda i: (0, 0)),
                  pl.BlockSpec((1, d), lambda i: (0, 0))],
        out_specs=pl.BlockSpec((tile, d), lambda i: (i, 0)),
        out_shape=jax.ShapeDtypeStruct((n, d), F32),
        scratch_shapes=[pltpu.VMEM((2, TOP_K, tile, d), F32), pltpu.SemaphoreType.DMA((2,))],
        compiler_params=_cparams(1, 40),
        name="moe_combine_embed",
    )(dest_tiles, dest_tiles, ys, gates, r, p, g_ple.reshape(1, d), w_ple, w_ple_gate, g_final.reshape(1, d))


def _slot_layout(idx, rank, counts, n_blocks):
    n_exp = counts.shape[0]
    padded = (counts + MOE_ROWS - 1) // MOE_ROWS * MOE_ROWS
    pad_end = jnp.cumsum(padded)
    pad_start = pad_end - padded
    cnt_start = jnp.cumsum(counts) - counts
    start_of = jnp.sum(jnp.where(idx[..., None] == jnp.arange(n_exp), pad_start, 0), axis=-1)
    dest = (start_of + rank).astype(I32).reshape(-1)
    block_first_row = jnp.arange(n_blocks) * MOE_ROWS
    block_expert = jnp.minimum(
        jnp.sum(pad_end[None, :] <= block_first_row[:, None], axis=1), n_exp - 1).astype(I32)
    n_active = (pad_end[-1:] // MOE_ROWS).astype(I32)
    sorted_tok = (jnp.argsort(dest) // TOP_K).astype(I32)
    in_expert = block_first_row[:, None] + jnp.arange(MOE_ROWS)[None, :] - pad_start[block_expert][:, None]
    real = in_expert < counts[block_expert][:, None]
    compact = jnp.clip(in_expert + cnt_start[block_expert][:, None], 0, dest.shape[0] - 1)
    slot_tok = jnp.where(real, sorted_tok[compact], 0).astype(I32)
    return dest, slot_tok, block_expert, n_active


def moe_and_embed(r, p, g_ffn, router_w, router_b, w_gate, b_gate, w_up, b_up, w_down, b_down,
                  g_ple, w_ple, w_ple_gate, g_final, final_norm):
    n, _ = r.shape
    n_blocks = n * TOP_K // MOE_ROWS + router_w.shape[1]
    idx, gates, rank, counts = moe_router(r, g_ffn, router_w, router_b)
    dest, slot_tok, block_expert, n_active = _slot_layout(idx, rank, counts, n_blocks)
    ys = moe_experts(r, g_ffn, slot_tok, block_expert, n_active, w_gate.astype(F32), b_gate.astype(F32),
                     w_up.astype(F32), b_up.astype(F32), w_down.astype(F32), b_down.astype(F32))
    return moe_combine_embed(ys, dest, gates, r, p, g_ple, w_ple.astype(BF16), w_ple_gate.astype(BF16),
                             g_final, final_norm)


def kernel(x, p, norm_mix, norm_ffn, norm_ple, norm_final, w_in_ab, ssm_a_re, ssm_a_im, ssm_b_re, ssm_b_im, ssm_c_re, ssm_c_im, ssm_d, ssm_log_step, glu_w, glu_b, w_out_ab, rel_bias, w_in_c, w_out_c, router_w, router_b, w_gate, b_gate, w_up, b_up, w_down, b_down, w_ple, w_ple_gate):
    bsz, seq, d = x.shape
    n = bsz * seq
    depth = p.shape[0]
    assert bsz == SUBLANES
    r = x.reshape(n, d).astype(F32)
    for i in range(depth):
        j = i // 2
        if i % 2 == 0:
            ssm_w = ssm_d.shape[1] * ssm_d.shape[2]
            moba_w = (w_in_ab.shape[2] - ssm_w) // 3
            u, qkv = norm_matmul(r, norm_mix[i], w_in_ab[j].astype(BF16),
                                 (ssm_w, 3 * moba_w), (F32, BF16))
            a_mat, b_mat, c_mat = _s5_params(ssm_a_re[j], ssm_a_im[j], ssm_b_re[j], ssm_b_im[j],
                                             ssm_c_re[j], ssm_c_im[j], ssm_log_step[j])
            u_tb = u.reshape(bsz, seq, ssm_w).transpose(1, 0, 2).reshape(n, ssm_w)
            y_a = s5_mixer(u_tb, a_mat, b_mat, c_mat, ssm_d[j].reshape(1, ssm_w).astype(F32),
                           glu_w[j].astype(BF16), glu_b[j].reshape(1, ssm_w).astype(F32))
            y_a = y_a.reshape(seq, bsz, ssm_w).transpose(1, 0, 2).reshape(n, ssm_w)
            y_b = moba_mixer(qkv.reshape(bsz, seq, 3 * moba_w), rel_bias, moba_w).reshape(n, moba_w)
            w_out = w_out_ab[j].astype(BF16)
            r = matmul_residual(r, [y_a, y_b], [w_out[:ssm_w], w_out[ssm_w:]])
        else:
            q_scale = jnp.where(jnp.arange(3 * d) < d, HEAD_DIM ** -0.5 * math.log2(math.e), 1.0)
            w_in = (w_in_c[j].astype(F32) * q_scale).astype(BF16)
            (qkv,) = norm_matmul(r, norm_mix[i], w_in, (3 * d,), (BF16,))
            y_c = stick_breaking_mixer(qkv.reshape(bsz, seq, 3 * d), d).reshape(n, d)
            r = matmul_residual(r, [y_c], [w_out_c[j].astype(BF16)])
        r = moe_and_embed(r, p[i].reshape(n, -1), norm_ffn[i], router_w[i], router_b[i],
                          w_gate[i], b_gate[i], w_up[i], b_up[i], w_down[i], b_down[i],
                          norm_ple[i], w_ple[i], w_ple_gate[i], norm_final, i == depth - 1)
    return r.reshape(bsz, seq, d).astype(x.dtype)
```

```python
import functools
import math

import jax
import jax.numpy as jnp
from jax import lax
from jax.experimental import pallas as pl
from jax.experimental.pallas import tpu as pltpu

F32 = jnp.float32
BF16 = jnp.bfloat16
I32 = jnp.int32

RMS_EPS = 1e-6
HEAD_DIM = 64
LANES = 128
SUBLANES = 8
SSM_GROUP = 16
SSM_STATE = 64
SSM_SLAB_GROUPS = LANES // SSM_GROUP
MOBA_BLOCK = 256
MOBA_TOPK = 3
MOBA_HEADS = 4
REL_BUCKETS = 32
REL_MAX_DIST = 128
SB_BLOCK = 256
SB_HEADS = 4
N_EXPERTS = 32
TOP_K = 4
SWIGLU_LIMIT = 7.0
SWIGLU_ALPHA = 1.702
MOE_ROWS = 256
NEG = -1e30
MIB = 1024 * 1024

_NT = (((1,), (1,)), ((), ()))


def _cparams(n_axes, vmem_mib):
    return pltpu.CompilerParams(
        dimension_semantics=("arbitrary",) * n_axes, vmem_limit_bytes=vmem_mib * MIB)


def _rms(x, g):
    ms = jnp.mean(x * x, axis=-1, keepdims=True)
    return x * lax.rsqrt(ms + RMS_EPS) * g


def _dot(a, b):
    return jnp.dot(a, b, preferred_element_type=F32)


def _dot_nt(a, b):
    return lax.dot_general(a, b, _NT, preferred_element_type=F32)


def _norm_matmul_kernel(x_ref, g_ref, w_ref, *o_refs, splits):
    h = _rms(x_ref[...], g_ref[...]).astype(BF16)
    y = _dot(h, w_ref[...])
    off = 0
    for o_ref, s in zip(o_refs, splits):
        o_ref[...] = y[:, off:off + s].astype(o_ref.dtype)
        off += s


def norm_matmul(x, g, w, splits, dtypes, tm=512):
    n, d = x.shape
    nout = w.shape[1]
    return pl.pallas_call(
        functools.partial(_norm_matmul_kernel, splits=splits),
        grid=(n // tm,),
        in_specs=[pl.BlockSpec((tm, d), lambda i: (i, 0)),
                  pl.BlockSpec((1, d), lambda i: (0, 0)),
                  pl.BlockSpec((d, nout), lambda i: (0, 0))],
        out_specs=[pl.BlockSpec((tm, s), lambda i: (i, 0)) for s in splits],
        out_shape=[jax.ShapeDtypeStruct((n, s), dt) for s, dt in zip(splits, dtypes)],
        compiler_params=_cparams(1, 48),
        name="norm_matmul",
    )(x, g.reshape(1, d), w)


def _matmul_residual_kernel(r_ref, *refs, n_in):
    acc = r_ref[...]
    for a_ref, w_ref in zip(refs[:n_in], refs[n_in:2 * n_in]):
        acc = acc + _dot(a_ref[...], w_ref[...])
    refs[2 * n_in][...] = acc


def matmul_residual(r, a_list, w_list, tm=512):
    n, d = r.shape
    n_in = len(a_list)
    in_specs = [pl.BlockSpec((tm, d), lambda i: (i, 0))]
    in_specs += [pl.BlockSpec((tm, a.shape[1]), lambda i: (i, 0)) for a in a_list]
    in_specs += [pl.BlockSpec(w.shape, lambda i: (0, 0)) for w in w_list]
    return pl.pallas_call(
        functools.partial(_matmul_residual_kernel, n_in=n_in),
        grid=(n // tm,),
        in_specs=in_specs,
        out_specs=pl.BlockSpec((tm, d), lambda i: (i, 0)),
        out_shape=jax.ShapeDtypeStruct((n, d), F32),
        compiler_params=_cparams(1, 32),
        name="matmul_residual",
    )(r, *a_list, *w_list)


def _s5_kernel(u_ref, a_ref, b_ref, c_ref, d_ref, gw_ref, gb_ref, o_ref, s_ref, x_ref,
               *, steps, n_slab):
    half = SSM_SLAB_GROUPS * SSM_STATE
    slab = 2 * half

    @pl.when(pl.program_id(0) == 0)
    def _():
        x_ref[...] = jnp.zeros_like(x_ref)

    u = u_ref[...]
    ub = u.astype(BF16)
    for s in range(n_slab):
        s_ref[:, s * slab:(s + 1) * slab] = _dot(ub[:, s * LANES:(s + 1) * LANES], b_ref[s])

    for s in range(n_slab):
        re = slice(s * slab, s * slab + half)
        im = slice(s * slab + half, (s + 1) * slab)
        ar = a_ref[:, re]
        ai = a_ref[:, im]

        def step(t, carry, re=re, im=im, ar=ar, ai=ai):
            xr, xi = carry
            rows = pl.ds(pl.multiple_of(t * SUBLANES, SUBLANES), SUBLANES)
            nr = ar * xr - ai * xi + s_ref[rows, re]
            ni = ar * xi + ai * xr + s_ref[rows, im]
            s_ref[rows, re] = nr
            s_ref[rows, im] = ni
            return nr, ni

        xr, xi = lax.fori_loop(0, steps, step, (x_ref[:, re], x_ref[:, im]), unroll=8)
        x_ref[:, re] = xr
        x_ref[:, im] = xi

    y = jnp.concatenate(
        [_dot(s_ref[:, s * slab:(s + 1) * slab].astype(BF16), c_ref[s]) for s in range(n_slab)],
        axis=1)
    y = y + d_ref[...] * u
    z = jax.nn.gelu(y)
    gate = jax.nn.sigmoid(_dot(z.astype(BF16), gw_ref[...]) + gb_ref[...])
    o_ref[...] = (z * gate).astype(o_ref.dtype)


def s5_mixer(u_tb, a_mat, b_mat, c_mat, d_vec, glu_w, glu_b, steps=64):
    rows, width = u_tb.shape
    n_slab = width // LANES
    n_state = 2 * n_slab * SSM_SLAB_GROUPS * SSM_STATE
    tm = steps * SUBLANES
    return pl.pallas_call(
        functools.partial(_s5_kernel, steps=steps, n_slab=n_slab),
        grid=(rows // tm,),
        in_specs=[pl.BlockSpec((tm, width), lambda i: (i, 0)),
                  pl.BlockSpec(a_mat.shape, lambda i: (0, 0)),
                  pl.BlockSpec(b_mat.shape, lambda i: (0, 0, 0)),
                  pl.BlockSpec(c_mat.shape, lambda i: (0, 0, 0)),
                  pl.BlockSpec((1, width), lambda i: (0, 0)),
                  pl.BlockSpec(glu_w.shape, lambda i: (0, 0)),
                  pl.BlockSpec((1, width), lambda i: (0, 0))],
        out_specs=pl.BlockSpec((tm, width), lambda i: (i, 0)),
        out_shape=jax.ShapeDtypeStruct((rows, width), BF16),
        scratch_shapes=[pltpu.VMEM((tm, n_state), F32), pltpu.VMEM((SUBLANES, n_state), F32)],
        compiler_params=_cparams(1, 40),
        name="s5_mixer",
    )(u_tb, a_mat, b_mat, c_mat, d_vec, glu_w, glu_b)


def _s5_params(a_re, a_im, b_re, b_im, c_re, c_im, log_step):
    n_grp = a_re.shape[0]
    n_slab = n_grp // SSM_SLAB_GROUPS
    lam = lax.complex(a_re.astype(F32), a_im.astype(F32))
    step = jnp.exp(log_step.astype(F32))[:, None]
    lam_bar = jnp.exp(lam * step)
    b_bar = ((lam_bar - 1.0) / lam)[:, :, None] * lax.complex(b_re.astype(F32), b_im.astype(F32))
    eye = jnp.eye(SSM_SLAB_GROUPS, dtype=F32)

    def slabbed(t):
        return t.reshape((n_slab, SSM_SLAB_GROUPS) + t.shape[1:])

    a_mat = jnp.concatenate(
        [slabbed(jnp.real(lam_bar)).reshape(n_slab, -1), slabbed(jnp.imag(lam_bar)).reshape(n_slab, -1)],
        axis=1).reshape(1, -1)
    a_mat = jnp.broadcast_to(a_mat, (SUBLANES, a_mat.shape[1]))
    b_parts = [jnp.einsum('sgnh,gk->sghkn', slabbed(part(b_bar)), eye)
               for part in (jnp.real, jnp.imag)]
    b_mat = jnp.stack(b_parts, axis=3).reshape(n_slab, LANES, -1)
    c_parts = [jnp.einsum('sghn,gk->sgnkh', slabbed(part), eye)
               for part in (c_re.astype(F32), -c_im.astype(F32))]
    c_mat = jnp.stack(c_parts, axis=1).reshape(n_slab, -1, LANES)
    return a_mat, b_mat.astype(BF16), c_mat.astype(BF16)


def _by_head(lane, cols):
    out = cols[-1]
    for h in reversed(range(len(cols) - 1)):
        out = jnp.where(lane // HEAD_DIM == h, cols[h], out)
    return out


def _moba_kernel(q_ref, k_ref, v_ref, tb_ref, o_ref, km_ref, *, n_blk):
    blk = MOBA_BLOCK
    heads = MOBA_HEADS
    width = heads * HEAD_DIM
    rows_all = heads * blk
    qb = pl.program_id(2)

    @pl.when(qb == 0)
    def _():
        km_ref[...] = jnp.zeros_like(km_ref)
        for n in range(n_blk):
            kb = k_ref[0, n * blk:(n + 1) * blk, :].astype(F32)
            km_ref[n:n + 1, :] = jnp.sum(kb, axis=0, keepdims=True) / blk

    q = q_ref[0]
    lane = lax.broadcasted_iota(I32, (blk, width), 1)
    q_stack = jnp.concatenate(
        [jnp.where(lane // HEAD_DIM == h, q, jnp.zeros_like(q)) for h in range(heads)], axis=0)

    blane = lax.broadcasted_iota(I32, (rows_all, LANES), 1)
    gate = lax.dot_general(q_stack.astype(F32), km_ref[...], _NT, preferred_element_type=F32,
                           precision=lax.Precision.HIGHEST)
    gate = jnp.where(blane < qb, gate, -jnp.inf)
    sel = jnp.zeros((rows_all, LANES), F32)
    for r in range(MOBA_TOPK):
        top = jnp.max(gate, axis=1, keepdims=True)
        idx = jnp.min(jnp.where(gate == top, blane, LANES), axis=1, keepdims=True)
        hit = blane == idx
        sel = jnp.where(jnp.logical_and(hit, r < qb), 1.0, sel)
        gate = jnp.where(hit, -jnp.inf, gate)

    qs = q_stack * (HEAD_DIM ** -0.5)
    row = lax.broadcasted_iota(I32, (rows_all, blk), 0)
    col = lax.broadcasted_iota(I32, (rows_all, blk), 1)

    def rows_of(n):
        return pl.ds(pl.multiple_of(n * blk, blk), blk)

    def weighted_values(p, n):
        vn = v_ref[0, rows_of(n), :]
        p = p.astype(BF16)
        p_cat = jnp.concatenate([p[h * blk:(h + 1) * blk] for h in range(heads)], axis=1)
        v_stack = jnp.concatenate(
            [jnp.where(lane // HEAD_DIM == h, vn, jnp.zeros_like(vn)) for h in range(heads)], axis=0)
        return _dot(p_cat, v_stack)

    def per_head(x):
        return _by_head(lane, [x[h * blk:(h + 1) * blk] for h in range(heads)])

    s = _dot_nt(qs, k_ref[0, rows_of(qb), :]) + tb_ref[0, 0]
    s = jnp.where(col <= (row & (blk - 1)), s, NEG)
    m = jnp.max(s, axis=1, keepdims=True)
    p = jnp.exp(s - m)
    l = jnp.sum(p, axis=1, keepdims=True)
    acc = weighted_values(p, qb)

    def body(n, carry):
        m, l, acc = carry
        chosen = jnp.sum(jnp.where(blane == n, sel, 0.0), axis=1, keepdims=True) > 0.0
        s = _dot_nt(qs, k_ref[0, rows_of(n), :]) + tb_ref[0, jnp.minimum(qb - n, 2)]
        s = jnp.where(chosen, s, NEG)
        m_new = jnp.maximum(m, jnp.max(s, axis=1, keepdims=True))
        alpha = jnp.exp(m - m_new)
        p = jnp.exp(s - m_new)
        l = alpha * l + jnp.sum(p, axis=1, keepdims=True)
        acc = per_head(alpha) * acc + weighted_values(p, n)
        return m_new, l, acc

    _, l, acc = lax.fori_loop(0, qb, body, (m, l, acc))
    o_ref[0] = (acc / per_head(l)).astype(o_ref.dtype)


def _rel_bucket(dist):
    exact = REL_BUCKETS // 2
    n = jnp.maximum(dist, 0)
    nf = jnp.maximum(n, 1).astype(F32)
    log_ratio = jnp.log(nf / exact) / math.log(REL_MAX_DIST / exact)
    large = exact + (log_ratio * (REL_BUCKETS - exact)).astype(I32)
    large = jnp.minimum(large, REL_BUCKETS - 1)
    return jnp.where(n < exact, n, large)


def _moba_bias_tables(rel_bias):
    assert REL_MAX_DIST <= MOBA_BLOCK + 1
    i = jnp.arange(MOBA_BLOCK)[:, None]
    j = jnp.arange(MOBA_BLOCK)[None, :]
    dist = jnp.arange(3)[:, None, None] * MOBA_BLOCK + (i - j)[None]
    onehot = (_rel_bucket(dist)[..., None] == jnp.arange(REL_BUCKETS)).astype(F32)
    return jnp.einsum('oijb,bh->hoij', onehot, rel_bias.astype(F32), precision=lax.Precision.HIGHEST)


def moba_mixer(qkv, rel_bias, width):
    bsz, seq, _ = qkv.shape
    blk = MOBA_BLOCK
    n_blk = seq // blk
    gw = MOBA_HEADS * HEAD_DIM
    n_grp = width // gw
    assert seq % blk == 0 and n_blk <= LANES
    tables = _moba_bias_tables(rel_bias).reshape(n_grp, MOBA_HEADS, 3, blk, blk)
    tables = tables.transpose(0, 2, 1, 3, 4).reshape(n_grp, 3, MOBA_HEADS * blk, blk)
    return pl.pallas_call(
        functools.partial(_moba_kernel, n_blk=n_blk),
        grid=(bsz, n_grp, n_blk),
        in_specs=[pl.BlockSpec((1, blk, gw), lambda b, p, i: (b, i, p)),
                  pl.BlockSpec((1, seq, gw), lambda b, p, i: (b, 0, n_grp + p)),
                  pl.BlockSpec((1, seq, gw), lambda b, p, i: (b, 0, 2 * n_grp + p)),
                  pl.BlockSpec((1, 3, MOBA_HEADS * blk, blk), lambda b, p, i: (p, 0, 0, 0))],
        out_specs=pl.BlockSpec((1, blk, gw), lambda b, p, i: (b, i, p)),
        out_shape=jax.ShapeDtypeStruct((bsz, seq, width), BF16),
        scratch_shapes=[pltpu.VMEM((LANES, gw), F32)],
        compiler_params=_cparams(3, 48),
        name="moba_mixer",
    )(qkv, qkv, qkv, tables)


def _sb_kernel(q_ref, k_ref, v_ref, tri_ref, o_ref):
    blk = SB_BLOCK
    width = SB_HEADS * HEAD_DIM
    n_sub = blk // LANES
    rows_all = SB_HEADS * blk
    qb = pl.program_id(2)
    q = q_ref[0]
    lane = lax.broadcasted_iota(I32, (blk, width), 1)
    q_stack = jnp.concatenate(
        [jnp.where(lane // HEAD_DIM == h, q, jnp.zeros_like(q)) for h in range(SB_HEADS)],
        axis=0)
    row = lax.broadcasted_iota(I32, (rows_all, blk), 0)
    col = lax.broadcasted_iota(I32, (rows_all, blk), 1)
    past = col < (row & (blk - 1))

    def rows_of(n):
        return pl.ds(pl.multiple_of(n * blk, blk), blk)

    def logits(n):
        return _dot_nt(q_stack, k_ref[0, rows_of(n), :])

    def weights(z, carried, diagonal):
        neg_abs = lax.bitcast_convert_type(
            lax.bitcast_convert_type(z, jnp.uint32) | jnp.uint32(0x80000000), F32)
        drop = jnp.maximum(z, 0.0) + jnp.log2(1.0 + jnp.exp2(neg_abs))
        if diagonal:
            drop = jnp.where(past, drop, 0.0)
        hi32 = lax.bitcast_convert_type(
            lax.bitcast_convert_type(drop, jnp.uint32) & jnp.uint32(0xFFFF0000), F32)
        hi = hi32.astype(BF16)
        lo = (drop - hi32).astype(BF16)
        lhs = jnp.concatenate(
            [jnp.concatenate([hi[:, c * LANES:(c + 1) * LANES], lo[:, c * LANES:(c + 1) * LANES]], axis=1)
             for c in range(n_sub)], axis=0)
        sums = _dot(lhs, tri_ref[...])
        newer = carried
        later = [None] * n_sub
        for c in reversed(range(n_sub)):
            within = sums[c * rows_all:(c + 1) * rows_all, :LANES]
            total = sums[c * rows_all:(c + 1) * rows_all, LANES:]
            later[c] = within if newer is None else within + newer
            newer = total if newer is None else newer + total
        w = jnp.exp2(z - drop - jnp.concatenate(later, axis=1))
        if diagonal:
            w = jnp.where(past, w, 0.0)
        w = w.astype(BF16)
        return jnp.concatenate([w[h * blk:(h + 1) * blk] for h in range(SB_HEADS)], axis=1), newer

    def weighted_values(w_cat, n):
        vn = v_ref[0, rows_of(n), :]
        v_stack = jnp.concatenate(
            [jnp.where(lane // HEAD_DIM == h, vn, jnp.zeros_like(vn)) for h in range(SB_HEADS)], axis=0)
        return _dot(w_cat, v_stack)

    w_cat, carried = weights(logits(qb), None, True)
    acc = weighted_values(w_cat, qb)

    def body(i, carry):
        acc, carried = carry
        n = qb - 1 - i
        w_cat, carried = weights(logits(n), carried, False)
        return acc + weighted_values(w_cat, n), carried

    acc, _ = lax.fori_loop(0, qb, body, (acc, carried))
    o_ref[0] = acc.astype(o_ref.dtype)


def stick_breaking_mixer(qkv, width):
    bsz, seq, _ = qkv.shape
    blk = SB_BLOCK
    gw = SB_HEADS * HEAD_DIM
    n_grp = width // gw
    tri = (jnp.arange(LANES)[:, None] > jnp.arange(LANES)[None, :]).astype(BF16)
    tri = jnp.concatenate([tri, jnp.ones((LANES, LANES), BF16)], axis=1)
    tri = jnp.concatenate([tri, tri], axis=0)
    return pl.pallas_call(
        _sb_kernel,
        grid=(bsz, n_grp, seq // blk),
        in_specs=[pl.BlockSpec((1, blk, gw), lambda b, p, i: (b, i, p)),
                  pl.BlockSpec((1, seq, gw), lambda b, p, i: (b, 0, n_grp + p)),
                  pl.BlockSpec((1, seq, gw), lambda b, p, i: (b, 0, 2 * n_grp + p)),
                  pl.BlockSpec(tri.shape, lambda b, p, i: (0, 0))],
        out_specs=pl.BlockSpec((1, blk, gw), lambda b, p, i: (b, i, p)),
        out_shape=jax.ShapeDtypeStruct((bsz, seq, width), BF16),
        compiler_params=_cparams(3, 48),
        name="stick_breaking",
    )(qkv, qkv, qkv, tri)


def _router_kernel(r_ref, g_ref, w_ref, b_ref, idx_ref, gate_ref, rank_ref, cnt_ref, run_ref):
    tm = r_ref.shape[0]

    @pl.when(pl.program_id(0) == 0)
    def _():
        run_ref[...] = jnp.zeros_like(run_ref)

    h = _rms(r_ref[...], g_ref[...])
    logits = jnp.dot(h, w_ref[...], preferred_element_type=F32,
                     precision=lax.Precision.HIGHEST) + b_ref[...]
    lane = lax.broadcasted_iota(I32, (tm, LANES), 1)
    tops, hits = [], []
    for _ in range(TOP_K):
        top = jnp.max(logits, axis=1, keepdims=True)
        idx = jnp.min(jnp.where(logits == top, lane, LANES), axis=1, keepdims=True)
        hit = lane == idx
        logits = jnp.where(hit, -jnp.inf, logits)
        tops.append(top)
        hits.append(hit)
    exps = [jnp.exp(t - tops[0]) for t in tops]
    denom = exps[0]
    for e in exps[1:]:
        denom = denom + e

    member = jnp.zeros((tm, LANES), F32)
    for hit in hits:
        member = jnp.where(hit, 1.0, member)
    before = (lax.broadcasted_iota(I32, (tm, tm), 1) < lax.broadcasted_iota(I32, (tm, tm), 0))
    ahead = _dot(before.astype(BF16), member.astype(BF16)) + run_ref[...]

    idx_out = jnp.zeros((tm, LANES), I32)
    gate_out = jnp.zeros((tm, LANES), F32)
    rank_out = jnp.zeros((tm, LANES), F32)
    for k in range(TOP_K):
        idx_k = jnp.sum(jnp.where(hits[k], lane, 0), axis=1, keepdims=True)
        rank_k = jnp.sum(jnp.where(hits[k], ahead, 0.0), axis=1, keepdims=True)
        idx_out = jnp.where(lane == k, idx_k, idx_out)
        gate_out = jnp.where(lane == k, exps[k] / denom, gate_out)
        rank_out = jnp.where(lane == k, rank_k, rank_out)
    idx_ref[...] = idx_out
    gate_ref[...] = gate_out
    rank_ref[...] = rank_out.astype(I32)
    run_ref[...] = run_ref[...] + jnp.sum(member, axis=0, keepdims=True)
    cnt_ref[...] = run_ref[...].astype(I32)


def moe_router(r, g, router_w, router_b, tm=512):
    n, d = r.shape
    n_exp = router_w.shape[1]
    w_pad = jnp.pad(router_w.astype(F32), ((0, 0), (0, LANES - n_exp)))
    b_pad = jnp.pad(router_b.astype(F32).reshape(1, n_exp), ((0, 0), (0, LANES - n_exp)),
                    constant_values=-jnp.inf)
    tile = pl.BlockSpec((tm, LANES), lambda i: (i, 0))
    idx, gates, rank, counts = pl.pallas_call(
        _router_kernel,
        grid=(n // tm,),
        in_specs=[pl.BlockSpec((tm, d), lambda i: (i, 0)),
                  pl.BlockSpec((1, d), lambda i: (0, 0)),
                  pl.BlockSpec((d, LANES), lambda i: (0, 0)),
                  pl.BlockSpec((1, LANES), lambda i: (0, 0))],
        out_specs=[tile, tile, tile, pl.BlockSpec((1, LANES), lambda i: (0, 0))],
        out_shape=[jax.ShapeDtypeStruct((n, LANES), I32), jax.ShapeDtypeStruct((n, LANES), F32),
                   jax.ShapeDtypeStruct((n, LANES), I32), jax.ShapeDtypeStruct((1, LANES), I32)],
        scratch_shapes=[pltpu.VMEM((1, LANES), F32)],
        compiler_params=_cparams(1, 32),
        name="moe_router",
    )(r, g.reshape(1, d), w_pad, b_pad)
    return idx[:, :TOP_K], gates, rank[:, :TOP_K], counts[0, :n_exp]


def _expert_kernel(be_ref, na_ref, tok_ref, tok_next_ref, r_hbm, g_ref, wg_ref, bg_ref, wu_ref, bu_ref,
                   wd_ref, bd_ref, y_ref, buf_ref, wg_bf, wu_bf, wd_bf, sem):
    b = pl.program_id(0)
    n_active = na_ref[0]
    cur = lax.rem(b, 2)

    def row_copy(tok, buf, tile_row, sub):
        return pltpu.make_async_copy(
            r_hbm.at[pl.ds(tok, 1)], buf_ref.at[buf, tile_row, pl.ds(sub, 1)], sem.at[buf])

    def start_gather(toks, buf):
        def start(t, c):
            for sub in range(SUBLANES):
                row_copy(toks[0, 0, t * SUBLANES + sub], buf, t, sub).start()
            return c
        lax.fori_loop(0, MOE_ROWS // SUBLANES, start, 0)

    @pl.when(jnp.logical_and(b == 0, n_active > 0))
    def _():
        start_gather(tok_ref, 0)

    @pl.when(b + 1 < n_active)
    def _():
        start_gather(tok_next_ref, 1 - cur)

    @pl.when(b >= n_active)
    def _():
        y_ref[...] = jnp.zeros_like(y_ref)

    @pl.when(b < n_active)
    def _():
        @pl.when(jnp.logical_or(b == 0, be_ref[b] != be_ref[jnp.maximum(b - 1, 0)]))
        def _():
            wg_bf[...] = wg_ref[0].astype(BF16)
            wu_bf[...] = wu_ref[0].astype(BF16)
            wd_bf[...] = wd_ref[0].astype(BF16)

        def wait(t, c):
            for sub in range(SUBLANES):
                row_copy(0, cur, 0, sub).wait()
            return c
        lax.fori_loop(0, MOE_ROWS // SUBLANES, wait, 0)

        h = _rms(buf_ref[cur].reshape(MOE_ROWS, -1), g_ref[...]).astype(BF16)
        gate = jnp.minimum(_dot(h, wg_bf[...]) + bg_ref[0], SWIGLU_LIMIT)
        up = jnp.clip(_dot(h, wu_bf[...]) + bu_ref[0], -SWIGLU_LIMIT, SWIGLU_LIMIT)
        act = gate * jax.nn.sigmoid(SWIGLU_ALPHA * gate) * (up + 1.0)
        y_ref[...] = _dot(act.astype(BF16), wd_bf[...]) + bd_ref[0]


def moe_experts(r, g, slot_tok, block_expert, n_active, w_gate, b_gate, w_up, b_up, w_down, b_down):
    n, d = r.shape
    n_exp, _, d_ff = w_gate.shape
    n_blocks = block_expert.shape[0]
    toks = slot_tok.reshape(n_blocks, 1, MOE_ROWS)

    def expert3(i, be, na):
        return (be[i], 0, 0)

    grid_spec = pltpu.PrefetchScalarGridSpec(
        num_scalar_prefetch=2,
        grid=(n_blocks,),
        in_specs=[pl.BlockSpec((1, 1, MOE_ROWS), lambda i, be, na: (i, 0, 0), memory_space=pltpu.SMEM),
                  pl.BlockSpec((1, 1, MOE_ROWS), lambda i, be, na: (jnp.minimum(i + 1, n_blocks - 1), 0, 0),
                               memory_space=pltpu.SMEM),
                  pl.BlockSpec(memory_space=pl.ANY),
                  pl.BlockSpec((1, d), lambda i, be, na: (0, 0)),
                  pl.BlockSpec((1, d, d_ff), expert3), pl.BlockSpec((1, 1, d_ff), expert3),
                  pl.BlockSpec((1, d, d_ff), expert3), pl.BlockSpec((1, 1, d_ff), expert3),
                  pl.BlockSpec((1, d_ff, d), expert3), pl.BlockSpec((1, 1, d), expert3)],
        out_specs=pl.BlockSpec((MOE_ROWS, d), lambda i, be, na: (i, 0)),
        scratch_shapes=[pltpu.VMEM((2, MOE_ROWS // SUBLANES, SUBLANES, d), F32),
                        pltpu.VMEM((d, d_ff), BF16), pltpu.VMEM((d, d_ff), BF16), pltpu.VMEM((d_ff, d), BF16),
                        pltpu.SemaphoreType.DMA((2,))])
    return pl.pallas_call(
        _expert_kernel,
        grid_spec=grid_spec,
        out_shape=jax.ShapeDtypeStruct((n_blocks * MOE_ROWS, d), F32),
        compiler_params=_cparams(1, 56),
        name="moe_experts",
    )(block_expert, n_active, toks, toks, r, g.reshape(1, d),
      w_gate, b_gate.reshape(n_exp, 1, d_ff), w_up, b_up.reshape(n_exp, 1, d_ff),
      w_down, b_down.reshape(n_exp, 1, d))


def _combine_kernel(dest_ref, dest_next_ref, ys_hbm, gate_ref, r_ref, p_ref, gp_ref, wp_ref, wpg_ref, gf_ref,
                    o_ref, buf_ref, sem, *, tile, final_norm):
    i = pl.program_id(0)
    cur = lax.rem(i, 2)

    def row_copy(slot, buf, k, tile_row, sub):
        return pltpu.make_async_copy(
            ys_hbm.at[pl.ds(slot, 1)], buf_ref.at[buf, k, tile_row, pl.ds(sub, 1)], sem.at[buf])

    def start_gather(dests, buf):
        def start(t, c):
            for sub in range(SUBLANES):
                for k in range(TOP_K):
                    row_copy(dests[0, 0, (t * SUBLANES + sub) * TOP_K + k], buf, k, t, sub).start()
            return c
        lax.fori_loop(0, tile // SUBLANES, start, 0)

    @pl.when(i == 0)
    def _():
        start_gather(dest_ref, 0)

    @pl.when(i + 1 < pl.num_programs(0))
    def _():
        start_gather(dest_next_ref, 1 - cur)

    def wait(t, c):
        for sub in range(SUBLANES):
            for k in range(TOP_K):
                row_copy(0, cur, k, 0, sub).wait()
        return c
    lax.fori_loop(0, tile // SUBLANES, wait, 0)

    gates = gate_ref[...]
    r = r_ref[...]
    for k in range(TOP_K):
        r = r + buf_ref[cur, k].reshape(tile, -1) * gates[:, k:k + 1]
    h = _rms(r, gp_ref[...]).astype(BF16)
    gate = jax.nn.sigmoid(_dot(h, wpg_ref[...]))
    r = r + _dot(p_ref[...].astype(BF16), wp_ref[...]) * gate
    if final_norm:
        r = _rms(r, gf_ref[...])
    o_ref[...] = r


def moe_combine_embed(ys, dest, gates, r, p, g_ple, w_ple, w_ple_gate, g_final, final_norm, tile=256):
    n, d = r.shape
    pd = p.shape[1]
    n_tiles = n // tile
    dest_tiles = dest.reshape(n_tiles, 1, tile * TOP_K)
    return pl.pallas_call(
        functools.partial(_combine_kernel, tile=tile, final_norm=final_norm),
        grid=(n_tiles,),
        in_specs=[pl.BlockSpec((1, 1, tile * TOP_K), lambda i: (i, 0, 0), memory_space=pltpu.SMEM),
                  pl.BlockSpec((1, 1, tile * TOP_K), lambda i: (jnp.minimum(i + 1, n_tiles - 1), 0, 0),
                               memory_space=pltpu.SMEM),
                  pl.BlockSpec(memory_space=pl.ANY),
                  pl.BlockSpec((tile, LANES), lambda i: (i, 0)),
                  pl.BlockSpec((tile, d), lambda i: (i, 0)),
                  pl.BlockSpec((tile, pd), lambda i: (i, 0)),
                  pl.BlockSpec((1, d), lambda i: (0, 0)),
                  pl.BlockSpec((pd, d), lambda i: (0, 0)),
                  pl.BlockSpec((d, d), lambda i: (0, 0)),
                  pl.BlockSpec((1, d), lambda i: (0, 0))],
        out_specs=pl.BlockSpec((tile, d), lambda i: (i, 0)),
        out_shape=jax.ShapeDtypeStruct((n, d), F32),
        scratch_shapes=[pltpu.VMEM((2, TOP_K, tile // SUBLANES, SUBLANES, d), F32),
                        pltpu.SemaphoreType.DMA((2,))],
        compiler_params=_cparams(1, 40),
        name="moe_combine_embed",
    )(dest_tiles, dest_tiles, ys, gates, r, p, g_ple.reshape(1, d), w_ple, w_ple_gate, g_final.reshape(1, d))


def _slot_layout(idx, rank, counts, n_blocks):
    n_exp = counts.shape[0]
    padded = (counts + MOE_ROWS - 1) // MOE_ROWS * MOE_ROWS
    pad_end = jnp.cumsum(padded)
    pad_start = pad_end - padded
    cnt_start = jnp.cumsum(counts) - counts
    start_of = jnp.sum(jnp.where(idx[..., None] == jnp.arange(n_exp), pad_start, 0), axis=-1)
    dest = (start_of + rank).astype(I32).reshape(-1)
    block_first_row = jnp.arange(n_blocks) * MOE_ROWS
    block_expert = jnp.minimum(
        jnp.sum(pad_end[None, :] <= block_first_row[:, None], axis=1), n_exp - 1).astype(I32)
    n_active = (pad_end[-1:] // MOE_ROWS).astype(I32)
    sorted_tok = (jnp.argsort(dest) // TOP_K).astype(I32)
    in_expert = block_first_row[:, None] + jnp.arange(MOE_ROWS)[None, :] - pad_start[block_expert][:, None]
    real = in_expert < counts[block_expert][:, None]
    compact = jnp.clip(in_expert + cnt_start[block_expert][:, None], 0, dest.shape[0] - 1)
    slot_tok = jnp.where(real, sorted_tok[compact], 0).astype(I32)
    return dest, slot_tok, block_expert, n_active


def moe_and_embed(r, p, g_ffn, router_w, router_b, w_gate, b_gate, w_up, b_up, w_down, b_down,
                  g_ple, w_ple, w_ple_gate, g_final, final_norm):
    n, _ = r.shape
    n_blocks = n * TOP_K // MOE_ROWS + router_w.shape[1]
    idx, gates, rank, counts = moe_router(r, g_ffn, router_w, router_b)
    dest, slot_tok, block_expert, n_active = _slot_layout(idx, rank, counts, n_blocks)
    ys = moe_experts(r, g_ffn, slot_tok, block_expert, n_active, w_gate.astype(F32), b_gate.astype(F32),
                     w_up.astype(F32), b_up.astype(F32), w_down.astype(F32), b_down.astype(F32))
    return moe_combine_embed(ys, dest, gates, r, p, g_ple, w_ple.astype(BF16), w_ple_gate.astype(BF16),
                             g_final, final_norm)


def kernel(x, p, norm_mix, norm_ffn, norm_ple, norm_final, w_in_ab, ssm_a_re, ssm_a_im, ssm_b_re, ssm_b_im, ssm_c_re, ssm_c_im, ssm_d, ssm_log_step, glu_w, glu_b, w_out_ab, rel_bias, w_in_c, w_out_c, router_w, router_b, w_gate, b_gate, w_up, b_up, w_down, b_down, w_ple, w_ple_gate):
    bsz, seq, d = x.shape
    n = bsz * seq
    depth = p.shape[0]
    assert bsz == SUBLANES
    r = x.reshape(n, d).astype(F32)
    for i in range(depth):
        j = i // 2
        if i % 2 == 0:
            ssm_w = ssm_d.shape[1] * ssm_d.shape[2]
            moba_w = (w_in_ab.shape[2] - ssm_w) // 3
            u, qkv = norm_matmul(r, norm_mix[i], w_in_ab[j].astype(BF16),
                                 (ssm_w, 3 * moba_w), (F32, BF16))
            a_mat, b_mat, c_mat = _s5_params(ssm_a_re[j], ssm_a_im[j], ssm_b_re[j], ssm_b_im[j],
                                             ssm_c_re[j], ssm_c_im[j], ssm_log_step[j])
            u_tb = u.reshape(bsz, seq, ssm_w).transpose(1, 0, 2).reshape(n, ssm_w)
            y_a = s5_mixer(u_tb, a_mat, b_mat, c_mat, ssm_d[j].reshape(1, ssm_w).astype(F32),
                           glu_w[j].astype(BF16), glu_b[j].reshape(1, ssm_w).astype(F32))
            y_a = y_a.reshape(seq, bsz, ssm_w).transpose(1, 0, 2).reshape(n, ssm_w)
            y_b = moba_mixer(qkv.reshape(bsz, seq, 3 * moba_w), rel_bias, moba_w).reshape(n, moba_w)
            w_out = w_out_ab[j].astype(BF16)
            r = matmul_residual(r, [y_a, y_b], [w_out[:ssm_w], w_out[ssm_w:]])
        else:
            q_scale = jnp.where(jnp.arange(3 * d) < d, HEAD_DIM ** -0.5 * math.log2(math.e), 1.0)
            w_in = (w_in_c[j].astype(F32) * q_scale).astype(BF16)
            (qkv,) = norm_matmul(r, norm_mix[i], w_in, (3 * d,), (BF16,))
            y_c = stick_breaking_mixer(qkv.reshape(bsz, seq, 3 * d), d).reshape(n, d)
            r = matmul_residual(r, [y_c], [w_out_c[j].astype(BF16)])
        r = moe_and_embed(r, p[i].reshape(n, -1), norm_ffn[i], router_w[i], router_b[i],
                          w_gate[i], b_gate[i], w_up[i], b_up[i], w_down[i], b_down[i],
                          norm_ple[i], w_ple[i], w_ple_gate[i], norm_final, i == depth - 1)
    return r.reshape(bsz, seq, d).astype(x.dtype)
```

```python
import functools
import math

import jax
import jax.numpy as jnp
from jax import lax
from jax.experimental import pallas as pl
from jax.experimental.pallas import tpu as pltpu

F32 = jnp.float32
BF16 = jnp.bfloat16
I32 = jnp.int32

RMS_EPS = 1e-6
HEAD_DIM = 64
LANES = 128
SUBLANES = 8
SSM_GROUP = 16
SSM_STATE = 64
SSM_SLAB_GROUPS = LANES // SSM_GROUP
MOBA_BLOCK = 256
MOBA_TOPK = 3
MOBA_HEADS = 4
REL_BUCKETS = 32
REL_MAX_DIST = 128
SB_BLOCK = 256
SB_HEADS = 4
N_EXPERTS = 32
TOP_K = 4
SWIGLU_LIMIT = 7.0
SWIGLU_ALPHA = 1.702
MOE_ROWS = 256
GATHER_BUFS = 3
NEG = -1e30
MIB = 1024 * 1024

_NT = (((1,), (1,)), ((), ()))


def _cparams(n_axes, vmem_mib):
    return pltpu.CompilerParams(
        dimension_semantics=("arbitrary",) * n_axes, vmem_limit_bytes=vmem_mib * MIB)


def _rms(x, g):
    ms = jnp.mean(x * x, axis=-1, keepdims=True)
    return x * lax.rsqrt(ms + RMS_EPS) * g


def _dot(a, b):
    return jnp.dot(a, b, preferred_element_type=F32)


def _dot_nt(a, b):
    return lax.dot_general(a, b, _NT, preferred_element_type=F32)


def _norm_matmul_kernel(x_ref, g_ref, w_ref, *o_refs, splits):
    h = _rms(x_ref[...], g_ref[...]).astype(BF16)
    y = _dot(h, w_ref[...])
    off = 0
    for o_ref, s in zip(o_refs, splits):
        o_ref[...] = y[:, off:off + s].astype(o_ref.dtype)
        off += s


def norm_matmul(x, g, w, splits, dtypes, tm=512):
    n, d = x.shape
    nout = w.shape[1]
    return pl.pallas_call(
        functools.partial(_norm_matmul_kernel, splits=splits),
        grid=(n // tm,),
        in_specs=[pl.BlockSpec((tm, d), lambda i: (i, 0)),
                  pl.BlockSpec((1, d), lambda i: (0, 0)),
                  pl.BlockSpec((d, nout), lambda i: (0, 0))],
        out_specs=[pl.BlockSpec((tm, s), lambda i: (i, 0)) for s in splits],
        out_shape=[jax.ShapeDtypeStruct((n, s), dt) for s, dt in zip(splits, dtypes)],
        compiler_params=_cparams(1, 48),
        name="norm_matmul",
    )(x, g.reshape(1, d), w)


def _store_rows_as_tiles(o3_ref, x):
    for c in range(o3_ref.shape[1]):
        o3_ref[:, c, :] = x[:, c * LANES:(c + 1) * LANES]


def _load_rows_from_tiles(x2_ref, rows, d_tiles):
    return jnp.concatenate([x2_ref[pl.ds(c, rows, stride=d_tiles), :] for c in range(d_tiles)], axis=1)


def _matmul_residual_kernel(r_ref, *refs, n_in):
    acc = r_ref[...]
    for a_ref, w_ref in zip(refs[:n_in], refs[n_in:2 * n_in]):
        acc = acc + _dot(a_ref[...], w_ref[...])
    refs[2 * n_in][...] = acc
    _store_rows_as_tiles(refs[2 * n_in + 1], acc)


def matmul_residual(r, a_list, w_list, tm=512):
    n, d = r.shape
    n_in = len(a_list)
    in_specs = [pl.BlockSpec((tm, d), lambda i: (i, 0))]
    in_specs += [pl.BlockSpec((tm, a.shape[1]), lambda i: (i, 0)) for a in a_list]
    in_specs += [pl.BlockSpec(w.shape, lambda i: (0, 0)) for w in w_list]
    return pl.pallas_call(
        functools.partial(_matmul_residual_kernel, n_in=n_in),
        grid=(n // tm,),
        in_specs=in_specs,
        out_specs=[pl.BlockSpec((tm, d), lambda i: (i, 0)),
                   pl.BlockSpec((tm, d // LANES, LANES), lambda i: (i, 0, 0))],
        out_shape=[jax.ShapeDtypeStruct((n, d), F32), jax.ShapeDtypeStruct((n, d // LANES, LANES), F32)],
        compiler_params=_cparams(1, 40),
        name="matmul_residual",
    )(r, *a_list, *w_list)


def _s5_kernel(u_ref, a_ref, b_ref, c_ref, d_ref, gw_ref, gb_ref, o_ref, s_ref, x_ref,
               *, steps, n_slab):
    half = SSM_SLAB_GROUPS * SSM_STATE
    slab = 2 * half

    @pl.when(pl.program_id(0) == 0)
    def _():
        x_ref[...] = jnp.zeros_like(x_ref)

    u = u_ref[...]
    ub = u.astype(BF16)
    for s in range(n_slab):
        s_ref[:, s * slab:(s + 1) * slab] = _dot(ub[:, s * LANES:(s + 1) * LANES], b_ref[s])

    for s in range(n_slab):
        re = slice(s * slab, s * slab + half)
        im = slice(s * slab + half, (s + 1) * slab)
        ar = a_ref[:, re]
        ai = a_ref[:, im]

        def step(t, carry, re=re, im=im, ar=ar, ai=ai):
            xr, xi = carry
            rows = pl.ds(pl.multiple_of(t * SUBLANES, SUBLANES), SUBLANES)
            nr = ar * xr - ai * xi + s_ref[rows, re]
            ni = ar * xi + ai * xr + s_ref[rows, im]
            s_ref[rows, re] = nr
            s_ref[rows, im] = ni
            return nr, ni

        xr, xi = lax.fori_loop(0, steps, step, (x_ref[:, re], x_ref[:, im]), unroll=8)
        x_ref[:, re] = xr
        x_ref[:, im] = xi

    y = jnp.concatenate(
        [_dot(s_ref[:, s * slab:(s + 1) * slab].astype(BF16), c_ref[s]) for s in range(n_slab)],
        axis=1)
    y = y + d_ref[...] * u
    z = jax.nn.gelu(y)
    gate = jax.nn.sigmoid(_dot(z.astype(BF16), gw_ref[...]) + gb_ref[...])
    o_ref[...] = (z * gate).astype(o_ref.dtype)


def s5_mixer(u_tb, a_mat, b_mat, c_mat, d_vec, glu_w, glu_b, steps=64):
    rows, width = u_tb.shape
    n_slab = width // LANES
    n_state = 2 * n_slab * SSM_SLAB_GROUPS * SSM_STATE
    tm = steps * SUBLANES
    return pl.pallas_call(
        functools.partial(_s5_kernel, steps=steps, n_slab=n_slab),
        grid=(rows // tm,),
        in_specs=[pl.BlockSpec((tm, width), lambda i: (i, 0)),
                  pl.BlockSpec(a_mat.shape, lambda i: (0, 0)),
                  pl.BlockSpec(b_mat.shape, lambda i: (0, 0, 0)),
                  pl.BlockSpec(c_mat.shape, lambda i: (0, 0, 0)),
                  pl.BlockSpec((1, width), lambda i: (0, 0)),
                  pl.BlockSpec(glu_w.shape, lambda i: (0, 0)),
                  pl.BlockSpec((1, width), lambda i: (0, 0))],
        out_specs=pl.BlockSpec((tm, width), lambda i: (i, 0)),
        out_shape=jax.ShapeDtypeStruct((rows, width), BF16),
        scratch_shapes=[pltpu.VMEM((tm, n_state), F32), pltpu.VMEM((SUBLANES, n_state), F32)],
        compiler_params=_cparams(1, 40),
        name="s5_mixer",
    )(u_tb, a_mat, b_mat, c_mat, d_vec, glu_w, glu_b)


def _s5_params(a_re, a_im, b_re, b_im, c_re, c_im, log_step):
    n_grp = a_re.shape[0]
    n_slab = n_grp // SSM_SLAB_GROUPS
    lam = lax.complex(a_re.astype(F32), a_im.astype(F32))
    step = jnp.exp(log_step.astype(F32))[:, None]
    lam_bar = jnp.exp(lam * step)
    b_bar = ((lam_bar - 1.0) / lam)[:, :, None] * lax.complex(b_re.astype(F32), b_im.astype(F32))
    eye = jnp.eye(SSM_SLAB_GROUPS, dtype=F32)

    def slabbed(t):
        return t.reshape((n_slab, SSM_SLAB_GROUPS) + t.shape[1:])

    a_mat = jnp.concatenate(
        [slabbed(jnp.real(lam_bar)).reshape(n_slab, -1), slabbed(jnp.imag(lam_bar)).reshape(n_slab, -1)],
        axis=1).reshape(1, -1)
    a_mat = jnp.broadcast_to(a_mat, (SUBLANES, a_mat.shape[1]))
    b_parts = [jnp.einsum('sgnh,gk->sghkn', slabbed(part(b_bar)), eye)
               for part in (jnp.real, jnp.imag)]
    b_mat = jnp.stack(b_parts, axis=3).reshape(n_slab, LANES, -1)
    c_parts = [jnp.einsum('sghn,gk->sgnkh', slabbed(part), eye)
               for part in (c_re.astype(F32), -c_im.astype(F32))]
    c_mat = jnp.stack(c_parts, axis=1).reshape(n_slab, -1, LANES)
    return a_mat, b_mat.astype(BF16), c_mat.astype(BF16)


def _by_head(lane, cols):
    out = cols[-1]
    for h in reversed(range(len(cols) - 1)):
        out = jnp.where(lane // HEAD_DIM == h, cols[h], out)
    return out


def _moba_kernel(q_ref, k_ref, v_ref, tb_ref, o_ref, km_ref, *, n_blk):
    blk = MOBA_BLOCK
    heads = MOBA_HEADS
    width = heads * HEAD_DIM
    rows_all = heads * blk
    qb = pl.program_id(2)

    @pl.when(qb == 0)
    def _():
        km_ref[...] = jnp.zeros_like(km_ref)
        for n in range(n_blk):
            kb = k_ref[0, n * blk:(n + 1) * blk, :].astype(F32)
            km_ref[n:n + 1, :] = jnp.sum(kb, axis=0, keepdims=True) / blk

    q = q_ref[0]
    lane = lax.broadcasted_iota(I32, (blk, width), 1)
    q_stack = jnp.concatenate(
        [jnp.where(lane // HEAD_DIM == h, q, jnp.zeros_like(q)) for h in range(heads)], axis=0)

    blane = lax.broadcasted_iota(I32, (rows_all, LANES), 1)
    gate = lax.dot_general(q_stack.astype(F32), km_ref[...], _NT, preferred_element_type=F32,
                           precision=lax.Precision.HIGHEST)
    gate = jnp.where(blane < qb, gate, -jnp.inf)
    sel = jnp.zeros((rows_all, LANES), F32)
    for r in range(MOBA_TOPK):
        top = jnp.max(gate, axis=1, keepdims=True)
        idx = jnp.min(jnp.where(gate == top, blane, LANES), axis=1, keepdims=True)
        hit = blane == idx
        sel = jnp.where(jnp.logical_and(hit, r < qb), 1.0, sel)
        gate = jnp.where(hit, -jnp.inf, gate)

    qs = q_stack * (HEAD_DIM ** -0.5)
    row = lax.broadcasted_iota(I32, (rows_all, blk), 0)
    col = lax.broadcasted_iota(I32, (rows_all, blk), 1)

    def rows_of(n):
        return pl.ds(pl.multiple_of(n * blk, blk), blk)

    def weighted_values(p, n):
        vn = v_ref[0, rows_of(n), :]
        p = p.astype(BF16)
        p_cat = jnp.concatenate([p[h * blk:(h + 1) * blk] for h in range(heads)], axis=1)
        v_stack = jnp.concatenate(
            [jnp.where(lane // HEAD_DIM == h, vn, jnp.zeros_like(vn)) for h in range(heads)], axis=0)
        return _dot(p_cat, v_stack)

    def per_head(x):
        return _by_head(lane, [x[h * blk:(h + 1) * blk] for h in range(heads)])

    s = _dot_nt(qs, k_ref[0, rows_of(qb), :]) + tb_ref[0, 0]
    s = jnp.where(col <= (row & (blk - 1)), s, NEG)
    m = jnp.max(s, axis=1, keepdims=True)
    p = jnp.exp(s - m)
    l = jnp.sum(p, axis=1, keepdims=True)
    acc = weighted_values(p, qb)

    def body(n, carry):
        m, l, acc = carry
        chosen = jnp.sum(jnp.where(blane == n, sel, 0.0), axis=1, keepdims=True) > 0.0
        s = _dot_nt(qs, k_ref[0, rows_of(n), :]) + tb_ref[0, jnp.minimum(qb - n, 2)]
        s = jnp.where(chosen, s, NEG)
        m_new = jnp.maximum(m, jnp.max(s, axis=1, keepdims=True))
        alpha = jnp.exp(m - m_new)
        p = jnp.exp(s - m_new)
        l = alpha * l + jnp.sum(p, axis=1, keepdims=True)
        acc = per_head(alpha) * acc + weighted_values(p, n)
        return m_new, l, acc

    _, l, acc = lax.fori_loop(0, qb, body, (m, l, acc))
    o_ref[0] = (acc / per_head(l)).astype(o_ref.dtype)


def _rel_bucket(dist):
    exact = REL_BUCKETS // 2
    n = jnp.maximum(dist, 0)
    nf = jnp.maximum(n, 1).astype(F32)
    log_ratio = jnp.log(nf / exact) / math.log(REL_MAX_DIST / exact)
    large = exact + (log_ratio * (REL_BUCKETS - exact)).astype(I32)
    large = jnp.minimum(large, REL_BUCKETS - 1)
    return jnp.where(n < exact, n, large)


def _moba_bias_tables(rel_bias):
    assert REL_MAX_DIST <= MOBA_BLOCK + 1
    i = jnp.arange(MOBA_BLOCK)[:, None]
    j = jnp.arange(MOBA_BLOCK)[None, :]
    dist = jnp.arange(3)[:, None, None] * MOBA_BLOCK + (i - j)[None]
    onehot = (_rel_bucket(dist)[..., None] == jnp.arange(REL_BUCKETS)).astype(F32)
    return jnp.einsum('oijb,bh->hoij', onehot, rel_bias.astype(F32), precision=lax.Precision.HIGHEST)


def moba_mixer(qkv, rel_bias, width):
    bsz, seq, _ = qkv.shape
    blk = MOBA_BLOCK
    n_blk = seq // blk
    gw = MOBA_HEADS * HEAD_DIM
    n_grp = width // gw
    assert seq % blk == 0 and n_blk <= LANES
    tables = _moba_bias_tables(rel_bias).reshape(n_grp, MOBA_HEADS, 3, blk, blk)
    tables = tables.transpose(0, 2, 1, 3, 4).reshape(n_grp, 3, MOBA_HEADS * blk, blk)
    return pl.pallas_call(
        functools.partial(_moba_kernel, n_blk=n_blk),
        grid=(bsz, n_grp, n_blk),
        in_specs=[pl.BlockSpec((1, blk, gw), lambda b, p, i: (b, i, p)),
                  pl.BlockSpec((1, seq, gw), lambda b, p, i: (b, 0, n_grp + p)),
                  pl.BlockSpec((1, seq, gw), lambda b, p, i: (b, 0, 2 * n_grp + p)),
                  pl.BlockSpec((1, 3, MOBA_HEADS * blk, blk), lambda b, p, i: (p, 0, 0, 0))],
        out_specs=pl.BlockSpec((1, blk, gw), lambda b, p, i: (b, i, p)),
        out_shape=jax.ShapeDtypeStruct((bsz, seq, width), BF16),
        scratch_shapes=[pltpu.VMEM((LANES, gw), F32)],
        compiler_params=_cparams(3, 48),
        name="moba_mixer",
    )(qkv, qkv, qkv, tables)


def _sb_kernel(q_ref, k_ref, v_ref, tri_ref, o_ref):
    blk = SB_BLOCK
    width = SB_HEADS * HEAD_DIM
    n_sub = blk // LANES
    rows_all = SB_HEADS * blk
    qb = pl.program_id(2)
    q = q_ref[0]
    lane = lax.broadcasted_iota(I32, (blk, width), 1)
    q_stack = jnp.concatenate(
        [jnp.where(lane // HEAD_DIM == h, q, jnp.zeros_like(q)) for h in range(SB_HEADS)],
        axis=0)
    row = lax.broadcasted_iota(I32, (rows_all, blk), 0)
    col = lax.broadcasted_iota(I32, (rows_all, blk), 1)
    past = col < (row & (blk - 1))

    def rows_of(n):
        return pl.ds(pl.multiple_of(n * blk, blk), blk)

    def logits(n):
        return _dot_nt(q_stack, k_ref[0, rows_of(n), :])

    def weights(z, carried, diagonal):
        neg_abs = lax.bitcast_convert_type(
            lax.bitcast_convert_type(z, jnp.uint32) | jnp.uint32(0x80000000), F32)
        drop = jnp.maximum(z, 0.0) + jnp.log2(1.0 + jnp.exp2(neg_abs))
        if diagonal:
            drop = jnp.where(past, drop, 0.0)
        hi32 = lax.bitcast_convert_type(
            lax.bitcast_convert_type(drop, jnp.uint32) & jnp.uint32(0xFFFF0000), F32)
        hi = hi32.astype(BF16)
        lo = (drop - hi32).astype(BF16)
        lhs = jnp.concatenate(
            [jnp.concatenate([hi[:, c * LANES:(c + 1) * LANES], lo[:, c * LANES:(c + 1) * LANES]], axis=1)
             for c in range(n_sub)], axis=0)
        sums = _dot(lhs, tri_ref[...])
        newer = carried
        later = [None] * n_sub
        for c in reversed(range(n_sub)):
            within = sums[c * rows_all:(c + 1) * rows_all, :LANES]
            total = sums[c * rows_all:(c + 1) * rows_all, LANES:]
            later[c] = within if newer is None else within + newer
            newer = total if newer is None else newer + total
        w = jnp.exp2(z - drop - jnp.concatenate(later, axis=1))
        if diagonal:
            w = jnp.where(past, w, 0.0)
        w = w.astype(BF16)
        return jnp.concatenate([w[h * blk:(h + 1) * blk] for h in range(SB_HEADS)], axis=1), newer

    def weighted_values(w_cat, n):
        vn = v_ref[0, rows_of(n), :]
        v_stack = jnp.concatenate(
            [jnp.where(lane // HEAD_DIM == h, vn, jnp.zeros_like(vn)) for h in range(SB_HEADS)], axis=0)
        return _dot(w_cat, v_stack)

    w_cat, carried = weights(logits(qb), None, True)
    acc = weighted_values(w_cat, qb)

    def body(i, carry):
        acc, carried = carry
        n = qb - 1 - i
        w_cat, carried = weights(logits(n), carried, False)
        return acc + weighted_values(w_cat, n), carried

    acc, _ = lax.fori_loop(0, qb, body, (acc, carried))
    o_ref[0] = acc.astype(o_ref.dtype)


def stick_breaking_mixer(qkv, width):
    bsz, seq, _ = qkv.shape
    blk = SB_BLOCK
    gw = SB_HEADS * HEAD_DIM
    n_grp = width // gw
    tri = (jnp.arange(LANES)[:, None] > jnp.arange(LANES)[None, :]).astype(BF16)
    tri = jnp.concatenate([tri, jnp.ones((LANES, LANES), BF16)], axis=1)
    tri = jnp.concatenate([tri, tri], axis=0)
    return pl.pallas_call(
        _sb_kernel,
        grid=(bsz, n_grp, seq // blk),
        in_specs=[pl.BlockSpec((1, blk, gw), lambda b, p, i: (b, i, p)),
                  pl.BlockSpec((1, seq, gw), lambda b, p, i: (b, 0, n_grp + p)),
                  pl.BlockSpec((1, seq, gw), lambda b, p, i: (b, 0, 2 * n_grp + p)),
                  pl.BlockSpec(tri.shape, lambda b, p, i: (0, 0))],
        out_specs=pl.BlockSpec((1, blk, gw), lambda b, p, i: (b, i, p)),
        out_shape=jax.ShapeDtypeStruct((bsz, seq, width), BF16),
        compiler_params=_cparams(3, 48),
        name="stick_breaking",
    )(qkv, qkv, qkv, tri)


def _router_kernel(r_ref, g_ref, w_ref, b_ref, idx_ref, gate_ref, rank_ref, cnt_ref, run_ref):
    tm = r_ref.shape[0]

    @pl.when(pl.program_id(0) == 0)
    def _():
        run_ref[...] = jnp.zeros_like(run_ref)

    h = _rms(r_ref[...], g_ref[...])
    logits = jnp.dot(h, w_ref[...], preferred_element_type=F32,
                     precision=lax.Precision.HIGHEST) + b_ref[...]
    lane = lax.broadcasted_iota(I32, (tm, LANES), 1)
    tops, hits = [], []
    for _ in range(TOP_K):
        top = jnp.max(logits, axis=1, keepdims=True)
        idx = jnp.min(jnp.where(logits == top, lane, LANES), axis=1, keepdims=True)
        hit = lane == idx
        logits = jnp.where(hit, -jnp.inf, logits)
        tops.append(top)
        hits.append(hit)
    exps = [jnp.exp(t - tops[0]) for t in tops]
    denom = exps[0]
    for e in exps[1:]:
        denom = denom + e

    member = jnp.zeros((tm, LANES), F32)
    for hit in hits:
        member = jnp.where(hit, 1.0, member)
    before = (lax.broadcasted_iota(I32, (tm, tm), 1) < lax.broadcasted_iota(I32, (tm, tm), 0))
    ahead = _dot(before.astype(BF16), member.astype(BF16)) + run_ref[...]

    idx_out = jnp.zeros((tm, LANES), I32)
    gate_out = jnp.zeros((tm, LANES), F32)
    rank_out = jnp.zeros((tm, LANES), F32)
    for k in range(TOP_K):
        idx_k = jnp.sum(jnp.where(hits[k], lane, 0), axis=1, keepdims=True)
        rank_k = jnp.sum(jnp.where(hits[k], ahead, 0.0), axis=1, keepdims=True)
        idx_out = jnp.where(lane == k, idx_k, idx_out)
        gate_out = jnp.where(lane == k, exps[k] / denom, gate_out)
        rank_out = jnp.where(lane == k, rank_k, rank_out)
    idx_ref[...] = idx_out
    gate_ref[...] = gate_out
    rank_ref[...] = rank_out.astype(I32)
    run_ref[...] = run_ref[...] + jnp.sum(member, axis=0, keepdims=True)
    cnt_ref[...] = run_ref[...].astype(I32)


def moe_router(r, g, router_w, router_b, tm=512):
    n, d = r.shape
    n_exp = router_w.shape[1]
    w_pad = jnp.pad(router_w.astype(F32), ((0, 0), (0, LANES - n_exp)))
    b_pad = jnp.pad(router_b.astype(F32).reshape(1, n_exp), ((0, 0), (0, LANES - n_exp)),
                    constant_values=-jnp.inf)
    tile = pl.BlockSpec((tm, LANES), lambda i: (i, 0))
    idx, gates, rank, counts = pl.pallas_call(
        _router_kernel,
        grid=(n // tm,),
        in_specs=[pl.BlockSpec((tm, d), lambda i: (i, 0)),
                  pl.BlockSpec((1, d), lambda i: (0, 0)),
                  pl.BlockSpec((d, LANES), lambda i: (0, 0)),
                  pl.BlockSpec((1, LANES), lambda i: (0, 0))],
        out_specs=[tile, tile, tile, pl.BlockSpec((1, LANES), lambda i: (0, 0))],
        out_shape=[jax.ShapeDtypeStruct((n, LANES), I32), jax.ShapeDtypeStruct((n, LANES), F32),
                   jax.ShapeDtypeStruct((n, LANES), I32), jax.ShapeDtypeStruct((1, LANES), I32)],
        scratch_shapes=[pltpu.VMEM((1, LANES), F32)],
        compiler_params=_cparams(1, 32),
        name="moe_router",
    )(r, g.reshape(1, d), w_pad, b_pad)
    return idx[:, :TOP_K], gates, rank[:, :TOP_K], counts[0, :n_exp]


def _expert_kernel(be_ref, tok0_ref, tok1_ref, tok_ahead_ref, r_hbm, g_ref, wg_ref, bg_ref, wu_ref, bu_ref,
                   wd_ref, bd_ref, y_ref, buf_ref, wg_bf, wu_bf, wd_bf, sem):
    b = pl.program_id(0)
    last = pl.num_programs(0) - 1
    cur = lax.rem(b, GATHER_BUFS)
    ahead = lax.rem(b + GATHER_BUFS - 1, GATHER_BUFS)
    tiles = MOE_ROWS // SUBLANES
    d_tiles = r_hbm.shape[1]

    def row_copy(tok, buf, j):
        return pltpu.make_async_copy(
            r_hbm.at[tok], buf_ref.at[buf, pl.ds(j * d_tiles, d_tiles)], sem.at[buf])

    def start_gather_loop(toks, buf):
        def start(t, c):
            for sub in range(SUBLANES):
                j = t * SUBLANES + sub
                row_copy(toks[0, 0, j], buf, j).start()
            return c
        lax.fori_loop(0, tiles, start, 0)

    def wait_gather(buf):
        def wait(t, c):
            for sub in range(SUBLANES):
                row_copy(0, buf, 0).wait()
            return c
        lax.fori_loop(0, tiles, wait, 0)

    @pl.when(b == 0)
    def _():
        start_gather_loop(tok0_ref, 0)
        start_gather_loop(tok1_ref, 1)

    @pl.when(jnp.logical_or(b == 0, be_ref[b] != be_ref[jnp.maximum(b - 1, 0)]))
    def _():
        wg_bf[...] = wg_ref[0].astype(BF16)
        wu_bf[...] = wu_ref[0].astype(BF16)
        wd_bf[...] = wd_ref[0].astype(BF16)

    wait_gather(cur)
    h = _rms(_load_rows_from_tiles(buf_ref.at[cur], MOE_ROWS, d_tiles), g_ref[...]).astype(BF16)
    gate = jnp.minimum(_dot(h, wg_bf[...]) + bg_ref[0], SWIGLU_LIMIT)
    up = jnp.clip(_dot(h, wu_bf[...]) + bu_ref[0], -SWIGLU_LIMIT, SWIGLU_LIMIT)
    act = gate * jax.nn.sigmoid(SWIGLU_ALPHA * gate) * (up + 1.0)
    _store_rows_as_tiles(y_ref, _dot(act.astype(BF16), wd_bf[...]) + bd_ref[0])
    for j in range(MOE_ROWS):
        row_copy(tok_ahead_ref[0, 0, j], ahead, j).start()

    @pl.when(b == last)
    def _():
        for back in range(1, GATHER_BUFS):
            wait_gather(lax.rem(b + back, GATHER_BUFS))


def moe_experts(r_tiles, g, slot_tok, block_expert, w_gate, b_gate, w_up, b_up, w_down, b_down):
    n, d_tiles, _ = r_tiles.shape
    d = d_tiles * LANES
    n_exp, _, d_ff = w_gate.shape
    n_blocks = block_expert.shape[0]
    assert n_blocks >= GATHER_BUFS
    toks = slot_tok.reshape(n_blocks, 1, MOE_ROWS)

    def expert3(i, be):
        return (be[i], 0, 0)

    def tok_block(index):
        return pl.BlockSpec((1, 1, MOE_ROWS), index, memory_space=pltpu.SMEM)

    grid_spec = pltpu.PrefetchScalarGridSpec(
        num_scalar_prefetch=1,
        grid=(n_blocks,),
        in_specs=[tok_block(lambda i, be: (0, 0, 0)),
                  tok_block(lambda i, be: (1, 0, 0)),
                  tok_block(lambda i, be: (jnp.minimum(i + GATHER_BUFS - 1, n_blocks - 1), 0, 0)),
                  pl.BlockSpec(memory_space=pl.ANY),
                  pl.BlockSpec((1, d), lambda i, be: (0, 0)),
                  pl.BlockSpec((1, d, d_ff), expert3), pl.BlockSpec((1, 1, d_ff), expert3),
                  pl.BlockSpec((1, d, d_ff), expert3), pl.BlockSpec((1, 1, d_ff), expert3),
                  pl.BlockSpec((1, d_ff, d), expert3), pl.BlockSpec((1, 1, d), expert3)],
        out_specs=pl.BlockSpec((MOE_ROWS, d_tiles, LANES), lambda i, be: (i, 0, 0)),
        scratch_shapes=[pltpu.VMEM((GATHER_BUFS, MOE_ROWS * d_tiles, LANES), F32),
                        pltpu.VMEM((d, d_ff), BF16), pltpu.VMEM((d, d_ff), BF16), pltpu.VMEM((d_ff, d), BF16),
                        pltpu.SemaphoreType.DMA((GATHER_BUFS,))])
    return pl.pallas_call(
        _expert_kernel,
        grid_spec=grid_spec,
        out_shape=jax.ShapeDtypeStruct((n_blocks * MOE_ROWS, d_tiles, LANES), F32),
        compiler_params=_cparams(1, 56),
        name="moe_experts",
    )(block_expert, toks, toks, toks, r_tiles, g.reshape(1, d),
      w_gate, b_gate.reshape(n_exp, 1, d_ff), w_up, b_up.reshape(n_exp, 1, d_ff),
      w_down, b_down.reshape(n_exp, 1, d))


def _combine_kernel(dest0_ref, dest1_ref, dest_ahead_ref, ys_hbm, gate_ref, r_ref, p_ref, gp_ref, wp_ref,
                    wpg_ref, gf_ref, o_ref, buf_ref, sem, *, tile, final_norm):
    i = pl.program_id(0)
    last = pl.num_programs(0) - 1
    cur = lax.rem(i, GATHER_BUFS)
    ahead = lax.rem(i + GATHER_BUFS - 1, GATHER_BUFS)
    tiles = tile // SUBLANES
    d_tiles = ys_hbm.shape[1]

    def row_copy(slot, buf, k, j):
        return pltpu.make_async_copy(
            ys_hbm.at[slot], buf_ref.at[buf, k, pl.ds(j * d_tiles, d_tiles)], sem.at[buf])

    def start_gather_loop(dests, buf):
        def start(t, c):
            for sub in range(SUBLANES):
                for k in range(TOP_K):
                    j = t * SUBLANES + sub
                    row_copy(dests[0, 0, j * TOP_K + k], buf, k, j).start()
            return c
        lax.fori_loop(0, tiles, start, 0)

    def wait_gather(buf):
        def wait(t, c):
            for sub in range(SUBLANES):
                for k in range(TOP_K):
                    row_copy(0, buf, k, 0).wait()
            return c
        lax.fori_loop(0, tiles, wait, 0)

    @pl.when(i == 0)
    def _():
        start_gather_loop(dest0_ref, 0)
        start_gather_loop(dest1_ref, 1)

    wait_gather(cur)
    gates = gate_ref[...]
    r = r_ref[...]
    for k in range(TOP_K):
        r = r + _load_rows_from_tiles(buf_ref.at[cur, k], tile, d_tiles) * gates[:, k:k + 1]
    h = _rms(r, gp_ref[...]).astype(BF16)
    gate = jax.nn.sigmoid(_dot(h, wpg_ref[...]))
    r = r + _dot(p_ref[...].astype(BF16), wp_ref[...]) * gate
    if final_norm:
        r = _rms(r, gf_ref[...])
    o_ref[...] = r
    for j in range(tile):
        for k in range(TOP_K):
            row_copy(dest_ahead_ref[0, 0, j * TOP_K + k], ahead, k, j).start()

    @pl.when(i == last)
    def _():
        for back in range(1, GATHER_BUFS):
            wait_gather(lax.rem(i + back, GATHER_BUFS))


def moe_combine_embed(ys, dest, gates, r, p, g_ple, w_ple, w_ple_gate, g_final, final_norm, tile=256):
    n, d = r.shape
    pd = p.shape[1]
    n_tiles = n // tile
    assert n_tiles >= GATHER_BUFS
    dest_tiles = dest.reshape(n_tiles, 1, tile * TOP_K)

    def dest_block(index):
        return pl.BlockSpec((1, 1, tile * TOP_K), index, memory_space=pltpu.SMEM)

    return pl.pallas_call(
        functools.partial(_combine_kernel, tile=tile, final_norm=final_norm),
        grid=(n_tiles,),
        in_specs=[dest_block(lambda i: (0, 0, 0)),
                  dest_block(lambda i: (1, 0, 0)),
                  dest_block(lambda i: (jnp.minimum(i + GATHER_BUFS - 1, n_tiles - 1), 0, 0)),
                  pl.BlockSpec(memory_space=pl.ANY),
                  pl.BlockSpec((tile, LANES), lambda i: (i, 0)),
                  pl.BlockSpec((tile, d), lambda i: (i, 0)),
                  pl.BlockSpec((tile, pd), lambda i: (i, 0)),
                  pl.BlockSpec((1, d), lambda i: (0, 0)),
                  pl.BlockSpec((pd, d), lambda i: (0, 0)),
                  pl.BlockSpec((d, d), lambda i: (0, 0)),
                  pl.BlockSpec((1, d), lambda i: (0, 0))],
        out_specs=pl.BlockSpec((tile, d), lambda i: (i, 0)),
        out_shape=jax.ShapeDtypeStruct((n, d), F32),
        scratch_shapes=[pltpu.VMEM((GATHER_BUFS, TOP_K, tile * (d // LANES), LANES), F32),
                        pltpu.SemaphoreType.DMA((GATHER_BUFS,))],
        compiler_params=_cparams(1, 48),
        name="moe_combine_embed",
    )(dest_tiles, dest_tiles, dest_tiles, ys, gates, r, p, g_ple.reshape(1, d), w_ple, w_ple_gate,
      g_final.reshape(1, d))


def _slot_layout(idx, rank, counts, n_blocks):
    n_exp = counts.shape[0]
    padded = (counts + MOE_ROWS - 1) // MOE_ROWS * MOE_ROWS
    pad_end = jnp.cumsum(padded)
    pad_start = pad_end - padded
    cnt_start = jnp.cumsum(counts) - counts
    start_of = jnp.sum(jnp.where(idx[..., None] == jnp.arange(n_exp), pad_start, 0), axis=-1)
    dest = (start_of + rank).astype(I32).reshape(-1)
    block_first_row = jnp.arange(n_blocks) * MOE_ROWS
    block_expert = jnp.minimum(
        jnp.sum(pad_end[None, :] <= block_first_row[:, None], axis=1), n_exp - 1).astype(I32)
    sorted_tok = (jnp.argsort(dest) // TOP_K).astype(I32)
    in_expert = block_first_row[:, None] + jnp.arange(MOE_ROWS)[None, :] - pad_start[block_expert][:, None]
    real = in_expert < counts[block_expert][:, None]
    compact = jnp.clip(in_expert + cnt_start[block_expert][:, None], 0, dest.shape[0] - 1)
    slot_tok = jnp.where(real, sorted_tok[compact], 0).astype(I32)
    return dest, slot_tok, block_expert


def moe_and_embed(r, r_tiles, p, g_ffn, router_w, router_b, w_gate, b_gate, w_up, b_up, w_down, b_down,
                  g_ple, w_ple, w_ple_gate, g_final, final_norm):
    n, _ = r.shape
    n_blocks = n * TOP_K // MOE_ROWS + router_w.shape[1]
    idx, gates, rank, counts = moe_router(r, g_ffn, router_w, router_b)
    dest, slot_tok, block_expert = _slot_layout(idx, rank, counts, n_blocks)
    ys = moe_experts(r_tiles, g_ffn, slot_tok, block_expert, w_gate.astype(F32), b_gate.astype(F32),
                     w_up.astype(F32), b_up.astype(F32), w_down.astype(F32), b_down.astype(F32))
    return moe_combine_embed(ys, dest, gates, r, p, g_ple, w_ple.astype(BF16), w_ple_gate.astype(BF16),
                             g_final, final_norm)


def kernel(x, p, norm_mix, norm_ffn, norm_ple, norm_final, w_in_ab, ssm_a_re, ssm_a_im, ssm_b_re, ssm_b_im, ssm_c_re, ssm_c_im, ssm_d, ssm_log_step, glu_w, glu_b, w_out_ab, rel_bias, w_in_c, w_out_c, router_w, router_b, w_gate, b_gate, w_up, b_up, w_down, b_down, w_ple, w_ple_gate):
    bsz, seq, d = x.shape
    n = bsz * seq
    depth = p.shape[0]
    assert bsz == SUBLANES
    r = x.reshape(n, d).astype(F32)
    for i in range(depth):
        j = i // 2
        if i % 2 == 0:
            ssm_w = ssm_d.shape[1] * ssm_d.shape[2]
            moba_w = (w_in_ab.shape[2] - ssm_w) // 3
            u, qkv = norm_matmul(r, norm_mix[i], w_in_ab[j].astype(BF16),
                                 (ssm_w, 3 * moba_w), (F32, BF16))
            a_mat, b_mat, c_mat = _s5_params(ssm_a_re[j], ssm_a_im[j], ssm_b_re[j], ssm_b_im[j],
                                             ssm_c_re[j], ssm_c_im[j], ssm_log_step[j])
            u_tb = u.reshape(bsz, seq, ssm_w).transpose(1, 0, 2).reshape(n, ssm_w)
            y_a = s5_mixer(u_tb, a_mat, b_mat, c_mat, ssm_d[j].reshape(1, ssm_w).astype(F32),
                           glu_w[j].astype(BF16), glu_b[j].reshape(1, ssm_w).astype(F32))
            y_a = y_a.reshape(seq, bsz, ssm_w).transpose(1, 0, 2).reshape(n, ssm_w)
            y_b = moba_mixer(qkv.reshape(bsz, seq, 3 * moba_w), rel_bias, moba_w).reshape(n, moba_w)
            w_out = w_out_ab[j].astype(BF16)
            r, r_tiles = matmul_residual(r, [y_a, y_b], [w_out[:ssm_w], w_out[ssm_w:]])
        else:
            q_scale = jnp.where(jnp.arange(3 * d) < d, HEAD_DIM ** -0.5 * math.log2(math.e), 1.0)
            w_in = (w_in_c[j].astype(F32) * q_scale).astype(BF16)
            (qkv,) = norm_matmul(r, norm_mix[i], w_in, (3 * d,), (BF16,))
            y_c = stick_breaking_mixer(qkv.reshape(bsz, seq, 3 * d), d).reshape(n, d)
            r, r_tiles = matmul_residual(r, [y_c], [w_out_c[j].astype(BF16)])
        r = moe_and_embed(r, r_tiles, p[i].reshape(n, -1), norm_ffn[i], router_w[i], router_b[i],
                          w_gate[i], b_gate[i], w_up[i], b_up[i], w_down[i], b_down[i],
                          norm_ple[i], w_ple[i], w_ple_gate[i], norm_final, i == depth - 1)
    return r.reshape(bsz, seq, d).astype(x.dtype)
```

```python
import functools
import math

import jax
import jax.numpy as jnp
from jax import lax
from jax.experimental import pallas as pl
from jax.experimental.pallas import tpu as pltpu

F32 = jnp.float32
BF16 = jnp.bfloat16
I32 = jnp.int32

RMS_EPS = 1e-6
HEAD_DIM = 64
LANES = 128
SUBLANES = 8
SSM_GROUP = 16
SSM_STATE = 64
SSM_SLAB_GROUPS = LANES // SSM_GROUP
MOBA_BLOCK = 256
MOBA_TOPK = 3
MOBA_HEADS = 4
REL_BUCKETS = 32
REL_MAX_DIST = 128
SB_BLOCK = 256
SB_HEADS = 4
N_EXPERTS = 32
TOP_K = 4
SWIGLU_LIMIT = 7.0
SWIGLU_ALPHA = 1.702
MOE_ROWS = 256
GATHER_BUFS = 3
NEG = -1e30
MIB = 1024 * 1024

_NT = (((1,), (1,)), ((), ()))


def _cparams(n_axes, vmem_mib):
    return pltpu.CompilerParams(
        dimension_semantics=("arbitrary",) * n_axes, vmem_limit_bytes=vmem_mib * MIB)


def _rms(x, g):
    ms = jnp.mean(x * x, axis=-1, keepdims=True)
    return x * lax.rsqrt(ms + RMS_EPS) * g


def _dot(a, b):
    return jnp.dot(a, b, preferred_element_type=F32)


def _dot_nt(a, b):
    return lax.dot_general(a, b, _NT, preferred_element_type=F32)


def _norm_matmul_kernel(x_ref, g_ref, w_ref, *o_refs, splits):
    h = _rms(x_ref[...], g_ref[...]).astype(BF16)
    y = _dot(h, w_ref[...])
    off = 0
    for o_ref, s in zip(o_refs, splits):
        o_ref[...] = y[:, off:off + s].astype(o_ref.dtype)
        off += s


def norm_matmul(x, g, w, splits, dtypes, tm=512):
    n, d = x.shape
    nout = w.shape[1]
    return pl.pallas_call(
        functools.partial(_norm_matmul_kernel, splits=splits),
        grid=(n // tm,),
        in_specs=[pl.BlockSpec((tm, d), lambda i: (i, 0)),
                  pl.BlockSpec((1, d), lambda i: (0, 0)),
                  pl.BlockSpec((d, nout), lambda i: (0, 0))],
        out_specs=[pl.BlockSpec((tm, s), lambda i: (i, 0)) for s in splits],
        out_shape=[jax.ShapeDtypeStruct((n, s), dt) for s, dt in zip(splits, dtypes)],
        compiler_params=_cparams(1, 48),
        name="norm_matmul",
    )(x, g.reshape(1, d), w)


def _store_rows_as_tiles(o3_ref, x):
    for c in range(o3_ref.shape[1]):
        o3_ref[:, c, :] = x[:, c * LANES:(c + 1) * LANES]


def _load_rows_from_tiles(x2_ref, rows, d_tiles):
    return jnp.concatenate([x2_ref[pl.ds(c, rows, stride=d_tiles), :] for c in range(d_tiles)], axis=1)


def _matmul_residual_kernel(r_ref, *refs, n_in):
    acc = r_ref[...]
    for a_ref, w_ref in zip(refs[:n_in], refs[n_in:2 * n_in]):
        acc = acc + _dot(a_ref[...], w_ref[...])
    refs[2 * n_in][...] = acc
    _store_rows_as_tiles(refs[2 * n_in + 1], acc)


def matmul_residual(r, a_list, w_list, tm=512):
    n, d = r.shape
    n_in = len(a_list)
    in_specs = [pl.BlockSpec((tm, d), lambda i: (i, 0))]
    in_specs += [pl.BlockSpec((tm, a.shape[1]), lambda i: (i, 0)) for a in a_list]
    in_specs += [pl.BlockSpec(w.shape, lambda i: (0, 0)) for w in w_list]
    return pl.pallas_call(
        functools.partial(_matmul_residual_kernel, n_in=n_in),
        grid=(n // tm,),
        in_specs=in_specs,
        out_specs=[pl.BlockSpec((tm, d), lambda i: (i, 0)),
                   pl.BlockSpec((tm, d // LANES, LANES), lambda i: (i, 0, 0))],
        out_shape=[jax.ShapeDtypeStruct((n, d), F32), jax.ShapeDtypeStruct((n, d // LANES, LANES), F32)],
        compiler_params=_cparams(1, 40),
        name="matmul_residual",
    )(r, *a_list, *w_list)


def _s5_kernel(u_ref, a_ref, b_ref, c_ref, d_ref, gw_ref, gb_ref, o_ref, s_ref, x_ref,
               *, steps, n_slab):
    half = SSM_SLAB_GROUPS * SSM_STATE
    slab = 2 * half

    @pl.when(pl.program_id(0) == 0)
    def _():
        x_ref[...] = jnp.zeros_like(x_ref)

    u = u_ref[...]
    ub = u.astype(BF16)
    for s in range(n_slab):
        s_ref[:, s * slab:(s + 1) * slab] = _dot(ub[:, s * LANES:(s + 1) * LANES], b_ref[s])

    for s in range(n_slab):
        re = slice(s * slab, s * slab + half)
        im = slice(s * slab + half, (s + 1) * slab)
        ar = a_ref[:, re]
        ai = a_ref[:, im]

        def step(t, carry, re=re, im=im, ar=ar, ai=ai):
            xr, xi = carry
            rows = pl.ds(pl.multiple_of(t * SUBLANES, SUBLANES), SUBLANES)
            nr = ar * xr - ai * xi + s_ref[rows, re]
            ni = ar * xi + ai * xr + s_ref[rows, im]
            s_ref[rows, re] = nr
            s_ref[rows, im] = ni
            return nr, ni

        xr, xi = lax.fori_loop(0, steps, step, (x_ref[:, re], x_ref[:, im]), unroll=8)
        x_ref[:, re] = xr
        x_ref[:, im] = xi

    y = jnp.concatenate(
        [_dot(s_ref[:, s * slab:(s + 1) * slab].astype(BF16), c_ref[s]) for s in range(n_slab)],
        axis=1)
    y = y + d_ref[...] * u
    z = jax.nn.gelu(y)
    gate = jax.nn.sigmoid(_dot(z.astype(BF16), gw_ref[...]) + gb_ref[...])
    o_ref[...] = (z * gate).astype(o_ref.dtype)


def s5_mixer(u_tb, a_mat, b_mat, c_mat, d_vec, glu_w, glu_b, steps=64):
    rows, width = u_tb.shape
    n_slab = width // LANES
    n_state = 2 * n_slab * SSM_SLAB_GROUPS * SSM_STATE
    tm = steps * SUBLANES
    return pl.pallas_call(
        functools.partial(_s5_kernel, steps=steps, n_slab=n_slab),
        grid=(rows // tm,),
        in_specs=[pl.BlockSpec((tm, width), lambda i: (i, 0)),
                  pl.BlockSpec(a_mat.shape, lambda i: (0, 0)),
                  pl.BlockSpec(b_mat.shape, lambda i: (0, 0, 0)),
                  pl.BlockSpec(c_mat.shape, lambda i: (0, 0, 0)),
                  pl.BlockSpec((1, width), lambda i: (0, 0)),
                  pl.BlockSpec(glu_w.shape, lambda i: (0, 0)),
                  pl.BlockSpec((1, width), lambda i: (0, 0))],
        out_specs=pl.BlockSpec((tm, width), lambda i: (i, 0)),
        out_shape=jax.ShapeDtypeStruct((rows, width), BF16),
        scratch_shapes=[pltpu.VMEM((tm, n_state), F32), pltpu.VMEM((SUBLANES, n_state), F32)],
        compiler_params=_cparams(1, 40),
        name="s5_mixer",
    )(u_tb, a_mat, b_mat, c_mat, d_vec, glu_w, glu_b)


def _s5_params(a_re, a_im, b_re, b_im, c_re, c_im, log_step):
    n_grp = a_re.shape[0]
    n_slab = n_grp // SSM_SLAB_GROUPS
    lam = lax.complex(a_re.astype(F32), a_im.astype(F32))
    step = jnp.exp(log_step.astype(F32))[:, None]
    lam_bar = jnp.exp(lam * step)
    b_bar = ((lam_bar - 1.0) / lam)[:, :, None] * lax.complex(b_re.astype(F32), b_im.astype(F32))
    eye = jnp.eye(SSM_SLAB_GROUPS, dtype=F32)

    def slabbed(t):
        return t.reshape((n_slab, SSM_SLAB_GROUPS) + t.shape[1:])

    a_mat = jnp.concatenate(
        [slabbed(jnp.real(lam_bar)).reshape(n_slab, -1), slabbed(jnp.imag(lam_bar)).reshape(n_slab, -1)],
        axis=1).reshape(1, -1)
    a_mat = jnp.broadcast_to(a_mat, (SUBLANES, a_mat.shape[1]))
    b_parts = [jnp.einsum('sgnh,gk->sghkn', slabbed(part(b_bar)), eye)
               for part in (jnp.real, jnp.imag)]
    b_mat = jnp.stack(b_parts, axis=3).reshape(n_slab, LANES, -1)
    c_parts = [jnp.einsum('sghn,gk->sgnkh', slabbed(part), eye)
               for part in (c_re.astype(F32), -c_im.astype(F32))]
    c_mat = jnp.stack(c_parts, axis=1).reshape(n_slab, -1, LANES)
    return a_mat, b_mat.astype(BF16), c_mat.astype(BF16)


def _by_head(lane, cols):
    out = cols[-1]
    for h in reversed(range(len(cols) - 1)):
        out = jnp.where(lane // HEAD_DIM == h, cols[h], out)
    return out


def _moba_kernel(q_ref, k_ref, v_ref, tb_ref, o_ref, mean_ref, km_ref, *, n_blk):
    blk = MOBA_BLOCK
    heads = MOBA_HEADS
    width = heads * HEAD_DIM
    rows_all = heads * blk
    qb = pl.program_id(2)

    @pl.when(qb == 0)
    def _():
        mean_ref[...] = jnp.zeros_like(mean_ref)
        for n in range(n_blk):
            kb = k_ref[0, n * blk:(n + 1) * blk, :].astype(F32)
            mean_ref[n:n + 1, :] = jnp.sum(kb, axis=0, keepdims=True) / blk
        mean = mean_ref[...]
        hi = mean.astype(BF16)
        km_ref[:, :width] = hi
        km_ref[:, width:] = (mean - hi.astype(F32)).astype(BF16)

    q = q_ref[0]
    lane = lax.broadcasted_iota(I32, (blk, width), 1)
    q_stack = jnp.concatenate(
        [jnp.where(lane // HEAD_DIM == h, q, jnp.zeros_like(q)) for h in range(heads)], axis=0)

    blane = lax.broadcasted_iota(I32, (rows_all, LANES), 1)
    gate = _dot_nt(jnp.concatenate([q_stack, q_stack], axis=1), km_ref[...])
    gate = jnp.where(blane < qb, gate, -jnp.inf)
    sel = jnp.zeros((rows_all, LANES), F32)
    blane_f = blane.astype(F32)
    for r in range(MOBA_TOPK):
        top = jnp.max(gate, axis=1, keepdims=True)
        idx = jnp.min(jnp.where(gate == top, blane_f, float(LANES)), axis=1, keepdims=True)
        hit = blane_f == idx
        sel = jnp.where(jnp.logical_and(hit, r < qb), 1.0, sel)
        gate = jnp.where(hit, -jnp.inf, gate)

    qs = q_stack * (HEAD_DIM ** -0.5)
    row = lax.broadcasted_iota(I32, (rows_all, blk), 0)
    col = lax.broadcasted_iota(I32, (rows_all, blk), 1)

    def rows_of(n):
        return pl.ds(pl.multiple_of(n * blk, blk), blk)

    def weighted_values(p, n):
        vn = v_ref[0, rows_of(n), :]
        p = p.astype(BF16)
        p_cat = jnp.concatenate([p[h * blk:(h + 1) * blk] for h in range(heads)], axis=1)
        v_stack = jnp.concatenate(
            [jnp.where(lane // HEAD_DIM == h, vn, jnp.zeros_like(vn)) for h in range(heads)], axis=0)
        return _dot(p_cat, v_stack)

    def per_head(x):
        return _by_head(lane, [x[h * blk:(h + 1) * blk] for h in range(heads)])

    s = _dot_nt(qs, k_ref[0, rows_of(qb), :]) + tb_ref[0, 0]
    s = jnp.where(col <= (row & (blk - 1)), s, NEG)
    m = jnp.max(s, axis=1, keepdims=True)
    p = jnp.exp(s - m)
    l = jnp.sum(p, axis=1, keepdims=True)
    acc = weighted_values(p, qb)

    def past_blocks(first, per_trip):
        def body(i, carry):
            m, l, acc = carry
            for u in range(per_trip):
                n = first + per_trip * i + u
                chosen = jnp.sum(jnp.where(blane == n, sel, 0.0), axis=1, keepdims=True) > 0.0
                s = _dot_nt(qs, k_ref[0, rows_of(n), :]) + tb_ref[0, jnp.minimum(qb - n, 2)]
                s = jnp.where(chosen, s, NEG)
                m_new = jnp.maximum(m, jnp.max(s, axis=1, keepdims=True))
                alpha = jnp.exp(m - m_new)
                p = jnp.exp(s - m_new)
                l = alpha * l + jnp.sum(p, axis=1, keepdims=True)
                acc = per_head(alpha) * acc + weighted_values(p, n)
                m = m_new
            return m, l, acc
        return body

    odd = qb & 1
    carry = lax.fori_loop(0, odd, past_blocks(0, 1), (m, l, acc))
    _, l, acc = lax.fori_loop(0, qb // 2, past_blocks(odd, 2), carry)
    o_ref[0] = (acc / per_head(l)).astype(o_ref.dtype)


def _rel_bucket(dist):
    exact = REL_BUCKETS // 2
    n = jnp.maximum(dist, 0)
    nf = jnp.maximum(n, 1).astype(F32)
    log_ratio = jnp.log(nf / exact) / math.log(REL_MAX_DIST / exact)
    large = exact + (log_ratio * (REL_BUCKETS - exact)).astype(I32)
    large = jnp.minimum(large, REL_BUCKETS - 1)
    return jnp.where(n < exact, n, large)


def _moba_bias_tables(rel_bias):
    assert REL_MAX_DIST <= MOBA_BLOCK + 1
    i = jnp.arange(MOBA_BLOCK)[:, None]
    j = jnp.arange(MOBA_BLOCK)[None, :]
    dist = jnp.arange(3)[:, None, None] * MOBA_BLOCK + (i - j)[None]
    onehot = (_rel_bucket(dist)[..., None] == jnp.arange(REL_BUCKETS)).astype(F32)
    return jnp.einsum('oijb,bh->hoij', onehot, rel_bias.astype(F32), precision=lax.Precision.HIGHEST)


def moba_mixer(qkv, rel_bias, width):
    bsz, seq, _ = qkv.shape
    blk = MOBA_BLOCK
    n_blk = seq // blk
    gw = MOBA_HEADS * HEAD_DIM
    n_grp = width // gw
    assert seq % blk == 0 and n_blk <= LANES
    tables = _moba_bias_tables(rel_bias).reshape(n_grp, MOBA_HEADS, 3, blk, blk)
    tables = tables.transpose(0, 2, 1, 3, 4).reshape(n_grp, 3, MOBA_HEADS * blk, blk)
    return pl.pallas_call(
        functools.partial(_moba_kernel, n_blk=n_blk),
        grid=(bsz, n_grp, n_blk),
        in_specs=[pl.BlockSpec((1, blk, gw), lambda b, p, i: (b, i, p)),
                  pl.BlockSpec((1, seq, gw), lambda b, p, i: (b, 0, n_grp + p)),
                  pl.BlockSpec((1, seq, gw), lambda b, p, i: (b, 0, 2 * n_grp + p)),
                  pl.BlockSpec((1, 3, MOBA_HEADS * blk, blk), lambda b, p, i: (p, 0, 0, 0))],
        out_specs=pl.BlockSpec((1, blk, gw), lambda b, p, i: (b, i, p)),
        out_shape=jax.ShapeDtypeStruct((bsz, seq, width), BF16),
        scratch_shapes=[pltpu.VMEM((LANES, gw), F32), pltpu.VMEM((LANES, 2 * gw), BF16)],
        compiler_params=_cparams(3, 48),
        name="moba_mixer",
    )(qkv, qkv, qkv, tables)


def _sb_kernel(q_ref, k_ref, v_ref, tri_ref, o_ref):
    blk = SB_BLOCK
    width = SB_HEADS * HEAD_DIM
    n_sub = blk // LANES
    rows_all = SB_HEADS * blk
    qb = pl.program_id(2)
    q = q_ref[0]
    lane = lax.broadcasted_iota(I32, (blk, width), 1)
    q_stack = jnp.concatenate(
        [jnp.where(lane // HEAD_DIM == h, q, jnp.zeros_like(q)) for h in range(SB_HEADS)],
        axis=0)
    row = lax.broadcasted_iota(I32, (rows_all, blk), 0)
    col = lax.broadcasted_iota(I32, (rows_all, blk), 1)
    past = col < (row & (blk - 1))

    def rows_of(n):
        return pl.ds(pl.multiple_of(n * blk, blk), blk)

    def logits(n):
        return _dot_nt(q_stack, k_ref[0, rows_of(n), :])

    def weights(z, carried, diagonal):
        neg_abs = lax.bitcast_convert_type(
            lax.bitcast_convert_type(z, jnp.uint32) | jnp.uint32(0x80000000), F32)
        drop = jnp.maximum(z, 0.0) + jnp.log2(1.0 + jnp.exp2(neg_abs))
        if diagonal:
            drop = jnp.where(past, drop, 0.0)
        hi32 = lax.bitcast_convert_type(
            lax.bitcast_convert_type(drop, jnp.uint32) & jnp.uint32(0xFFFF0000), F32)
        hi = hi32.astype(BF16)
        lo = (drop - hi32).astype(BF16)
        lhs = jnp.concatenate(
            [jnp.concatenate([hi[:, c * LANES:(c + 1) * LANES], lo[:, c * LANES:(c + 1) * LANES]], axis=1)
             for c in range(n_sub)], axis=0)
        sums = _dot(lhs, tri_ref[...])
        newer = carried
        later = [None] * n_sub
        for c in reversed(range(n_sub)):
            within = sums[c * rows_all:(c + 1) * rows_all, :LANES]
            total = sums[c * rows_all:(c + 1) * rows_all, LANES:]
            later[c] = within if newer is None else within + newer
            newer = total if newer is None else newer + total
        w = jnp.exp2(z - drop - jnp.concatenate(later, axis=1))
        if diagonal:
            w = jnp.where(past, w, 0.0)
        w = w.astype(BF16)
        return jnp.concatenate([w[h * blk:(h + 1) * blk] for h in range(SB_HEADS)], axis=1), newer

    def weighted_values(w_cat, n):
        vn = v_ref[0, rows_of(n), :]
        v_stack = jnp.concatenate(
            [jnp.where(lane // HEAD_DIM == h, vn, jnp.zeros_like(vn)) for h in range(SB_HEADS)], axis=0)
        return _dot(w_cat, v_stack)

    w_cat, carried = weights(logits(qb), None, True)
    acc = weighted_values(w_cat, qb)

    def older_blocks(first, per_trip):
        def body(i, carry):
            acc, carried = carry
            for u in range(per_trip):
                n = first - per_trip * i - u
                w_cat, carried = weights(logits(n), carried, False)
                acc = acc + weighted_values(w_cat, n)
            return acc, carried
        return body

    odd = qb & 1
    acc, carried = lax.fori_loop(0, odd, older_blocks(qb - 1, 1), (acc, carried))
    acc, _ = lax.fori_loop(0, qb // 2, older_blocks(qb - 1 - odd, 2), (acc, carried))
    o_ref[0] = acc.astype(o_ref.dtype)


def stick_breaking_mixer(qkv, width):
    bsz, seq, _ = qkv.shape
    blk = SB_BLOCK
    gw = SB_HEADS * HEAD_DIM
    n_grp = width // gw
    tri = (jnp.arange(LANES)[:, None] > jnp.arange(LANES)[None, :]).astype(BF16)
    tri = jnp.concatenate([tri, jnp.ones((LANES, LANES), BF16)], axis=1)
    tri = jnp.concatenate([tri, tri], axis=0)
    return pl.pallas_call(
        _sb_kernel,
        grid=(bsz, n_grp, seq // blk),
        in_specs=[pl.BlockSpec((1, blk, gw), lambda b, p, i: (b, i, p)),
                  pl.BlockSpec((1, seq, gw), lambda b, p, i: (b, 0, n_grp + p)),
                  pl.BlockSpec((1, seq, gw), lambda b, p, i: (b, 0, 2 * n_grp + p)),
                  pl.BlockSpec(tri.shape, lambda b, p, i: (0, 0))],
        out_specs=pl.BlockSpec((1, blk, gw), lambda b, p, i: (b, i, p)),
        out_shape=jax.ShapeDtypeStruct((bsz, seq, width), BF16),
        compiler_params=_cparams(3, 48),
        name="stick_breaking",
    )(qkv, qkv, qkv, tri)


def _router_kernel(r_ref, g_ref, w_ref, b_ref, idx_ref, gate_ref, rank_ref, cnt_ref, run_ref):
    tm = r_ref.shape[0]

    @pl.when(pl.program_id(0) == 0)
    def _():
        run_ref[...] = jnp.zeros_like(run_ref)

    h = _rms(r_ref[...], g_ref[...])
    h_hi = h.astype(BF16)
    h_lo = (h - h_hi.astype(F32)).astype(BF16)
    logits = _dot(jnp.concatenate([h_hi, h_hi, h_lo], axis=1), w_ref[...]) + b_ref[...]
    lane = lax.broadcasted_iota(I32, (tm, LANES), 1)
    lane_f = lane.astype(F32)
    tops, hits = [], []
    for _ in range(TOP_K):
        top = jnp.max(logits, axis=1, keepdims=True)
        idx = jnp.min(jnp.where(logits == top, lane_f, float(LANES)), axis=1, keepdims=True)
        hit = lane_f == idx
        logits = jnp.where(hit, -jnp.inf, logits)
        tops.append(top)
        hits.append(hit)
    exps = [jnp.exp(t - tops[0]) for t in tops]
    denom = exps[0]
    for e in exps[1:]:
        denom = denom + e

    member = jnp.zeros((tm, LANES), F32)
    for hit in hits:
        member = jnp.where(hit, 1.0, member)
    before = (lax.broadcasted_iota(I32, (tm, tm), 1) < lax.broadcasted_iota(I32, (tm, tm), 0))
    ahead = _dot(before.astype(BF16), member.astype(BF16)) + run_ref[...]

    idx_out = jnp.zeros((tm, LANES), F32)
    gate_out = jnp.zeros((tm, LANES), F32)
    rank_out = jnp.zeros((tm, LANES), F32)
    for k in range(TOP_K):
        idx_k = jnp.sum(jnp.where(hits[k], lane_f, 0.0), axis=1, keepdims=True)
        rank_k = jnp.sum(jnp.where(hits[k], ahead, 0.0), axis=1, keepdims=True)
        idx_out = jnp.where(lane == k, idx_k, idx_out)
        gate_out = jnp.where(lane == k, exps[k] / denom, gate_out)
        rank_out = jnp.where(lane == k, rank_k, rank_out)
    idx_ref[...] = idx_out.astype(I32)
    gate_ref[...] = gate_out
    rank_ref[...] = rank_out.astype(I32)
    run_ref[...] = run_ref[...] + jnp.sum(member, axis=0, keepdims=True)
    cnt_ref[...] = run_ref[...].astype(I32)


def moe_router(r, g, router_w, router_b, tm=512):
    n, d = r.shape
    n_exp = router_w.shape[1]
    w_pad = jnp.pad(router_w.astype(F32), ((0, 0), (0, LANES - n_exp)))
    w_hi = w_pad.astype(BF16)
    w_lo = (w_pad - w_hi.astype(F32)).astype(BF16)
    w_split = jnp.concatenate([w_hi, w_lo, w_hi], axis=0)
    b_pad = jnp.pad(router_b.astype(F32).reshape(1, n_exp), ((0, 0), (0, LANES - n_exp)),
                    constant_values=-jnp.inf)
    tile = pl.BlockSpec((tm, LANES), lambda i: (i, 0))
    idx, gates, rank, counts = pl.pallas_call(
        _router_kernel,
        grid=(n // tm,),
        in_specs=[pl.BlockSpec((tm, d), lambda i: (i, 0)),
                  pl.BlockSpec((1, d), lambda i: (0, 0)),
                  pl.BlockSpec((3 * d, LANES), lambda i: (0, 0)),
                  pl.BlockSpec((1, LANES), lambda i: (0, 0))],
        out_specs=[tile, tile, tile, pl.BlockSpec((1, LANES), lambda i: (0, 0))],
        out_shape=[jax.ShapeDtypeStruct((n, LANES), I32), jax.ShapeDtypeStruct((n, LANES), F32),
                   jax.ShapeDtypeStruct((n, LANES), I32), jax.ShapeDtypeStruct((1, LANES), I32)],
        scratch_shapes=[pltpu.VMEM((1, LANES), F32)],
        compiler_params=_cparams(1, 32),
        name="moe_router",
    )(r, g.reshape(1, d), w_split, b_pad)
    return idx[:, :TOP_K], gates, rank[:, :TOP_K], counts[0, :n_exp]


def _expert_kernel(be_ref, tok0_ref, tok1_ref, tok_ahead_ref, r_hbm, g_ref, wg_ref, bg_ref, wu_ref, bu_ref,
                   wd_ref, bd_ref, y_ref, buf_ref, wg_bf, wu_bf, wd_bf, sem):
    b = pl.program_id(0)
    last = pl.num_programs(0) - 1
    cur = lax.rem(b, GATHER_BUFS)
    ahead = lax.rem(b + GATHER_BUFS - 1, GATHER_BUFS)
    tiles = MOE_ROWS // SUBLANES
    d_tiles = r_hbm.shape[1]

    def row_copy(tok, buf, j):
        return pltpu.make_async_copy(
            r_hbm.at[tok], buf_ref.at[buf, pl.ds(j * d_tiles, d_tiles)], sem.at[buf])

    def start_gather_loop(toks, buf):
        def start(t, c):
            for sub in range(SUBLANES):
                j = t * SUBLANES + sub
                row_copy(toks[0, 0, j], buf, j).start()
            return c
        lax.fori_loop(0, tiles, start, 0)

    def wait_gather(buf):
        def wait(t, c):
            for sub in range(SUBLANES):
                row_copy(0, buf, 0).wait()
            return c
        lax.fori_loop(0, tiles, wait, 0)

    @pl.when(b == 0)
    def _():
        start_gather_loop(tok0_ref, 0)
        start_gather_loop(tok1_ref, 1)

    @pl.when(jnp.logical_or(b == 0, be_ref[b] != be_ref[jnp.maximum(b - 1, 0)]))
    def _():
        wg_bf[...] = wg_ref[0].astype(BF16)
        wu_bf[...] = wu_ref[0].astype(BF16)
        wd_bf[...] = wd_ref[0].astype(BF16)

    wait_gather(cur)
    h = _rms(_load_rows_from_tiles(buf_ref.at[cur], MOE_ROWS, d_tiles), g_ref[...]).astype(BF16)
    gate = jnp.minimum(_dot(h, wg_bf[...]) + bg_ref[0], SWIGLU_LIMIT)
    up = jnp.clip(_dot(h, wu_bf[...]) + bu_ref[0], -SWIGLU_LIMIT, SWIGLU_LIMIT)
    act = gate * jax.nn.sigmoid(SWIGLU_ALPHA * gate) * (up + 1.0)
    _store_rows_as_tiles(y_ref, _dot(act.astype(BF16), wd_bf[...]) + bd_ref[0])
    for j in range(MOE_ROWS):
        row_copy(tok_ahead_ref[0, 0, j], ahead, j).start()

    @pl.when(b == last)
    def _():
        for back in range(1, GATHER_BUFS):
            wait_gather(lax.rem(b + back, GATHER_BUFS))


def moe_experts(r_tiles, g, slot_tok, block_expert, w_gate, b_gate, w_up, b_up, w_down, b_down):
    n, d_tiles, _ = r_tiles.shape
    d = d_tiles * LANES
    n_exp, _, d_ff = w_gate.shape
    n_blocks = block_expert.shape[0]
    assert n_blocks >= GATHER_BUFS
    toks = slot_tok.reshape(n_blocks, 1, MOE_ROWS)

    def expert3(i, be):
        return (be[i], 0, 0)

    def tok_block(index):
        return pl.BlockSpec((1, 1, MOE_ROWS), index, memory_space=pltpu.SMEM)

    grid_spec = pltpu.PrefetchScalarGridSpec(
        num_scalar_prefetch=1,
        grid=(n_blocks,),
        in_specs=[tok_block(lambda i, be: (0, 0, 0)),
                  tok_block(lambda i, be: (1, 0, 0)),
                  tok_block(lambda i, be: (jnp.minimum(i + GATHER_BUFS - 1, n_blocks - 1), 0, 0)),
                  pl.BlockSpec(memory_space=pl.ANY),
                  pl.BlockSpec((1, d), lambda i, be: (0, 0)),
                  pl.BlockSpec((1, d, d_ff), expert3), pl.BlockSpec((1, 1, d_ff), expert3),
                  pl.BlockSpec((1, d, d_ff), expert3), pl.BlockSpec((1, 1, d_ff), expert3),
                  pl.BlockSpec((1, d_ff, d), expert3), pl.BlockSpec((1, 1, d), expert3)],
        out_specs=pl.BlockSpec((MOE_ROWS, d_tiles, LANES), lambda i, be: (i, 0, 0)),
        scratch_shapes=[pltpu.VMEM((GATHER_BUFS, MOE_ROWS * d_tiles, LANES), F32),
                        pltpu.VMEM((d, d_ff), BF16), pltpu.VMEM((d, d_ff), BF16), pltpu.VMEM((d_ff, d), BF16),
                        pltpu.SemaphoreType.DMA((GATHER_BUFS,))])
    return pl.pallas_call(
        _expert_kernel,
        grid_spec=grid_spec,
        out_shape=jax.ShapeDtypeStruct((n_blocks * MOE_ROWS, d_tiles, LANES), F32),
        compiler_params=_cparams(1, 56),
        name="moe_experts",
    )(block_expert, toks, toks, toks, r_tiles, g.reshape(1, d),
      w_gate, b_gate.reshape(n_exp, 1, d_ff), w_up, b_up.reshape(n_exp, 1, d_ff),
      w_down, b_down.reshape(n_exp, 1, d))


def _combine_kernel(dest0_ref, dest1_ref, dest_ahead_ref, ys_hbm, gate_ref, r_ref, p_ref, gp_ref, wp_ref,
                    wpg_ref, gf_ref, o_ref, buf_ref, sem, *, tile, final_norm):
    i = pl.program_id(0)
    last = pl.num_programs(0) - 1
    cur = lax.rem(i, GATHER_BUFS)
    ahead = lax.rem(i + GATHER_BUFS - 1, GATHER_BUFS)
    tiles = tile // SUBLANES
    d_tiles = ys_hbm.shape[1]

    def row_copy(slot, buf, k, j):
        return pltpu.make_async_copy(
            ys_hbm.at[slot], buf_ref.at[buf, k, pl.ds(j * d_tiles, d_tiles)], sem.at[buf])

    def start_gather_loop(dests, buf):
        def start(t, c):
            for sub in range(SUBLANES):
                for k in range(TOP_K):
                    j = t * SUBLANES + sub
                    row_copy(dests[0, 0, j * TOP_K + k], buf, k, j).start()
            return c
        lax.fori_loop(0, tiles, start, 0)

    def wait_gather(buf):
        def wait(t, c):
            for sub in range(SUBLANES):
                for k in range(TOP_K):
                    row_copy(0, buf, k, 0).wait()
            return c
        lax.fori_loop(0, tiles, wait, 0)

    @pl.when(i == 0)
    def _():
        start_gather_loop(dest0_ref, 0)
        start_gather_loop(dest1_ref, 1)

    wait_gather(cur)
    gates = gate_ref[...]
    r = r_ref[...]
    for k in range(TOP_K):
        r = r + _load_rows_from_tiles(buf_ref.at[cur, k], tile, d_tiles) * gates[:, k:k + 1]
    h = _rms(r, gp_ref[...]).astype(BF16)
    gate = jax.nn.sigmoid(_dot(h, wpg_ref[...]))
    r = r + _dot(p_ref[...].astype(BF16), wp_ref[...]) * gate
    if final_norm:
        r = _rms(r, gf_ref[...])
    o_ref[...] = r
    for j in range(tile):
        for k in range(TOP_K):
            row_copy(dest_ahead_ref[0, 0, j * TOP_K + k], ahead, k, j).start()

    @pl.when(i == last)
    def _():
        for back in range(1, GATHER_BUFS):
            wait_gather(lax.rem(i + back, GATHER_BUFS))


def moe_combine_embed(ys, dest, gates, r, p, g_ple, w_ple, w_ple_gate, g_final, final_norm, tile=256):
    n, d = r.shape
    pd = p.shape[1]
    n_tiles = n // tile
    assert n_tiles >= GATHER_BUFS
    dest_tiles = dest.reshape(n_tiles, 1, tile * TOP_K)

    def dest_block(index):
        return pl.BlockSpec((1, 1, tile * TOP_K), index, memory_space=pltpu.SMEM)

    return pl.pallas_call(
        functools.partial(_combine_kernel, tile=tile, final_norm=final_norm),
        grid=(n_tiles,),
        in_specs=[dest_block(lambda i: (0, 0, 0)),
                  dest_block(lambda i: (1, 0, 0)),
                  dest_block(lambda i: (jnp.minimum(i + GATHER_BUFS - 1, n_tiles - 1), 0, 0)),
                  pl.BlockSpec(memory_space=pl.ANY),
                  pl.BlockSpec((tile, LANES), lambda i: (i, 0)),
                  pl.BlockSpec((tile, d), lambda i: (i, 0)),
                  pl.BlockSpec((tile, pd), lambda i: (i, 0)),
                  pl.BlockSpec((1, d), lambda i: (0, 0)),
                  pl.BlockSpec((pd, d), lambda i: (0, 0)),
                  pl.BlockSpec((d, d), lambda i: (0, 0)),
                  pl.BlockSpec((1, d), lambda i: (0, 0))],
        out_specs=pl.BlockSpec((tile, d), lambda i: (i, 0)),
        out_shape=jax.ShapeDtypeStruct((n, d), F32),
        scratch_shapes=[pltpu.VMEM((GATHER_BUFS, TOP_K, tile * (d // LANES), LANES), F32),
                        pltpu.SemaphoreType.DMA((GATHER_BUFS,))],
        compiler_params=_cparams(1, 48),
        name="moe_combine_embed",
    )(dest_tiles, dest_tiles, dest_tiles, ys, gates, r, p, g_ple.reshape(1, d), w_ple, w_ple_gate,
      g_final.reshape(1, d))


def _slot_layout(idx, rank, counts, n_blocks):
    n_exp = counts.shape[0]
    padded = (counts + MOE_ROWS - 1) // MOE_ROWS * MOE_ROWS
    pad_end = jnp.cumsum(padded)
    pad_start = pad_end - padded
    cnt_start = jnp.cumsum(counts) - counts
    start_of = jnp.sum(jnp.where(idx[..., None] == jnp.arange(n_exp), pad_start, 0), axis=-1)
    dest = (start_of + rank).astype(I32).reshape(-1)
    block_first_row = jnp.arange(n_blocks) * MOE_ROWS
    block_expert = jnp.minimum(
        jnp.sum(pad_end[None, :] <= block_first_row[:, None], axis=1), n_exp - 1).astype(I32)
    sorted_tok = (jnp.argsort(dest) // TOP_K).astype(I32)
    in_expert = block_first_row[:, None] + jnp.arange(MOE_ROWS)[None, :] - pad_start[block_expert][:, None]
    real = in_expert < counts[block_expert][:, None]
    compact = jnp.clip(in_expert + cnt_start[block_expert][:, None], 0, dest.shape[0] - 1)
    slot_tok = jnp.where(real, sorted_tok[compact], 0).astype(I32)
    return dest, slot_tok, block_expert


def moe_and_embed(r, r_tiles, p, g_ffn, router_w, router_b, w_gate, b_gate, w_up, b_up, w_down, b_down,
                  g_ple, w_ple, w_ple_gate, g_final, final_norm):
    n, _ = r.shape
    n_blocks = n * TOP_K // MOE_ROWS + router_w.shape[1]
    idx, gates, rank, counts = moe_router(r, g_ffn, router_w, router_b)
    dest, slot_tok, block_expert = _slot_layout(idx, rank, counts, n_blocks)
    ys = moe_experts(r_tiles, g_ffn, slot_tok, block_expert, w_gate.astype(F32), b_gate.astype(F32),
                     w_up.astype(F32), b_up.astype(F32), w_down.astype(F32), b_down.astype(F32))
    return moe_combine_embed(ys, dest, gates, r, p, g_ple, w_ple.astype(BF16), w_ple_gate.astype(BF16),
                             g_final, final_norm)


def kernel(x, p, norm_mix, norm_ffn, norm_ple, norm_final, w_in_ab, ssm_a_re, ssm_a_im, ssm_b_re, ssm_b_im, ssm_c_re, ssm_c_im, ssm_d, ssm_log_step, glu_w, glu_b, w_out_ab, rel_bias, w_in_c, w_out_c, router_w, router_b, w_gate, b_gate, w_up, b_up, w_down, b_down, w_ple, w_ple_gate):
    bsz, seq, d = x.shape
    n = bsz * seq
    depth = p.shape[0]
    assert bsz == SUBLANES
    r = x.reshape(n, d).astype(F32)
    for i in range(depth):
        j = i // 2
        if i % 2 == 0:
            ssm_w = ssm_d.shape[1] * ssm_d.shape[2]
            moba_w = (w_in_ab.shape[2] - ssm_w) // 3
            u, qkv = norm_matmul(r, norm_mix[i], w_in_ab[j].astype(BF16),
                                 (ssm_w, 3 * moba_w), (F32, BF16))
            a_mat, b_mat, c_mat = _s5_params(ssm_a_re[j], ssm_a_im[j], ssm_b_re[j], ssm_b_im[j],
                                             ssm_c_re[j], ssm_c_im[j], ssm_log_step[j])
            u_tb = u.reshape(bsz, seq, ssm_w).transpose(1, 0, 2).reshape(n, ssm_w)
            y_a = s5_mixer(u_tb, a_mat, b_mat, c_mat, ssm_d[j].reshape(1, ssm_w).astype(F32),
                           glu_w[j].astype(BF16), glu_b[j].reshape(1, ssm_w).astype(F32))
            y_a = y_a.reshape(seq, bsz, ssm_w).transpose(1, 0, 2).reshape(n, ssm_w)
            y_b = moba_mixer(qkv.reshape(bsz, seq, 3 * moba_w), rel_bias, moba_w).reshape(n, moba_w)
            w_out = w_out_ab[j].astype(BF16)
            r, r_tiles = matmul_residual(r, [y_a, y_b], [w_out[:ssm_w], w_out[ssm_w:]])
        else:
            q_scale = jnp.where(jnp.arange(3 * d) < d, HEAD_DIM ** -0.5 * math.log2(math.e), 1.0)
            w_in = (w_in_c[j].astype(F32) * q_scale).astype(BF16)
            (qkv,) = norm_matmul(r, norm_mix[i], w_in, (3 * d,), (BF16,))
            y_c = stick_breaking_mixer(qkv.reshape(bsz, seq, 3 * d), d).reshape(n, d)
            r, r_tiles = matmul_residual(r, [y_c], [w_out_c[j].astype(BF16)])
        r = moe_and_embed(r, r_tiles, p[i].reshape(n, -1), norm_ffn[i], router_w[i], router_b[i],
                          w_gate[i], b_gate[i], w_up[i], b_up[i], w_down[i], b_down[i],
                          norm_ple[i], w_ple[i], w_ple_gate[i], norm_final, i == depth - 1)
    return r.reshape(bsz, seq, d).astype(x.dtype)
```

```python
import functools
import math

import jax
import jax.numpy as jnp
from jax import lax
from jax.experimental import pallas as pl
from jax.experimental.pallas import tpu as pltpu

F32 = jnp.float32
BF16 = jnp.bfloat16
I32 = jnp.int32

RMS_EPS = 1e-6
HEAD_DIM = 64
LANES = 128
SUBLANES = 8
SSM_GROUP = 16
SSM_STATE = 64
SSM_SLAB_GROUPS = LANES // SSM_GROUP
MOBA_BLOCK = 256
MOBA_TOPK = 3
MOBA_HEADS = 4
REL_BUCKETS = 32
REL_MAX_DIST = 128
SB_BLOCK = 256
SB_HEADS = 4
N_EXPERTS = 32
TOP_K = 4
SWIGLU_LIMIT = 7.0
SWIGLU_ALPHA = 1.702
MOE_ROWS = 256
GATHER_BUFS = 3
DMA_QUEUES = 2
NEG = -1e30
MIB = 1024 * 1024

_NT = (((1,), (1,)), ((), ()))


def _cparams(n_axes, vmem_mib):
    return pltpu.CompilerParams(
        dimension_semantics=("arbitrary",) * n_axes, vmem_limit_bytes=vmem_mib * MIB)


def _rms(x, g):
    ms = jnp.mean(x * x, axis=-1, keepdims=True)
    return x * lax.rsqrt(ms + RMS_EPS) * g


def _dot(a, b):
    return jnp.dot(a, b, preferred_element_type=F32)


def _dot_nt(a, b):
    return lax.dot_general(a, b, _NT, preferred_element_type=F32)


def _norm_matmul_kernel(x_ref, g_ref, w_ref, *o_refs, splits):
    h = _rms(x_ref[...], g_ref[...]).astype(BF16)
    y = _dot(h, w_ref[...])
    off = 0
    for o_ref, s in zip(o_refs, splits):
        o_ref[...] = y[:, off:off + s].astype(o_ref.dtype)
        off += s


def norm_matmul(x, g, w, splits, dtypes, tm=512):
    n, d = x.shape
    nout = w.shape[1]
    return pl.pallas_call(
        functools.partial(_norm_matmul_kernel, splits=splits),
        grid=(n // tm,),
        in_specs=[pl.BlockSpec((tm, d), lambda i: (i, 0)),
                  pl.BlockSpec((1, d), lambda i: (0, 0)),
                  pl.BlockSpec((d, nout), lambda i: (0, 0))],
        out_specs=[pl.BlockSpec((tm, s), lambda i: (i, 0)) for s in splits],
        out_shape=[jax.ShapeDtypeStruct((n, s), dt) for s, dt in zip(splits, dtypes)],
        compiler_params=_cparams(1, 48),
        name="norm_matmul",
    )(x, g.reshape(1, d), w)


def _store_rows_as_tiles(o3_ref, x):
    for c in range(o3_ref.shape[1]):
        o3_ref[:, c, :] = x[:, c * LANES:(c + 1) * LANES]


def _load_rows_from_tiles(x2_ref, rows, d_tiles):
    return jnp.concatenate([x2_ref[pl.ds(c, rows, stride=d_tiles), :] for c in range(d_tiles)], axis=1)


def _matmul_residual_kernel(r_ref, *refs, n_in):
    acc = r_ref[...]
    for a_ref, w_ref in zip(refs[:n_in], refs[n_in:2 * n_in]):
        acc = acc + _dot(a_ref[...], w_ref[...])
    refs[2 * n_in][...] = acc
    _store_rows_as_tiles(refs[2 * n_in + 1], acc)


def matmul_residual(r, a_list, w_list, tm=512):
    n, d = r.shape
    n_in = len(a_list)
    in_specs = [pl.BlockSpec((tm, d), lambda i: (i, 0))]
    in_specs += [pl.BlockSpec((tm, a.shape[1]), lambda i: (i, 0)) for a in a_list]
    in_specs += [pl.BlockSpec(w.shape, lambda i: (0, 0)) for w in w_list]
    return pl.pallas_call(
        functools.partial(_matmul_residual_kernel, n_in=n_in),
        grid=(n // tm,),
        in_specs=in_specs,
        out_specs=[pl.BlockSpec((tm, d), lambda i: (i, 0)),
                   pl.BlockSpec((tm, d // LANES, LANES), lambda i: (i, 0, 0))],
        out_shape=[jax.ShapeDtypeStruct((n, d), F32), jax.ShapeDtypeStruct((n, d // LANES, LANES), F32)],
        compiler_params=_cparams(1, 40),
        name="matmul_residual",
    )(r, *a_list, *w_list)


def _s5_kernel(u_ref, a_ref, b_ref, c_ref, d_ref, gw_ref, gb_ref, o_ref, s_ref, x_ref,
               *, steps, n_slab):
    half = SSM_SLAB_GROUPS * SSM_STATE
    slab = 2 * half

    @pl.when(pl.program_id(0) == 0)
    def _():
        x_ref[...] = jnp.zeros_like(x_ref)

    u = u_ref[...]
    ub = u.astype(BF16)
    for s in range(n_slab):
        s_ref[:, s * slab:(s + 1) * slab] = _dot(ub[:, s * LANES:(s + 1) * LANES], b_ref[s])

    for s in range(n_slab):
        re = slice(s * slab, s * slab + half)
        im = slice(s * slab + half, (s + 1) * slab)
        ar = a_ref[:, re]
        ai = a_ref[:, im]

        def step(t, carry, re=re, im=im, ar=ar, ai=ai):
            xr, xi = carry
            rows = pl.ds(pl.multiple_of(t * SUBLANES, SUBLANES), SUBLANES)
            nr = ar * xr - ai * xi + s_ref[rows, re]
            ni = ar * xi + ai * xr + s_ref[rows, im]
            s_ref[rows, re] = nr
            s_ref[rows, im] = ni
            return nr, ni

        xr, xi = lax.fori_loop(0, steps, step, (x_ref[:, re], x_ref[:, im]), unroll=8)
        x_ref[:, re] = xr
        x_ref[:, im] = xi

    y = jnp.concatenate(
        [_dot(s_ref[:, s * slab:(s + 1) * slab].astype(BF16), c_ref[s]) for s in range(n_slab)],
        axis=1)
    y = y + d_ref[...] * u
    z = jax.nn.gelu(y)
    gate = jax.nn.sigmoid(_dot(z.astype(BF16), gw_ref[...]) + gb_ref[...])
    o_ref[...] = (z * gate).astype(o_ref.dtype)


def s5_mixer(u_tb, a_mat, b_mat, c_mat, d_vec, glu_w, glu_b, steps=64):
    rows, width = u_tb.shape
    n_slab = width // LANES
    n_state = 2 * n_slab * SSM_SLAB_GROUPS * SSM_STATE
    tm = steps * SUBLANES
    return pl.pallas_call(
        functools.partial(_s5_kernel, steps=steps, n_slab=n_slab),
        grid=(rows // tm,),
        in_specs=[pl.BlockSpec((tm, width), lambda i: (i, 0)),
                  pl.BlockSpec(a_mat.shape, lambda i: (0, 0)),
                  pl.BlockSpec(b_mat.shape, lambda i: (0, 0, 0)),
                  pl.BlockSpec(c_mat.shape, lambda i: (0, 0, 0)),
                  pl.BlockSpec((1, width), lambda i: (0, 0)),
                  pl.BlockSpec(glu_w.shape, lambda i: (0, 0)),
                  pl.BlockSpec((1, width), lambda i: (0, 0))],
        out_specs=pl.BlockSpec((tm, width), lambda i: (i, 0)),
        out_shape=jax.ShapeDtypeStruct((rows, width), BF16),
        scratch_shapes=[pltpu.VMEM((tm, n_state), F32), pltpu.VMEM((SUBLANES, n_state), F32)],
        compiler_params=_cparams(1, 40),
        name="s5_mixer",
    )(u_tb, a_mat, b_mat, c_mat, d_vec, glu_w, glu_b)


def _s5_params(a_re, a_im, b_re, b_im, c_re, c_im, log_step):
    n_grp = a_re.shape[0]
    n_slab = n_grp // SSM_SLAB_GROUPS
    lam = lax.complex(a_re.astype(F32), a_im.astype(F32))
    step = jnp.exp(log_step.astype(F32))[:, None]
    lam_bar = jnp.exp(lam * step)
    b_bar = ((lam_bar - 1.0) / lam)[:, :, None] * lax.complex(b_re.astype(F32), b_im.astype(F32))
    eye = jnp.eye(SSM_SLAB_GROUPS, dtype=F32)

    def slabbed(t):
        return t.reshape((n_slab, SSM_SLAB_GROUPS) + t.shape[1:])

    a_mat = jnp.concatenate(
        [slabbed(jnp.real(lam_bar)).reshape(n_slab, -1), slabbed(jnp.imag(lam_bar)).reshape(n_slab, -1)],
        axis=1).reshape(1, -1)
    a_mat = jnp.broadcast_to(a_mat, (SUBLANES, a_mat.shape[1]))
    b_parts = [jnp.einsum('sgnh,gk->sghkn', slabbed(part(b_bar)), eye)
               for part in (jnp.real, jnp.imag)]
    b_mat = jnp.stack(b_parts, axis=3).reshape(n_slab, LANES, -1)
    c_parts = [jnp.einsum('sghn,gk->sgnkh', slabbed(part), eye)
               for part in (c_re.astype(F32), -c_im.astype(F32))]
    c_mat = jnp.stack(c_parts, axis=1).reshape(n_slab, -1, LANES)
    return a_mat, b_mat.astype(BF16), c_mat.astype(BF16)


def _by_head(lane, cols):
    out = cols[-1]
    for h in reversed(range(len(cols) - 1)):
        out = jnp.where(lane // HEAD_DIM == h, cols[h], out)
    return out


def _moba_kernel(q_ref, k_ref, v_ref, tb_ref, o_ref, mean_ref, km_ref, *, n_blk):
    blk = MOBA_BLOCK
    heads = MOBA_HEADS
    width = heads * HEAD_DIM
    rows_all = heads * blk
    qb = pl.program_id(2)

    @pl.when(qb == 0)
    def _():
        mean_ref[...] = jnp.zeros_like(mean_ref)
        for n in range(n_blk):
            kb = k_ref[0, n * blk:(n + 1) * blk, :].astype(F32)
            mean_ref[n:n + 1, :] = jnp.sum(kb, axis=0, keepdims=True) / blk
        mean = mean_ref[...]
        hi = mean.astype(BF16)
        km_ref[:, :width] = hi
        km_ref[:, width:] = (mean - hi.astype(F32)).astype(BF16)

    q = q_ref[0]
    lane = lax.broadcasted_iota(I32, (blk, width), 1)
    q_stack = jnp.concatenate(
        [jnp.where(lane // HEAD_DIM == h, q, jnp.zeros_like(q)) for h in range(heads)], axis=0)

    blane = lax.broadcasted_iota(I32, (rows_all, LANES), 1)
    gate = _dot_nt(jnp.concatenate([q_stack, q_stack], axis=1), km_ref[...])
    gate = jnp.where(blane < qb, gate, -jnp.inf)
    sel = jnp.zeros((rows_all, LANES), F32)
    blane_f = blane.astype(F32)
    for r in range(MOBA_TOPK):
        top = jnp.max(gate, axis=1, keepdims=True)
        idx = jnp.min(jnp.where(gate == top, blane_f, float(LANES)), axis=1, keepdims=True)
        hit = blane_f == idx
        sel = jnp.where(jnp.logical_and(hit, r < qb), 1.0, sel)
        gate = jnp.where(hit, -jnp.inf, gate)

    qs = q_stack * (HEAD_DIM ** -0.5)
    row = lax.broadcasted_iota(I32, (rows_all, blk), 0)
    col = lax.broadcasted_iota(I32, (rows_all, blk), 1)

    def rows_of(n):
        return pl.ds(pl.multiple_of(n * blk, blk), blk)

    def weighted_values(p, n):
        vn = v_ref[0, rows_of(n), :]
        p = p.astype(BF16)
        p_cat = jnp.concatenate([p[h * blk:(h + 1) * blk] for h in range(heads)], axis=1)
        v_stack = jnp.concatenate(
            [jnp.where(lane // HEAD_DIM == h, vn, jnp.zeros_like(vn)) for h in range(heads)], axis=0)
        return _dot(p_cat, v_stack)

    def per_head(x):
        return _by_head(lane, [x[h * blk:(h + 1) * blk] for h in range(heads)])

    s = _dot_nt(qs, k_ref[0, rows_of(qb), :]) + tb_ref[0, 0]
    s = jnp.where(col <= (row & (blk - 1)), s, NEG)
    m = jnp.max(s, axis=1, keepdims=True)
    p = jnp.exp(s - m)
    l = jnp.sum(p, axis=1, keepdims=True)
    acc = weighted_values(p, qb)

    def past_blocks(first, per_trip):
        def body(i, carry):
            m, l, acc = carry
            for u in range(per_trip):
                n = first + per_trip * i + u
                chosen = jnp.sum(jnp.where(blane == n, sel, 0.0), axis=1, keepdims=True) > 0.0
                s = _dot_nt(qs, k_ref[0, rows_of(n), :]) + tb_ref[0, jnp.minimum(qb - n, 2)]
                s = jnp.where(chosen, s, NEG)
                m_new = jnp.maximum(m, jnp.max(s, axis=1, keepdims=True))
                alpha = jnp.exp(m - m_new)
                p = jnp.exp(s - m_new)
                l = alpha * l + jnp.sum(p, axis=1, keepdims=True)
                acc = per_head(alpha) * acc + weighted_values(p, n)
                m = m_new
            return m, l, acc
        return body

    odd = qb & 1
    carry = lax.fori_loop(0, odd, past_blocks(0, 1), (m, l, acc))
    _, l, acc = lax.fori_loop(0, qb // 2, past_blocks(odd, 2), carry)
    o_ref[0] = (acc / per_head(l)).astype(o_ref.dtype)


def _rel_bucket(dist):
    exact = REL_BUCKETS // 2
    n = jnp.maximum(dist, 0)
    nf = jnp.maximum(n, 1).astype(F32)
    log_ratio = jnp.log(nf / exact) / math.log(REL_MAX_DIST / exact)
    large = exact + (log_ratio * (REL_BUCKETS - exact)).astype(I32)
    large = jnp.minimum(large, REL_BUCKETS - 1)
    return jnp.where(n < exact, n, large)


def _moba_bias_tables(rel_bias):
    assert REL_MAX_DIST <= MOBA_BLOCK + 1
    i = jnp.arange(MOBA_BLOCK)[:, None]
    j = jnp.arange(MOBA_BLOCK)[None, :]
    dist = jnp.arange(3)[:, None, None] * MOBA_BLOCK + (i - j)[None]
    onehot = (_rel_bucket(dist)[..., None] == jnp.arange(REL_BUCKETS)).astype(F32)
    return jnp.einsum('oijb,bh->hoij', onehot, rel_bias.astype(F32), precision=lax.Precision.HIGHEST)


def moba_mixer(qkv, rel_bias, width):
    bsz, seq, _ = qkv.shape
    blk = MOBA_BLOCK
    n_blk = seq // blk
    gw = MOBA_HEADS * HEAD_DIM
    n_grp = width // gw
    assert seq % blk == 0 and n_blk <= LANES
    tables = _moba_bias_tables(rel_bias).reshape(n_grp, MOBA_HEADS, 3, blk, blk)
    tables = tables.transpose(0, 2, 1, 3, 4).reshape(n_grp, 3, MOBA_HEADS * blk, blk)
    return pl.pallas_call(
        functools.partial(_moba_kernel, n_blk=n_blk),
        grid=(bsz, n_grp, n_blk),
        in_specs=[pl.BlockSpec((1, blk, gw), lambda b, p, i: (b, i, p)),
                  pl.BlockSpec((1, seq, gw), lambda b, p, i: (b, 0, n_grp + p)),
                  pl.BlockSpec((1, seq, gw), lambda b, p, i: (b, 0, 2 * n_grp + p)),
                  pl.BlockSpec((1, 3, MOBA_HEADS * blk, blk), lambda b, p, i: (p, 0, 0, 0))],
        out_specs=pl.BlockSpec((1, blk, gw), lambda b, p, i: (b, i, p)),
        out_shape=jax.ShapeDtypeStruct((bsz, seq, width), BF16),
        scratch_shapes=[pltpu.VMEM((LANES, gw), F32), pltpu.VMEM((LANES, 2 * gw), BF16)],
        compiler_params=_cparams(3, 48),
        name="moba_mixer",
    )(qkv, qkv, qkv, tables)


def _sb_kernel(q_ref, k_ref, v_ref, tri_ref, o_ref):
    blk = SB_BLOCK
    width = SB_HEADS * HEAD_DIM
    n_sub = blk // LANES
    rows_all = SB_HEADS * blk
    qb = pl.program_id(2)
    q = q_ref[0]
    lane = lax.broadcasted_iota(I32, (blk, width), 1)
    q_stack = jnp.concatenate(
        [jnp.where(lane // HEAD_DIM == h, q, jnp.zeros_like(q)) for h in range(SB_HEADS)],
        axis=0)
    row = lax.broadcasted_iota(I32, (rows_all, blk), 0)
    col = lax.broadcasted_iota(I32, (rows_all, blk), 1)
    past = col < (row & (blk - 1))

    def rows_of(n):
        return pl.ds(pl.multiple_of(n * blk, blk), blk)

    def logits(n):
        return _dot_nt(q_stack, k_ref[0, rows_of(n), :])

    def weights(z, carried, diagonal):
        neg_abs = lax.bitcast_convert_type(
            lax.bitcast_convert_type(z, jnp.uint32) | jnp.uint32(0x80000000), F32)
        drop = jnp.maximum(z, 0.0) + jnp.log2(1.0 + jnp.exp2(neg_abs))
        if diagonal:
            drop = jnp.where(past, drop, 0.0)
        hi32 = lax.bitcast_convert_type(
            lax.bitcast_convert_type(drop, jnp.uint32) & jnp.uint32(0xFFFF0000), F32)
        hi = hi32.astype(BF16)
        lo = (drop - hi32).astype(BF16)
        lhs = jnp.concatenate(
            [jnp.concatenate([hi[:, c * LANES:(c + 1) * LANES], lo[:, c * LANES:(c + 1) * LANES]], axis=1)
             for c in range(n_sub)], axis=0)
        sums = _dot(lhs, tri_ref[...])
        newer = carried
        from_key = [None] * n_sub
        for c in reversed(range(n_sub)):
            within = sums[c * rows_all:(c + 1) * rows_all, :LANES]
            total = sums[c * rows_all:(c + 1) * rows_all, LANES:]
            from_key[c] = within if newer is None else within + newer
            newer = total if newer is None else newer + total
        w = jnp.exp2(z - jnp.concatenate(from_key, axis=1))
        if diagonal:
            w = jnp.where(past, w, 0.0)
        w = w.astype(BF16)
        return jnp.concatenate([w[h * blk:(h + 1) * blk] for h in range(SB_HEADS)], axis=1), newer

    def weighted_values(w_cat, n):
        vn = v_ref[0, rows_of(n), :]
        v_stack = jnp.concatenate(
            [jnp.where(lane // HEAD_DIM == h, vn, jnp.zeros_like(vn)) for h in range(SB_HEADS)], axis=0)
        return _dot(w_cat, v_stack)

    w_cat, carried = weights(logits(qb), None, True)
    acc = weighted_values(w_cat, qb)

    def older_blocks(first, per_trip):
        def body(i, carry):
            acc, carried = carry
            for u in range(per_trip):
                n = first - per_trip * i - u
                w_cat, carried = weights(logits(n), carried, False)
                acc = acc + weighted_values(w_cat, n)
            return acc, carried
        return body

    odd = qb & 1
    acc, carried = lax.fori_loop(0, odd, older_blocks(qb - 1, 1), (acc, carried))
    acc, _ = lax.fori_loop(0, qb // 2, older_blocks(qb - 1 - odd, 2), (acc, carried))
    o_ref[0] = acc.astype(o_ref.dtype)


def stick_breaking_mixer(qkv, width):
    bsz, seq, _ = qkv.shape
    blk = SB_BLOCK
    gw = SB_HEADS * HEAD_DIM
    n_grp = width // gw
    tri = (jnp.arange(LANES)[:, None] >= jnp.arange(LANES)[None, :]).astype(BF16)
    tri = jnp.concatenate([tri, jnp.ones((LANES, LANES), BF16)], axis=1)
    tri = jnp.concatenate([tri, tri], axis=0)
    return pl.pallas_call(
        _sb_kernel,
        grid=(bsz, n_grp, seq // blk),
        in_specs=[pl.BlockSpec((1, blk, gw), lambda b, p, i: (b, i, p)),
                  pl.BlockSpec((1, seq, gw), lambda b, p, i: (b, 0, n_grp + p)),
                  pl.BlockSpec((1, seq, gw), lambda b, p, i: (b, 0, 2 * n_grp + p)),
                  pl.BlockSpec(tri.shape, lambda b, p, i: (0, 0))],
        out_specs=pl.BlockSpec((1, blk, gw), lambda b, p, i: (b, i, p)),
        out_shape=jax.ShapeDtypeStruct((bsz, seq, width), BF16),
        compiler_params=_cparams(3, 48),
        name="stick_breaking",
    )(qkv, qkv, qkv, tri)


def _router_kernel(r_ref, g_ref, w_ref, b_ref, idx_ref, gate_ref, rank_ref, cnt_ref, run_ref):
    tm = r_ref.shape[0]

    @pl.when(pl.program_id(0) == 0)
    def _():
        run_ref[...] = jnp.zeros_like(run_ref)

    h = _rms(r_ref[...], g_ref[...])
    h_hi = h.astype(BF16)
    h_lo = (h - h_hi.astype(F32)).astype(BF16)
    logits = _dot(jnp.concatenate([h_hi, h_hi, h_lo], axis=1), w_ref[...]) + b_ref[...]
    lane = lax.broadcasted_iota(I32, (tm, LANES), 1)
    lane_f = lane.astype(F32)
    tops, hits = [], []
    for _ in range(TOP_K):
        top = jnp.max(logits, axis=1, keepdims=True)
        idx = jnp.min(jnp.where(logits == top, lane_f, float(LANES)), axis=1, keepdims=True)
        hit = lane_f == idx
        logits = jnp.where(hit, -jnp.inf, logits)
        tops.append(top)
        hits.append(hit)
    exps = [jnp.exp(t - tops[0]) for t in tops]
    denom = exps[0]
    for e in exps[1:]:
        denom = denom + e

    member = jnp.zeros((tm, LANES), F32)
    for hit in hits:
        member = jnp.where(hit, 1.0, member)
    before = (lax.broadcasted_iota(I32, (tm, tm), 1) < lax.broadcasted_iota(I32, (tm, tm), 0))
    ahead = _dot(before.astype(BF16), member.astype(BF16)) + run_ref[...]

    idx_out = jnp.zeros((tm, LANES), F32)
    gate_out = jnp.zeros((tm, LANES), F32)
    rank_out = jnp.zeros((tm, LANES), F32)
    for k in range(TOP_K):
        idx_k = jnp.sum(jnp.where(hits[k], lane_f, 0.0), axis=1, keepdims=True)
        rank_k = jnp.sum(jnp.where(hits[k], ahead, 0.0), axis=1, keepdims=True)
        idx_out = jnp.where(lane == k, idx_k, idx_out)
        gate_out = jnp.where(lane == k, exps[k] / denom, gate_out)
        rank_out = jnp.where(lane == k, rank_k, rank_out)
    idx_ref[...] = idx_out.astype(I32)
    gate_ref[...] = gate_out
    rank_ref[...] = rank_out.astype(I32)
    run_ref[...] = run_ref[...] + jnp.sum(member, axis=0, keepdims=True)
    cnt_ref[...] = run_ref[...].astype(I32)


def moe_router(r, g, router_w, router_b, tm=512):
    n, d = r.shape
    n_exp = router_w.shape[1]
    w_pad = jnp.pad(router_w.astype(F32), ((0, 0), (0, LANES - n_exp)))
    w_hi = w_pad.astype(BF16)
    w_lo = (w_pad - w_hi.astype(F32)).astype(BF16)
    w_split = jnp.concatenate([w_hi, w_lo, w_hi], axis=0)
    b_pad = jnp.pad(router_b.astype(F32).reshape(1, n_exp), ((0, 0), (0, LANES - n_exp)),
                    constant_values=-jnp.inf)
    tile = pl.BlockSpec((tm, LANES), lambda i: (i, 0))
    idx, gates, rank, counts = pl.pallas_call(
        _router_kernel,
        grid=(n // tm,),
        in_specs=[pl.BlockSpec((tm, d), lambda i: (i, 0)),
                  pl.BlockSpec((1, d), lambda i: (0, 0)),
                  pl.BlockSpec((3 * d, LANES), lambda i: (0, 0)),
                  pl.BlockSpec((1, LANES), lambda i: (0, 0))],
        out_specs=[tile, tile, tile, pl.BlockSpec((1, LANES), lambda i: (0, 0))],
        out_shape=[jax.ShapeDtypeStruct((n, LANES), I32), jax.ShapeDtypeStruct((n, LANES), F32),
                   jax.ShapeDtypeStruct((n, LANES), I32), jax.ShapeDtypeStruct((1, LANES), I32)],
        scratch_shapes=[pltpu.VMEM((1, LANES), F32)],
        compiler_params=_cparams(1, 32),
        name="moe_router",
    )(r, g.reshape(1, d), w_split, b_pad)
    return idx[:, :TOP_K], gates, rank[:, :TOP_K], counts[0, :n_exp]


def _expert_kernel(be_ref, tok0_ref, tok1_ref, tok_ahead_ref, r_hbm, g_ref, wg_ref, bg_ref, wu_ref, bu_ref,
                   wd_ref, bd_ref, y_ref, buf_ref, wg_bf, wu_bf, wd_bf, sem):
    b = pl.program_id(0)
    last = pl.num_programs(0) - 1
    cur = lax.rem(b, GATHER_BUFS)
    ahead = lax.rem(b + GATHER_BUFS - 1, GATHER_BUFS)
    tiles = MOE_ROWS // SUBLANES
    d_tiles = r_hbm.shape[1]

    def row_copy(tok, buf, j):
        return pltpu.make_async_copy(
            r_hbm.at[tok], buf_ref.at[buf, pl.ds(j * d_tiles, d_tiles)], sem.at[buf])

    def start_gather_loop(toks, buf):
        def start(t, c):
            for sub in range(SUBLANES):
                j = t * SUBLANES + sub
                row_copy(toks[0, 0, j], buf, j).start()
            return c
        lax.fori_loop(0, tiles, start, 0)

    def wait_gather(buf):
        def wait(t, c):
            for sub in range(SUBLANES):
                row_copy(0, buf, 0).wait()
            return c
        lax.fori_loop(0, tiles, wait, 0)

    @pl.when(b == 0)
    def _():
        start_gather_loop(tok0_ref, 0)
        start_gather_loop(tok1_ref, 1)

    @pl.when(jnp.logical_or(b == 0, be_ref[b] != be_ref[jnp.maximum(b - 1, 0)]))
    def _():
        wg_bf[...] = wg_ref[0].astype(BF16)
        wu_bf[...] = wu_ref[0].astype(BF16)
        wd_bf[...] = wd_ref[0].astype(BF16)

    wait_gather(cur)
    h = _rms(_load_rows_from_tiles(buf_ref.at[cur], MOE_ROWS, d_tiles), g_ref[...]).astype(BF16)
    gate = jnp.minimum(_dot(h, wg_bf[...]) + bg_ref[0], SWIGLU_LIMIT)
    up = jnp.clip(_dot(h, wu_bf[...]) + bu_ref[0], -SWIGLU_LIMIT, SWIGLU_LIMIT)
    act = gate * jax.nn.sigmoid(SWIGLU_ALPHA * gate) * (up + 1.0)
    _store_rows_as_tiles(y_ref, _dot(act.astype(BF16), wd_bf[...]) + bd_ref[0])
    for j in range(MOE_ROWS):
        row_copy(tok_ahead_ref[0, 0, j], ahead, j).start(priority=j % DMA_QUEUES)

    @pl.when(b == last)
    def _():
        for back in range(1, GATHER_BUFS):
            wait_gather(lax.rem(b + back, GATHER_BUFS))


def moe_experts(r_tiles, g, slot_tok, block_expert, layer, w_gate, b_gate, w_up, b_up, w_down, b_down):
    n, d_tiles, _ = r_tiles.shape
    d = d_tiles * LANES
    n_layers, n_exp, _, d_ff = w_gate.shape
    n_blocks = block_expert.shape[0]
    assert n_blocks >= GATHER_BUFS
    toks = slot_tok.reshape(n_blocks, 1, MOE_ROWS)

    def expert4(i, be):
        return (layer, be[i], 0, 0)

    def tok_block(index):
        return pl.BlockSpec((1, 1, MOE_ROWS), index, memory_space=pltpu.SMEM)

    grid_spec = pltpu.PrefetchScalarGridSpec(
        num_scalar_prefetch=1,
        grid=(n_blocks,),
        in_specs=[tok_block(lambda i, be: (0, 0, 0)),
                  tok_block(lambda i, be: (1, 0, 0)),
                  tok_block(lambda i, be: (jnp.minimum(i + GATHER_BUFS - 1, n_blocks - 1), 0, 0)),
                  pl.BlockSpec(memory_space=pl.ANY),
                  pl.BlockSpec((1, d), lambda i, be: (0, 0)),
                  pl.BlockSpec((None, 1, d, d_ff), expert4), pl.BlockSpec((None, 1, 1, d_ff), expert4),
                  pl.BlockSpec((None, 1, d, d_ff), expert4), pl.BlockSpec((None, 1, 1, d_ff), expert4),
                  pl.BlockSpec((None, 1, d_ff, d), expert4), pl.BlockSpec((None, 1, 1, d), expert4)],
        out_specs=pl.BlockSpec((MOE_ROWS, d_tiles, LANES), lambda i, be: (i, 0, 0)),
        scratch_shapes=[pltpu.VMEM((GATHER_BUFS, MOE_ROWS * d_tiles, LANES), F32),
                        pltpu.VMEM((d, d_ff), BF16), pltpu.VMEM((d, d_ff), BF16), pltpu.VMEM((d_ff, d), BF16),
                        pltpu.SemaphoreType.DMA((GATHER_BUFS,))])
    return pl.pallas_call(
        _expert_kernel,
        grid_spec=grid_spec,
        out_shape=jax.ShapeDtypeStruct((n_blocks * MOE_ROWS, d_tiles, LANES), F32),
        compiler_params=_cparams(1, 56),
        name="moe_experts",
    )(block_expert, toks, toks, toks, r_tiles, g.reshape(1, d),
      w_gate, b_gate.reshape(n_layers, n_exp, 1, d_ff), w_up, b_up.reshape(n_layers, n_exp, 1, d_ff),
      w_down, b_down.reshape(n_layers, n_exp, 1, d))


def _combine_kernel(dest0_ref, dest1_ref, dest_ahead_ref, ys_hbm, gate_ref, r_ref, p_ref, gp_ref, wp_ref,
                    wpg_ref, gf_ref, o_ref, buf_ref, sem, *, tile, final_norm):
    i = pl.program_id(0)
    last = pl.num_programs(0) - 1
    cur = lax.rem(i, GATHER_BUFS)
    ahead = lax.rem(i + GATHER_BUFS - 1, GATHER_BUFS)
    tiles = tile // SUBLANES
    d_tiles = ys_hbm.shape[1]

    def row_copy(slot, buf, k, j):
        return pltpu.make_async_copy(
            ys_hbm.at[slot], buf_ref.at[buf, k, pl.ds(j * d_tiles, d_tiles)], sem.at[buf])

    def start_gather_loop(dests, buf):
        def start(t, c):
            for sub in range(SUBLANES):
                for k in range(TOP_K):
                    j = t * SUBLANES + sub
                    row_copy(dests[0, 0, j * TOP_K + k], buf, k, j).start()
            return c
        lax.fori_loop(0, tiles, start, 0)

    def wait_gather(buf):
        def wait(t, c):
            for sub in range(SUBLANES):
                for k in range(TOP_K):
                    row_copy(0, buf, k, 0).wait()
            return c
        lax.fori_loop(0, tiles, wait, 0)

    @pl.when(i == 0)
    def _():
        start_gather_loop(dest0_ref, 0)
        start_gather_loop(dest1_ref, 1)

    wait_gather(cur)
    gates = gate_ref[...]
    r = r_ref[...]
    for k in range(TOP_K):
        r = r + _load_rows_from_tiles(buf_ref.at[cur, k], tile, d_tiles) * gates[:, k:k + 1]
    h = _rms(r, gp_ref[...]).astype(BF16)
    gate = jax.nn.sigmoid(_dot(h, wpg_ref[...]))
    r = r + _dot(p_ref[...].astype(BF16), wp_ref[...]) * gate
    if final_norm:
        r = _rms(r, gf_ref[...])
    o_ref[...] = r
    for j in range(tile):
        for k in range(TOP_K):
            row_copy(dest_ahead_ref[0, 0, j * TOP_K + k], ahead, k, j).start(priority=k % DMA_QUEUES)

    @pl.when(i == last)
    def _():
        for back in range(1, GATHER_BUFS):
            wait_gather(lax.rem(i + back, GATHER_BUFS))


def moe_combine_embed(ys, dest, gates, r, p, g_ple, w_ple, w_ple_gate, g_final, final_norm, tile=256):
    n, d = r.shape
    pd = p.shape[1]
    n_tiles = n // tile
    assert n_tiles >= GATHER_BUFS
    dest_tiles = dest.reshape(n_tiles, 1, tile * TOP_K)

    def dest_block(index):
        return pl.BlockSpec((1, 1, tile * TOP_K), index, memory_space=pltpu.SMEM)

    return pl.pallas_call(
        functools.partial(_combine_kernel, tile=tile, final_norm=final_norm),
        grid=(n_tiles,),
        in_specs=[dest_block(lambda i: (0, 0, 0)),
                  dest_block(lambda i: (1, 0, 0)),
                  dest_block(lambda i: (jnp.minimum(i + GATHER_BUFS - 1, n_tiles - 1), 0, 0)),
                  pl.BlockSpec(memory_space=pl.ANY),
                  pl.BlockSpec((tile, LANES), lambda i: (i, 0)),
                  pl.BlockSpec((tile, d), lambda i: (i, 0)),
                  pl.BlockSpec((tile, pd), lambda i: (i, 0)),
                  pl.BlockSpec((1, d), lambda i: (0, 0)),
                  pl.BlockSpec((pd, d), lambda i: (0, 0)),
                  pl.BlockSpec((d, d), lambda i: (0, 0)),
                  pl.BlockSpec((1, d), lambda i: (0, 0))],
        out_specs=pl.BlockSpec((tile, d), lambda i: (i, 0)),
        out_shape=jax.ShapeDtypeStruct((n, d), F32),
        scratch_shapes=[pltpu.VMEM((GATHER_BUFS, TOP_K, tile * (d // LANES), LANES), F32),
                        pltpu.SemaphoreType.DMA((GATHER_BUFS,))],
        compiler_params=_cparams(1, 48),
        name="moe_combine_embed",
    )(dest_tiles, dest_tiles, dest_tiles, ys, gates, r, p, g_ple.reshape(1, d), w_ple, w_ple_gate,
      g_final.reshape(1, d))


def _slot_layout(idx, rank, counts, n_blocks):
    n_exp = counts.shape[0]
    padded = (counts + MOE_ROWS - 1) // MOE_ROWS * MOE_ROWS
    pad_end = jnp.cumsum(padded)
    pad_start = pad_end - padded
    cnt_start = jnp.cumsum(counts) - counts
    start_of = jnp.sum(jnp.where(idx[..., None] == jnp.arange(n_exp), pad_start, 0), axis=-1)
    dest = (start_of + rank).astype(I32).reshape(-1)
    block_first_row = jnp.arange(n_blocks) * MOE_ROWS
    block_expert = jnp.minimum(
        jnp.sum(pad_end[None, :] <= block_first_row[:, None], axis=1), n_exp - 1).astype(I32)
    sorted_tok = (jnp.argsort(dest) // TOP_K).astype(I32)
    in_expert = block_first_row[:, None] + jnp.arange(MOE_ROWS)[None, :] - pad_start[block_expert][:, None]
    real = in_expert < counts[block_expert][:, None]
    compact = jnp.clip(in_expert + cnt_start[block_expert][:, None], 0, dest.shape[0] - 1)
    slot_tok = jnp.where(real, sorted_tok[compact], 0).astype(I32)
    return dest, slot_tok, block_expert


def moe_and_embed(r, r_tiles, p, g_ffn, router_w, router_b, layer, w_gate, b_gate, w_up, b_up, w_down, b_down,
                  g_ple, w_ple, w_ple_gate, g_final, final_norm):
    n, _ = r.shape
    n_blocks = n * TOP_K // MOE_ROWS + router_w.shape[1]
    idx, gates, rank, counts = moe_router(r, g_ffn, router_w, router_b)
    dest, slot_tok, block_expert = _slot_layout(idx, rank, counts, n_blocks)
    ys = moe_experts(r_tiles, g_ffn, slot_tok, block_expert, layer, w_gate.astype(F32), b_gate.astype(F32),
                     w_up.astype(F32), b_up.astype(F32), w_down.astype(F32), b_down.astype(F32))
    return moe_combine_embed(ys, dest, gates, r, p, g_ple, w_ple.astype(BF16), w_ple_gate.astype(BF16),
                             g_final, final_norm)


def kernel(x, p, norm_mix, norm_ffn, norm_ple, norm_final, w_in_ab, ssm_a_re, ssm_a_im, ssm_b_re, ssm_b_im, ssm_c_re, ssm_c_im, ssm_d, ssm_log_step, glu_w, glu_b, w_out_ab, rel_bias, w_in_c, w_out_c, router_w, router_b, w_gate, b_gate, w_up, b_up, w_down, b_down, w_ple, w_ple_gate):
    bsz, seq, d = x.shape
    n = bsz * seq
    depth = p.shape[0]
    assert bsz == SUBLANES
    r = x.reshape(n, d).astype(F32)
    for i in range(depth):
        j = i // 2
        if i % 2 == 0:
            ssm_w = ssm_d.shape[1] * ssm_d.shape[2]
            moba_w = (w_in_ab.shape[2] - ssm_w) // 3
            u, qkv = norm_matmul(r, norm_mix[i], w_in_ab[j].astype(BF16),
                                 (ssm_w, 3 * moba_w), (F32, BF16))
            a_mat, b_mat, c_mat = _s5_params(ssm_a_re[j], ssm_a_im[j], ssm_b_re[j], ssm_b_im[j],
                                             ssm_c_re[j], ssm_c_im[j], ssm_log_step[j])
            u_tb = u.reshape(bsz, seq, ssm_w).transpose(1, 0, 2).reshape(n, ssm_w)
            y_a = s5_mixer(u_tb, a_mat, b_mat, c_mat, ssm_d[j].reshape(1, ssm_w).astype(F32),
                           glu_w[j].astype(BF16), glu_b[j].reshape(1, ssm_w).astype(F32))
            y_a = y_a.reshape(seq, bsz, ssm_w).transpose(1, 0, 2).reshape(n, ssm_w)
            y_b = moba_mixer(qkv.reshape(bsz, seq, 3 * moba_w), rel_bias, moba_w).reshape(n, moba_w)
            w_out = w_out_ab[j].astype(BF16)
            r, r_tiles = matmul_residual(r, [y_a, y_b], [w_out[:ssm_w], w_out[ssm_w:]])
        else:
            q_scale = jnp.where(jnp.arange(3 * d) < d, HEAD_DIM ** -0.5 * math.log2(math.e), 1.0)
            w_in = (w_in_c[j].astype(F32) * q_scale).astype(BF16)
            (qkv,) = norm_matmul(r, norm_mix[i], w_in, (3 * d,), (BF16,))
            y_c = stick_breaking_mixer(qkv.reshape(bsz, seq, 3 * d), d).reshape(n, d)
            r, r_tiles = matmul_residual(r, [y_c], [w_out_c[j].astype(BF16)])
        r = moe_and_embed(r, r_tiles, p[i].reshape(n, -1), norm_ffn[i], router_w[i], router_b[i],
                          i, w_gate, b_gate, w_up, b_up, w_down, b_down,
                          norm_ple[i], w_ple[i], w_ple_gate[i], norm_final, i == depth - 1)
    return r.reshape(bsz, seq, d).astype(x.dtype)
```

```python
import functools
import math

import jax
import jax.numpy as jnp
from jax import lax
from jax.experimental import pallas as pl
from jax.experimental.pallas import tpu as pltpu

F32 = jnp.float32
BF16 = jnp.bfloat16
I32 = jnp.int32

RMS_EPS = 1e-6
HEAD_DIM = 64
LANES = 128
SUBLANES = 8
SSM_GROUP = 16
SSM_STATE = 64
SSM_SLAB_GROUPS = LANES // SSM_GROUP
MOBA_BLOCK = 256
MOBA_TOPK = 3
MOBA_HEADS = 4
REL_BUCKETS = 32
REL_MAX_DIST = 128
SB_BLOCK = 256
SB_HEADS = 4
N_EXPERTS = 32
TOP_K = 4
SWIGLU_LIMIT = 7.0
SWIGLU_ALPHA = 1.702
MOE_ROWS = 256
GATHER_BUFS = 3
DMA_QUEUES = 2
NEG = -1e30
MIB = 1024 * 1024

_NT = (((1,), (1,)), ((), ()))


def _cparams(n_axes, vmem_mib):
    return pltpu.CompilerParams(
        dimension_semantics=("arbitrary",) * n_axes, vmem_limit_bytes=vmem_mib * MIB)


def _rms(x, g):
    ms = jnp.mean(x * x, axis=-1, keepdims=True)
    return x * lax.rsqrt(ms + RMS_EPS) * g


def _dot(a, b):
    return jnp.dot(a, b, preferred_element_type=F32)


def _dot_nt(a, b):
    return lax.dot_general(a, b, _NT, preferred_element_type=F32)


def _norm_matmul_kernel(x_ref, g_ref, w_ref, *o_refs, splits):
    h = _rms(x_ref[...], g_ref[...]).astype(BF16)
    y = _dot(h, w_ref[...])
    off = 0
    for o_ref, s in zip(o_refs, splits):
        o_ref[...] = y[:, off:off + s].astype(o_ref.dtype)
        off += s


def norm_matmul(x, g, w, splits, dtypes, tm=512):
    n, d = x.shape
    nout = w.shape[1]
    return pl.pallas_call(
        functools.partial(_norm_matmul_kernel, splits=splits),
        grid=(n // tm,),
        in_specs=[pl.BlockSpec((tm, d), lambda i: (i, 0)),
                  pl.BlockSpec((1, d), lambda i: (0, 0)),
                  pl.BlockSpec((d, nout), lambda i: (0, 0))],
        out_specs=[pl.BlockSpec((tm, s), lambda i: (i, 0)) for s in splits],
        out_shape=[jax.ShapeDtypeStruct((n, s), dt) for s, dt in zip(splits, dtypes)],
        compiler_params=_cparams(1, 48),
        name="norm_matmul",
    )(x, g.reshape(1, d), w)


def _store_rows_as_tiles(o3_ref, x):
    for c in range(o3_ref.shape[1]):
        o3_ref[:, c, :] = x[:, c * LANES:(c + 1) * LANES]


def _load_rows_from_tiles(x2_ref, rows, d_tiles):
    return jnp.concatenate([x2_ref[pl.ds(c, rows, stride=d_tiles), :] for c in range(d_tiles)], axis=1)


def _matmul_residual_kernel(r_ref, *refs, n_in):
    acc = r_ref[...]
    for a_ref, w_ref in zip(refs[:n_in], refs[n_in:2 * n_in]):
        acc = acc + _dot(a_ref[...], w_ref[...])
    refs[2 * n_in][...] = acc
    _store_rows_as_tiles(refs[2 * n_in + 1], acc)


def matmul_residual(r, a_list, w_list, tm=512):
    n, d = r.shape
    n_in = len(a_list)
    in_specs = [pl.BlockSpec((tm, d), lambda i: (i, 0))]
    in_specs += [pl.BlockSpec((tm, a.shape[1]), lambda i: (i, 0)) for a in a_list]
    in_specs += [pl.BlockSpec(w.shape, lambda i: (0, 0)) for w in w_list]
    return pl.pallas_call(
        functools.partial(_matmul_residual_kernel, n_in=n_in),
        grid=(n // tm,),
        in_specs=in_specs,
        out_specs=[pl.BlockSpec((tm, d), lambda i: (i, 0)),
                   pl.BlockSpec((tm, d // LANES, LANES), lambda i: (i, 0, 0))],
        out_shape=[jax.ShapeDtypeStruct((n, d), F32), jax.ShapeDtypeStruct((n, d // LANES, LANES), F32)],
        compiler_params=_cparams(1, 40),
        name="matmul_residual",
    )(r, *a_list, *w_list)


def _s5_kernel(u_ref, a_ref, b_ref, c_ref, d_ref, gw_ref, gb_ref, o_ref, s_ref, x_ref,
               *, steps, n_slab):
    half = SSM_SLAB_GROUPS * SSM_STATE
    slab = 2 * half

    @pl.when(pl.program_id(0) == 0)
    def _():
        x_ref[...] = jnp.zeros_like(x_ref)

    u = u_ref[...]
    ub = u.astype(BF16)
    for s in range(n_slab):
        s_ref[:, s * slab:(s + 1) * slab] = _dot(ub[:, s * LANES:(s + 1) * LANES], b_ref[s])

    for s in range(n_slab):
        re = slice(s * slab, s * slab + half)
        im = slice(s * slab + half, (s + 1) * slab)
        ar = a_ref[:, re]
        ai = a_ref[:, im]

        def step(t, carry, re=re, im=im, ar=ar, ai=ai):
            xr, xi = carry
            rows = pl.ds(pl.multiple_of(t * SUBLANES, SUBLANES), SUBLANES)
            nr = ar * xr - ai * xi + s_ref[rows, re]
            ni = ar * xi + ai * xr + s_ref[rows, im]
            s_ref[rows, re] = nr
            s_ref[rows, im] = ni
            return nr, ni

        xr, xi = lax.fori_loop(0, steps, step, (x_ref[:, re], x_ref[:, im]), unroll=8)
        x_ref[:, re] = xr
        x_ref[:, im] = xi

    y = jnp.concatenate(
        [_dot(s_ref[:, s * slab:(s + 1) * slab].astype(BF16), c_ref[s]) for s in range(n_slab)],
        axis=1)
    y = y + d_ref[...] * u
    z = jax.nn.gelu(y)
    gate = jax.nn.sigmoid(_dot(z.astype(BF16), gw_ref[...]) + gb_ref[...])
    o_ref[...] = (z * gate).astype(o_ref.dtype)


def s5_mixer(u_tb, a_mat, b_mat, c_mat, d_vec, glu_w, glu_b, steps=64):
    rows, width = u_tb.shape
    n_slab = width // LANES
    n_state = 2 * n_slab * SSM_SLAB_GROUPS * SSM_STATE
    tm = steps * SUBLANES
    return pl.pallas_call(
        functools.partial(_s5_kernel, steps=steps, n_slab=n_slab),
        grid=(rows // tm,),
        in_specs=[pl.BlockSpec((tm, width), lambda i: (i, 0)),
                  pl.BlockSpec(a_mat.shape, lambda i: (0, 0)),
                  pl.BlockSpec(b_mat.shape, lambda i: (0, 0, 0)),
                  pl.BlockSpec(c_mat.shape, lambda i: (0, 0, 0)),
                  pl.BlockSpec((1, width), lambda i: (0, 0)),
                  pl.BlockSpec(glu_w.shape, lambda i: (0, 0)),
                  pl.BlockSpec((1, width), lambda i: (0, 0))],
        out_specs=pl.BlockSpec((tm, width), lambda i: (i, 0)),
        out_shape=jax.ShapeDtypeStruct((rows, width), BF16),
        scratch_shapes=[pltpu.VMEM((tm, n_state), F32), pltpu.VMEM((SUBLANES, n_state), F32)],
        compiler_params=_cparams(1, 40),
        name="s5_mixer",
    )(u_tb, a_mat, b_mat, c_mat, d_vec, glu_w, glu_b)


def _s5_params(a_re, a_im, b_re, b_im, c_re, c_im, log_step):
    n_grp = a_re.shape[0]
    n_slab = n_grp // SSM_SLAB_GROUPS
    lam = lax.complex(a_re.astype(F32), a_im.astype(F32))
    step = jnp.exp(log_step.astype(F32))[:, None]
    lam_bar = jnp.exp(lam * step)
    b_bar = ((lam_bar - 1.0) / lam)[:, :, None] * lax.complex(b_re.astype(F32), b_im.astype(F32))
    eye = jnp.eye(SSM_SLAB_GROUPS, dtype=F32)

    def slabbed(t):
        return t.reshape((n_slab, SSM_SLAB_GROUPS) + t.shape[1:])

    a_mat = jnp.concatenate(
        [slabbed(jnp.real(lam_bar)).reshape(n_slab, -1), slabbed(jnp.imag(lam_bar)).reshape(n_slab, -1)],
        axis=1).reshape(1, -1)
    a_mat = jnp.broadcast_to(a_mat, (SUBLANES, a_mat.shape[1]))
    b_parts = [jnp.einsum('sgnh,gk->sghkn', slabbed(part(b_bar)), eye)
               for part in (jnp.real, jnp.imag)]
    b_mat = jnp.stack(b_parts, axis=3).reshape(n_slab, LANES, -1)
    c_parts = [jnp.einsum('sghn,gk->sgnkh', slabbed(part), eye)
               for part in (c_re.astype(F32), -c_im.astype(F32))]
    c_mat = jnp.stack(c_parts, axis=1).reshape(n_slab, -1, LANES)
    return a_mat, b_mat.astype(BF16), c_mat.astype(BF16)


def _by_head(lane, cols):
    out = cols[-1]
    for h in reversed(range(len(cols) - 1)):
        out = jnp.where(lane // HEAD_DIM == h, cols[h], out)
    return out


def _moba_kernel(q_ref, k_ref, v_ref, tb_ref, o_ref, mean_ref, km_ref, *, n_blk):
    blk = MOBA_BLOCK
    heads = MOBA_HEADS
    width = heads * HEAD_DIM
    rows_all = heads * blk
    qb = pl.program_id(2)

    @pl.when(qb == 0)
    def _():
        mean_ref[...] = jnp.zeros_like(mean_ref)
        for n in range(n_blk):
            kb = k_ref[0, n * blk:(n + 1) * blk, :].astype(F32)
            mean_ref[n:n + 1, :] = jnp.sum(kb, axis=0, keepdims=True) / blk
        mean = mean_ref[...]
        hi = mean.astype(BF16)
        km_ref[:, :width] = hi
        km_ref[:, width:] = (mean - hi.astype(F32)).astype(BF16)

    q = q_ref[0]
    lane = lax.broadcasted_iota(I32, (blk, width), 1)
    q_stack = jnp.concatenate(
        [jnp.where(lane // HEAD_DIM == h, q, jnp.zeros_like(q)) for h in range(heads)], axis=0)

    blane = lax.broadcasted_iota(I32, (rows_all, LANES), 1)
    gate = _dot_nt(jnp.concatenate([q_stack, q_stack], axis=1), km_ref[...])
    gate = jnp.where(blane < qb, gate, -jnp.inf)
    sel = jnp.zeros((rows_all, LANES), F32)
    blane_f = blane.astype(F32)
    for r in range(MOBA_TOPK):
        top = jnp.max(gate, axis=1, keepdims=True)
        idx = jnp.min(jnp.where(gate == top, blane_f, float(LANES)), axis=1, keepdims=True)
        hit = blane_f == idx
        sel = jnp.where(jnp.logical_and(hit, r < qb), 1.0, sel)
        gate = jnp.where(hit, -jnp.inf, gate)

    qs = q_stack * (HEAD_DIM ** -0.5)
    row = lax.broadcasted_iota(I32, (rows_all, blk), 0)
    col = lax.broadcasted_iota(I32, (rows_all, blk), 1)

    def rows_of(n):
        return pl.ds(pl.multiple_of(n * blk, blk), blk)

    def weighted_values(p, n):
        vn = v_ref[0, rows_of(n), :]
        p = p.astype(BF16)
        p_cat = jnp.concatenate([p[h * blk:(h + 1) * blk] for h in range(heads)], axis=1)
        v_stack = jnp.concatenate(
            [jnp.where(lane // HEAD_DIM == h, vn, jnp.zeros_like(vn)) for h in range(heads)], axis=0)
        return _dot(p_cat, v_stack)

    def per_head(x):
        return _by_head(lane, [x[h * blk:(h + 1) * blk] for h in range(heads)])

    s = _dot_nt(qs, k_ref[0, rows_of(qb), :]) + tb_ref[0, 0]
    s = jnp.where(col <= (row & (blk - 1)), s, NEG)
    m = jnp.max(s, axis=1, keepdims=True)
    p = jnp.exp(s - m)
    l = jnp.sum(p, axis=1, keepdims=True)
    acc = weighted_values(p, qb)

    def past_blocks(first, per_trip):
        def body(i, carry):
            m, l, acc = carry
            for u in range(per_trip):
                n = first + per_trip * i + u
                chosen = jnp.sum(jnp.where(blane == n, sel, 0.0), axis=1, keepdims=True) > 0.0
                s = _dot_nt(qs, k_ref[0, rows_of(n), :]) + tb_ref[0, jnp.minimum(qb - n, 2)]
                s = jnp.where(chosen, s, NEG)
                m_new = jnp.maximum(m, jnp.max(s, axis=1, keepdims=True))
                alpha = jnp.exp(m - m_new)
                p = jnp.exp(s - m_new)
                l = alpha * l + jnp.sum(p, axis=1, keepdims=True)
                acc = per_head(alpha) * acc + weighted_values(p, n)
                m = m_new
            return m, l, acc
        return body

    odd = qb & 1
    carry = lax.fori_loop(0, odd, past_blocks(0, 1), (m, l, acc))
    _, l, acc = lax.fori_loop(0, qb // 2, past_blocks(odd, 2), carry)
    o_ref[0] = (acc / per_head(l)).astype(o_ref.dtype)


def _rel_bucket(dist):
    exact = REL_BUCKETS // 2
    n = jnp.maximum(dist, 0)
    nf = jnp.maximum(n, 1).astype(F32)
    log_ratio = jnp.log(nf / exact) / math.log(REL_MAX_DIST / exact)
    large = exact + (log_ratio * (REL_BUCKETS - exact)).astype(I32)
    large = jnp.minimum(large, REL_BUCKETS - 1)
    return jnp.where(n < exact, n, large)


def _moba_bias_tables(rel_bias):
    assert REL_MAX_DIST <= MOBA_BLOCK + 1
    i = jnp.arange(MOBA_BLOCK)[:, None]
    j = jnp.arange(MOBA_BLOCK)[None, :]
    dist = jnp.arange(3)[:, None, None] * MOBA_BLOCK + (i - j)[None]
    onehot = (_rel_bucket(dist)[..., None] == jnp.arange(REL_BUCKETS)).astype(F32)
    return jnp.einsum('oijb,bh->hoij', onehot, rel_bias.astype(F32), precision=lax.Precision.HIGHEST)


def moba_mixer(qkv, rel_bias, width):
    bsz, seq, _ = qkv.shape
    blk = MOBA_BLOCK
    n_blk = seq // blk
    gw = MOBA_HEADS * HEAD_DIM
    n_grp = width // gw
    assert seq % blk == 0 and n_blk <= LANES
    tables = _moba_bias_tables(rel_bias).reshape(n_grp, MOBA_HEADS, 3, blk, blk)
    tables = tables.transpose(0, 2, 1, 3, 4).reshape(n_grp, 3, MOBA_HEADS * blk, blk)
    return pl.pallas_call(
        functools.partial(_moba_kernel, n_blk=n_blk),
        grid=(bsz, n_grp, n_blk),
        in_specs=[pl.BlockSpec((1, blk, gw), lambda b, p, i: (b, i, p)),
                  pl.BlockSpec((1, seq, gw), lambda b, p, i: (b, 0, n_grp + p)),
                  pl.BlockSpec((1, seq, gw), lambda b, p, i: (b, 0, 2 * n_grp + p)),
                  pl.BlockSpec((1, 3, MOBA_HEADS * blk, blk), lambda b, p, i: (p, 0, 0, 0))],
        out_specs=pl.BlockSpec((1, blk, gw), lambda b, p, i: (b, i, p)),
        out_shape=jax.ShapeDtypeStruct((bsz, seq, width), BF16),
        scratch_shapes=[pltpu.VMEM((LANES, gw), F32), pltpu.VMEM((LANES, 2 * gw), BF16)],
        compiler_params=_cparams(3, 48),
        name="moba_mixer",
    )(qkv, qkv, qkv, tables)


def _sb_kernel(q_ref, k_ref, v_ref, tri_ref, o_ref):
    blk = SB_BLOCK
    width = SB_HEADS * HEAD_DIM
    n_sub = blk // LANES
    rows_all = SB_HEADS * blk
    qb = pl.program_id(2)
    q = q_ref[0]
    lane = lax.broadcasted_iota(I32, (blk, width), 1)
    q_stack = jnp.concatenate(
        [jnp.where(lane // HEAD_DIM == h, q, jnp.zeros_like(q)) for h in range(SB_HEADS)],
        axis=0)
    row = lax.broadcasted_iota(I32, (rows_all, blk), 0)
    col = lax.broadcasted_iota(I32, (rows_all, blk), 1)
    past = col < (row & (blk - 1))

    def rows_of(n):
        return pl.ds(pl.multiple_of(n * blk, blk), blk)

    def logits(n):
        return _dot_nt(q_stack, k_ref[0, rows_of(n), :])

    def weights(z, carried, diagonal):
        neg_abs = lax.bitcast_convert_type(
            lax.bitcast_convert_type(z, jnp.uint32) | jnp.uint32(0x80000000), F32)
        drop = jnp.maximum(z, 0.0) + jnp.log2(1.0 + jnp.exp2(neg_abs))
        if diagonal:
            drop = jnp.where(past, drop, 0.0)
        hi32 = lax.bitcast_convert_type(
            lax.bitcast_convert_type(drop, jnp.uint32) & jnp.uint32(0xFFFF0000), F32)
        hi = hi32.astype(BF16)
        lo = (drop - hi32).astype(BF16)
        lhs = jnp.concatenate(
            [jnp.concatenate([hi[:, c * LANES:(c + 1) * LANES], lo[:, c * LANES:(c + 1) * LANES]], axis=1)
             for c in range(n_sub)], axis=0)
        sums = _dot(lhs, tri_ref[...])
        newer = carried
        from_key = [None] * n_sub
        for c in reversed(range(n_sub)):
            within = sums[c * rows_all:(c + 1) * rows_all, :LANES]
            total = sums[c * rows_all:(c + 1) * rows_all, LANES:]
            from_key[c] = within if newer is None else within + newer
            newer = total if newer is None else newer + total
        w = jnp.exp2(z - jnp.concatenate(from_key, axis=1))
        if diagonal:
            w = jnp.where(past, w, 0.0)
        w = w.astype(BF16)
        return jnp.concatenate([w[h * blk:(h + 1) * blk] for h in range(SB_HEADS)], axis=1), newer

    def weighted_values(w_cat, n):
        vn = v_ref[0, rows_of(n), :]
        v_stack = jnp.concatenate(
            [jnp.where(lane // HEAD_DIM == h, vn, jnp.zeros_like(vn)) for h in range(SB_HEADS)], axis=0)
        return _dot(w_cat, v_stack)

    w_cat, carried = weights(logits(qb), None, True)
    acc = weighted_values(w_cat, qb)

    def older_blocks(first, per_trip):
        def body(i, carry):
            acc, carried = carry
            for u in range(per_trip):
                n = first - per_trip * i - u
                w_cat, carried = weights(logits(n), carried, False)
                acc = acc + weighted_values(w_cat, n)
            return acc, carried
        return body

    odd = qb & 1
    acc, carried = lax.fori_loop(0, odd, older_blocks(qb - 1, 1), (acc, carried))
    acc, _ = lax.fori_loop(0, qb // 2, older_blocks(qb - 1 - odd, 2), (acc, carried))
    o_ref[0] = acc.astype(o_ref.dtype)


def stick_breaking_mixer(qkv, width):
    bsz, seq, _ = qkv.shape
    blk = SB_BLOCK
    gw = SB_HEADS * HEAD_DIM
    n_grp = width // gw
    tri = (jnp.arange(LANES)[:, None] >= jnp.arange(LANES)[None, :]).astype(BF16)
    tri = jnp.concatenate([tri, jnp.ones((LANES, LANES), BF16)], axis=1)
    tri = jnp.concatenate([tri, tri], axis=0)
    return pl.pallas_call(
        _sb_kernel,
        grid=(bsz, n_grp, seq // blk),
        in_specs=[pl.BlockSpec((1, blk, gw), lambda b, p, i: (b, i, p)),
                  pl.BlockSpec((1, seq, gw), lambda b, p, i: (b, 0, n_grp + p)),
                  pl.BlockSpec((1, seq, gw), lambda b, p, i: (b, 0, 2 * n_grp + p)),
                  pl.BlockSpec(tri.shape, lambda b, p, i: (0, 0))],
        out_specs=pl.BlockSpec((1, blk, gw), lambda b, p, i: (b, i, p)),
        out_shape=jax.ShapeDtypeStruct((bsz, seq, width), BF16),
        compiler_params=_cparams(3, 48),
        name="stick_breaking",
    )(qkv, qkv, qkv, tri)


def _router_kernel(r_ref, g_ref, w_ref, b_ref, idx_ref, gate_ref, rank_ref, cnt_ref, run_ref):
    tm = r_ref.shape[0]

    @pl.when(pl.program_id(0) == 0)
    def _():
        run_ref[...] = jnp.zeros_like(run_ref)

    h = _rms(r_ref[...], g_ref[...])
    h_hi = h.astype(BF16)
    h_lo = (h - h_hi.astype(F32)).astype(BF16)
    logits = _dot(jnp.concatenate([h_hi, h_hi, h_lo], axis=1), w_ref[...]) + b_ref[...]
    lane = lax.broadcasted_iota(I32, (tm, LANES), 1)
    lane_f = lane.astype(F32)
    tops, hits = [], []
    for _ in range(TOP_K):
        top = jnp.max(logits, axis=1, keepdims=True)
        idx = jnp.min(jnp.where(logits == top, lane_f, float(LANES)), axis=1, keepdims=True)
        hit = lane_f == idx
        logits = jnp.where(hit, -jnp.inf, logits)
        tops.append(top)
        hits.append(hit)
    exps = [jnp.exp(t - tops[0]) for t in tops]
    denom = exps[0]
    for e in exps[1:]:
        denom = denom + e

    member = jnp.zeros((tm, LANES), F32)
    for hit in hits:
        member = jnp.where(hit, 1.0, member)
    before = (lax.broadcasted_iota(I32, (tm, tm), 1) < lax.broadcasted_iota(I32, (tm, tm), 0))
    ahead = _dot(before.astype(BF16), member.astype(BF16)) + run_ref[...]

    idx_out = jnp.zeros((tm, LANES), F32)
    gate_out = jnp.zeros((tm, LANES), F32)
    rank_out = jnp.zeros((tm, LANES), F32)
    for k in range(TOP_K):
        idx_k = jnp.sum(jnp.where(hits[k], lane_f, 0.0), axis=1, keepdims=True)
        rank_k = jnp.sum(jnp.where(hits[k], ahead, 0.0), axis=1, keepdims=True)
        idx_out = jnp.where(lane == k, idx_k, idx_out)
        gate_out = jnp.where(lane == k, exps[k] / denom, gate_out)
        rank_out = jnp.where(lane == k, rank_k, rank_out)
    idx_ref[...] = idx_out.astype(I32)
    gate_ref[...] = gate_out
    rank_ref[...] = rank_out.astype(I32)
    run_ref[...] = run_ref[...] + jnp.sum(member, axis=0, keepdims=True)
    cnt_ref[...] = run_ref[...].astype(I32)


def moe_router(r, g, router_w, router_b, tm=512):
    n, d = r.shape
    n_exp = router_w.shape[1]
    w_pad = jnp.pad(router_w.astype(F32), ((0, 0), (0, LANES - n_exp)))
    w_hi = w_pad.astype(BF16)
    w_lo = (w_pad - w_hi.astype(F32)).astype(BF16)
    w_split = jnp.concatenate([w_hi, w_lo, w_hi], axis=0)
    b_pad = jnp.pad(router_b.astype(F32).reshape(1, n_exp), ((0, 0), (0, LANES - n_exp)),
                    constant_values=-jnp.inf)
    tile = pl.BlockSpec((tm, LANES), lambda i: (i, 0))
    idx, gates, rank, counts = pl.pallas_call(
        _router_kernel,
        grid=(n // tm,),
        in_specs=[pl.BlockSpec((tm, d), lambda i: (i, 0)),
                  pl.BlockSpec((1, d), lambda i: (0, 0)),
                  pl.BlockSpec((3 * d, LANES), lambda i: (0, 0)),
                  pl.BlockSpec((1, LANES), lambda i: (0, 0))],
        out_specs=[tile, tile, tile, pl.BlockSpec((1, LANES), lambda i: (0, 0))],
        out_shape=[jax.ShapeDtypeStruct((n, LANES), I32), jax.ShapeDtypeStruct((n, LANES), F32),
                   jax.ShapeDtypeStruct((n, LANES), I32), jax.ShapeDtypeStruct((1, LANES), I32)],
        scratch_shapes=[pltpu.VMEM((1, LANES), F32)],
        compiler_params=_cparams(1, 32),
        name="moe_router",
    )(r, g.reshape(1, d), w_split, b_pad)
    return idx[:, :TOP_K], gates, rank[:, :TOP_K], counts[0, :n_exp]


def _expert_kernel(be_ref, tok0_ref, tok1_ref, tok_ahead_ref, r_hbm, g_ref, wg_ref, bg_ref, wu_ref, bu_ref,
                   wd_ref, bd_ref, ys_hbm, buf_ref, wg_bf, wu_bf, wd_bf, y_buf, sem, out_sem):
    b = pl.program_id(0)
    last = pl.num_programs(0) - 1
    out_slot = lax.rem(b, 2)
    cur = lax.rem(b, GATHER_BUFS)
    ahead = lax.rem(b + GATHER_BUFS - 1, GATHER_BUFS)
    tiles = MOE_ROWS // SUBLANES
    d_tiles = r_hbm.shape[1]

    def row_copy(tok, buf, j):
        return pltpu.make_async_copy(
            r_hbm.at[tok], buf_ref.at[buf, pl.ds(j * d_tiles, d_tiles)], sem.at[buf])

    def start_gather_loop(toks, buf):
        def start(t, c):
            for sub in range(SUBLANES):
                j = t * SUBLANES + sub
                row_copy(toks[0, 0, j], buf, j).start()
            return c
        lax.fori_loop(0, tiles, start, 0)

    def wait_gather(buf):
        def wait(t, c):
            for sub in range(SUBLANES):
                row_copy(0, buf, 0).wait()
            return c
        lax.fori_loop(0, tiles, wait, 0)

    def out_copy(slot, blk, c):
        return pltpu.make_async_copy(
            y_buf.at[slot, :, pl.ds(c * LANES, LANES)],
            ys_hbm.at[pl.ds(blk * MOE_ROWS, MOE_ROWS), c, :], out_sem.at[slot])

    def wait_out(slot):
        for c in range(d_tiles):
            out_copy(slot, 0, c).wait()

    @pl.when(b == 0)
    def _():
        start_gather_loop(tok0_ref, 0)
        start_gather_loop(tok1_ref, 1)

    @pl.when(b >= 2)
    def _():
        wait_out(out_slot)

    @pl.when(jnp.logical_or(b == 0, be_ref[b] != be_ref[jnp.maximum(b - 1, 0)]))
    def _():
        wg_bf[...] = wg_ref[0].astype(BF16)
        wu_bf[...] = wu_ref[0].astype(BF16)
        wd_bf[...] = wd_ref[0].astype(BF16)

    wait_gather(cur)

    h = _rms(_load_rows_from_tiles(buf_ref.at[cur], MOE_ROWS, d_tiles), g_ref[...]).astype(BF16)
    gate = jnp.minimum(_dot(h, wg_bf[...]) + bg_ref[0], SWIGLU_LIMIT)
    up = jnp.clip(_dot(h, wu_bf[...]) + bu_ref[0], -SWIGLU_LIMIT, SWIGLU_LIMIT)
    act = (gate * jax.nn.sigmoid(SWIGLU_ALPHA * gate) * (up + 1.0)).astype(BF16)
    for j in range(MOE_ROWS):
        row_copy(tok_ahead_ref[0, 0, j], ahead, j).start(priority=j % DMA_QUEUES)
    y_buf[out_slot] = _dot(act, wd_bf[...]) + bd_ref[0]
    for c in range(d_tiles):
        out_copy(out_slot, b, c).start()

    @pl.when(b == last)
    def _():
        for back in range(1, GATHER_BUFS):
            wait_gather(lax.rem(b + back, GATHER_BUFS))
        wait_out(1 - out_slot)
        wait_out(out_slot)


def moe_experts(r_tiles, g, slot_tok, block_expert, layer, w_gate, b_gate, w_up, b_up, w_down, b_down):
    n, d_tiles, _ = r_tiles.shape
    d = d_tiles * LANES
    n_layers, n_exp, _, d_ff = w_gate.shape
    n_blocks = block_expert.shape[0]
    assert n_blocks >= GATHER_BUFS
    toks = slot_tok.reshape(n_blocks, 1, MOE_ROWS)

    def expert4(i, be):
        return (layer, be[i], 0, 0)

    def tok_block(index):
        return pl.BlockSpec((1, 1, MOE_ROWS), index, memory_space=pltpu.SMEM)

    grid_spec = pltpu.PrefetchScalarGridSpec(
        num_scalar_prefetch=1,
        grid=(n_blocks,),
        in_specs=[tok_block(lambda i, be: (0, 0, 0)),
                  tok_block(lambda i, be: (1, 0, 0)),
                  tok_block(lambda i, be: (jnp.minimum(i + GATHER_BUFS - 1, n_blocks - 1), 0, 0)),
                  pl.BlockSpec(memory_space=pl.ANY),
                  pl.BlockSpec((1, d), lambda i, be: (0, 0)),
                  pl.BlockSpec((None, 1, d, d_ff), expert4), pl.BlockSpec((None, 1, 1, d_ff), expert4),
                  pl.BlockSpec((None, 1, d, d_ff), expert4), pl.BlockSpec((None, 1, 1, d_ff), expert4),
                  pl.BlockSpec((None, 1, d_ff, d), expert4), pl.BlockSpec((None, 1, 1, d), expert4)],
        out_specs=pl.BlockSpec(memory_space=pl.ANY),
        scratch_shapes=[pltpu.VMEM((GATHER_BUFS, MOE_ROWS * d_tiles, LANES), F32),
                        pltpu.VMEM((d, d_ff), BF16), pltpu.VMEM((d, d_ff), BF16), pltpu.VMEM((d_ff, d), BF16),
                        pltpu.VMEM((2, MOE_ROWS, d), F32),
                        pltpu.SemaphoreType.DMA((GATHER_BUFS,)), pltpu.SemaphoreType.DMA((2,))])
    return pl.pallas_call(
        _expert_kernel,
        grid_spec=grid_spec,
        out_shape=jax.ShapeDtypeStruct((n_blocks * MOE_ROWS, d_tiles, LANES), F32),
        compiler_params=_cparams(1, 56),
        name="moe_experts",
    )(block_expert, toks, toks, toks, r_tiles, g.reshape(1, d),
      w_gate, b_gate.reshape(n_layers, n_exp, 1, d_ff), w_up, b_up.reshape(n_layers, n_exp, 1, d_ff),
      w_down, b_down.reshape(n_layers, n_exp, 1, d))


def _combine_kernel(dest0_ref, dest1_ref, dest_ahead_ref, ys_hbm, gate_ref, r_ref, p_ref, gp_ref, wp_ref,
                    wpg_ref, gf_ref, o_ref, buf_ref, sem, *, tile, final_norm):
    i = pl.program_id(0)
    last = pl.num_programs(0) - 1
    cur = lax.rem(i, GATHER_BUFS)
    ahead = lax.rem(i + GATHER_BUFS - 1, GATHER_BUFS)
    tiles = tile // SUBLANES
    d_tiles = ys_hbm.shape[1]

    def row_copy(slot, buf, k, j):
        return pltpu.make_async_copy(
            ys_hbm.at[slot], buf_ref.at[buf, k, pl.ds(j * d_tiles, d_tiles)], sem.at[buf])

    def start_gather_loop(dests, buf):
        def start(t, c):
            for sub in range(SUBLANES):
                for k in range(TOP_K):
                    j = t * SUBLANES + sub
                    row_copy(dests[0, 0, j * TOP_K + k], buf, k, j).start()
            return c
        lax.fori_loop(0, tiles, start, 0)

    def wait_gather(buf):
        def wait(t, c):
            for sub in range(SUBLANES):
                for k in range(TOP_K):
                    row_copy(0, buf, k, 0).wait()
            return c
        lax.fori_loop(0, tiles, wait, 0)

    @pl.when(i == 0)
    def _():
        start_gather_loop(dest0_ref, 0)
        start_gather_loop(dest1_ref, 1)

    wait_gather(cur)
    gates = gate_ref[...]
    r = r_ref[...]
    for k in range(TOP_K):
        r = r + _load_rows_from_tiles(buf_ref.at[cur, k], tile, d_tiles) * gates[:, k:k + 1]
    h = _rms(r, gp_ref[...]).astype(BF16)
    gate = jax.nn.sigmoid(_dot(h, wpg_ref[...]))
    r = r + _dot(p_ref[...].astype(BF16), wp_ref[...]) * gate
    if final_norm:
        r = _rms(r, gf_ref[...])
    o_ref[...] = r
    for j in range(tile):
        for k in range(TOP_K):
            row_copy(dest_ahead_ref[0, 0, j * TOP_K + k], ahead, k, j).start(priority=k % DMA_QUEUES)

    @pl.when(i == last)
    def _():
        for back in range(1, GATHER_BUFS):
            wait_gather(lax.rem(i + back, GATHER_BUFS))


def moe_combine_embed(ys, dest, gates, r, p, g_ple, w_ple, w_ple_gate, g_final, final_norm, tile=256):
    n, d = r.shape
    pd = p.shape[1]
    n_tiles = n // tile
    assert n_tiles >= GATHER_BUFS
    dest_tiles = dest.reshape(n_tiles, 1, tile * TOP_K)

    def dest_block(index):
        return pl.BlockSpec((1, 1, tile * TOP_K), index, memory_space=pltpu.SMEM)

    return pl.pallas_call(
        functools.partial(_combine_kernel, tile=tile, final_norm=final_norm),
        grid=(n_tiles,),
        in_specs=[dest_block(lambda i: (0, 0, 0)),
                  dest_block(lambda i: (1, 0, 0)),
                  dest_block(lambda i: (jnp.minimum(i + GATHER_BUFS - 1, n_tiles - 1), 0, 0)),
                  pl.BlockSpec(memory_space=pl.ANY),
                  pl.BlockSpec((tile, LANES), lambda i: (i, 0)),
                  pl.BlockSpec((tile, d), lambda i: (i, 0)),
                  pl.BlockSpec((tile, pd), lambda i: (i, 0)),
                  pl.BlockSpec((1, d), lambda i: (0, 0)),
                  pl.BlockSpec((pd, d), lambda i: (0, 0)),
                  pl.BlockSpec((d, d), lambda i: (0, 0)),
                  pl.BlockSpec((1, d), lambda i: (0, 0))],
        out_specs=pl.BlockSpec((tile, d), lambda i: (i, 0)),
        out_shape=jax.ShapeDtypeStruct((n, d), F32),
        scratch_shapes=[pltpu.VMEM((GATHER_BUFS, TOP_K, tile * (d // LANES), LANES), F32),
                        pltpu.SemaphoreType.DMA((GATHER_BUFS,))],
        compiler_params=_cparams(1, 48),
        name="moe_combine_embed",
    )(dest_tiles, dest_tiles, dest_tiles, ys, gates, r, p, g_ple.reshape(1, d), w_ple, w_ple_gate,
      g_final.reshape(1, d))


def _slot_layout(idx, rank, counts, n_blocks):
    n_exp = counts.shape[0]
    padded = (counts + MOE_ROWS - 1) // MOE_ROWS * MOE_ROWS
    pad_end = jnp.cumsum(padded)
    pad_start = pad_end - padded
    cnt_start = jnp.cumsum(counts) - counts
    start_of = jnp.sum(jnp.where(idx[..., None] == jnp.arange(n_exp), pad_start, 0), axis=-1)
    dest = (start_of + rank).astype(I32).reshape(-1)
    block_first_row = jnp.arange(n_blocks) * MOE_ROWS
    block_expert = jnp.minimum(
        jnp.sum(pad_end[None, :] <= block_first_row[:, None], axis=1), n_exp - 1).astype(I32)
    sorted_tok = (jnp.argsort(dest) // TOP_K).astype(I32)
    in_expert = block_first_row[:, None] + jnp.arange(MOE_ROWS)[None, :] - pad_start[block_expert][:, None]
    real = in_expert < counts[block_expert][:, None]
    compact = jnp.clip(in_expert + cnt_start[block_expert][:, None], 0, dest.shape[0] - 1)
    slot_tok = jnp.where(real, sorted_tok[compact], 0).astype(I32)
    return dest, slot_tok, block_expert


def moe_and_embed(r, r_tiles, p, g_ffn, router_w, router_b, layer, w_gate, b_gate, w_up, b_up, w_down, b_down,
                  g_ple, w_ple, w_ple_gate, g_final, final_norm):
    n, _ = r.shape
    n_blocks = n * TOP_K // MOE_ROWS + router_w.shape[1]
    idx, gates, rank, counts = moe_router(r, g_ffn, router_w, router_b)
    dest, slot_tok, block_expert = _slot_layout(idx, rank, counts, n_blocks)
    ys = moe_experts(r_tiles, g_ffn, slot_tok, block_expert, layer, w_gate.astype(F32), b_gate.astype(F32),
                     w_up.astype(F32), b_up.astype(F32), w_down.astype(F32), b_down.astype(F32))
    return moe_combine_embed(ys, dest, gates, r, p, g_ple, w_ple.astype(BF16), w_ple_gate.astype(BF16),
                             g_final, final_norm)


def kernel(x, p, norm_mix, norm_ffn, norm_ple, norm_final, w_in_ab, ssm_a_re, ssm_a_im, ssm_b_re, ssm_b_im, ssm_c_re, ssm_c_im, ssm_d, ssm_log_step, glu_w, glu_b, w_out_ab, rel_bias, w_in_c, w_out_c, router_w, router_b, w_gate, b_gate, w_up, b_up, w_down, b_down, w_ple, w_ple_gate):
    bsz, seq, d = x.shape
    n = bsz * seq
    depth = p.shape[0]
    assert bsz == SUBLANES
    r = x.reshape(n, d).astype(F32)
    for i in range(depth):
        j = i // 2
        if i % 2 == 0:
            ssm_w = ssm_d.shape[1] * ssm_d.shape[2]
            moba_w = (w_in_ab.shape[2] - ssm_w) // 3
            u, qkv = norm_matmul(r, norm_mix[i], w_in_ab[j].astype(BF16),
                                 (ssm_w, 3 * moba_w), (F32, BF16))
            a_mat, b_mat, c_mat = _s5_params(ssm_a_re[j], ssm_a_im[j], ssm_b_re[j], ssm_b_im[j],
                                             ssm_c_re[j], ssm_c_im[j], ssm_log_step[j])
            u_tb = u.reshape(bsz, seq, ssm_w).transpose(1, 0, 2).reshape(n, ssm_w)
            y_a = s5_mixer(u_tb, a_mat, b_mat, c_mat, ssm_d[j].reshape(1, ssm_w).astype(F32),
                           glu_w[j].astype(BF16), glu_b[j].reshape(1, ssm_w).astype(F32))
            y_a = y_a.reshape(seq, bsz, ssm_w).transpose(1, 0, 2).reshape(n, ssm_w)
            y_b = moba_mixer(qkv.reshape(bsz, seq, 3 * moba_w), rel_bias, moba_w).reshape(n, moba_w)
            w_out = w_out_ab[j].astype(BF16)
            r, r_tiles = matmul_residual(r, [y_a, y_b], [w_out[:ssm_w], w_out[ssm_w:]])
        else:
            q_scale = jnp.where(jnp.arange(3 * d) < d, HEAD_DIM ** -0.5 * math.log2(math.e), 1.0)
            w_in = (w_in_c[j].astype(F32) * q_scale).astype(BF16)
            (qkv,) = norm_matmul(r, norm_mix[i], w_in, (3 * d,), (BF16,))
            y_c = stick_breaking_mixer(qkv.reshape(bsz, seq, 3 * d), d).reshape(n, d)
            r, r_tiles = matmul_residual(r, [y_c], [w_out_c[j].astype(BF16)])
        r = moe_and_embed(r, r_tiles, p[i].reshape(n, -1), norm_ffn[i], router_w[i], router_b[i],
                          i, w_gate, b_gate, w_up, b_up, w_down, b_down,
                          norm_ple[i], w_ple[i], w_ple_gate[i], norm_final, i == depth - 1)
    return r.reshape(bsz, seq, d).astype(x.dtype)
```

```python
import functools
import math

import jax
import jax.numpy as jnp
from jax import lax
from jax.experimental import pallas as pl
from jax.experimental.pallas import tpu as pltpu

F32 = jnp.float32
BF16 = jnp.bfloat16
I32 = jnp.int32

RMS_EPS = 1e-6
HEAD_DIM = 64
LANES = 128
SUBLANES = 8
SSM_GROUP = 16
SSM_STATE = 64
SSM_SLAB_GROUPS = LANES // SSM_GROUP
MOBA_BLOCK = 256
MOBA_TOPK = 3
MOBA_HEADS = 4
REL_BUCKETS = 32
REL_MAX_DIST = 128
SB_BLOCK = 256
SB_HEADS = 4
N_EXPERTS = 32
TOP_K = 4
SWIGLU_LIMIT = 7.0
SWIGLU_ALPHA = 1.702
MOE_ROWS = 512
GATHER_BUFS = 3
DMA_QUEUES = 2
NEG = -1e30
MIB = 1024 * 1024

_NT = (((1,), (1,)), ((), ()))


def _cparams(n_axes, vmem_mib):
    return pltpu.CompilerParams(
        dimension_semantics=("arbitrary",) * n_axes, vmem_limit_bytes=vmem_mib * MIB)


def _rms(x, g):
    ms = jnp.mean(x * x, axis=-1, keepdims=True)
    return x * lax.rsqrt(ms + RMS_EPS) * g


def _dot(a, b):
    return jnp.dot(a, b, preferred_element_type=F32)


def _dot_nt(a, b):
    return lax.dot_general(a, b, _NT, preferred_element_type=F32)


def _norm_matmul_kernel(x_ref, g_ref, w_ref, *o_refs, splits):
    h = _rms(x_ref[...], g_ref[...]).astype(BF16)
    y = _dot(h, w_ref[...])
    off = 0
    for o_ref, s in zip(o_refs, splits):
        o_ref[...] = y[:, off:off + s].astype(o_ref.dtype)
        off += s


def norm_matmul(x, g, w, splits, dtypes, tm=512):
    n, d = x.shape
    nout = w.shape[1]
    return pl.pallas_call(
        functools.partial(_norm_matmul_kernel, splits=splits),
        grid=(n // tm,),
        in_specs=[pl.BlockSpec((tm, d), lambda i: (i, 0)),
                  pl.BlockSpec((1, d), lambda i: (0, 0)),
                  pl.BlockSpec((d, nout), lambda i: (0, 0))],
        out_specs=[pl.BlockSpec((tm, s), lambda i: (i, 0)) for s in splits],
        out_shape=[jax.ShapeDtypeStruct((n, s), dt) for s, dt in zip(splits, dtypes)],
        compiler_params=_cparams(1, 48),
        name="norm_matmul",
    )(x, g.reshape(1, d), w)


def _store_rows_as_tiles(o3_ref, x):
    for c in range(o3_ref.shape[1]):
        o3_ref[:, c, :] = x[:, c * LANES:(c + 1) * LANES]


def _load_rows_from_tiles(x2_ref, rows, d_tiles):
    return jnp.concatenate([x2_ref[pl.ds(c, rows, stride=d_tiles), :] for c in range(d_tiles)], axis=1)


def _matmul_residual_kernel(r_ref, *refs, n_in):
    acc = r_ref[...]
    for a_ref, w_ref in zip(refs[:n_in], refs[n_in:2 * n_in]):
        acc = acc + _dot(a_ref[...], w_ref[...])
    refs[2 * n_in][...] = acc
    _store_rows_as_tiles(refs[2 * n_in + 1], acc)


def matmul_residual(r, a_list, w_list, tm=512):
    n, d = r.shape
    n_in = len(a_list)
    in_specs = [pl.BlockSpec((tm, d), lambda i: (i, 0))]
    in_specs += [pl.BlockSpec((tm, a.shape[1]), lambda i: (i, 0)) for a in a_list]
    in_specs += [pl.BlockSpec(w.shape, lambda i: (0, 0)) for w in w_list]
    return pl.pallas_call(
        functools.partial(_matmul_residual_kernel, n_in=n_in),
        grid=(n // tm,),
        in_specs=in_specs,
        out_specs=[pl.BlockSpec((tm, d), lambda i: (i, 0)),
                   pl.BlockSpec((tm, d // LANES, LANES), lambda i: (i, 0, 0))],
        out_shape=[jax.ShapeDtypeStruct((n, d), F32), jax.ShapeDtypeStruct((n, d // LANES, LANES), F32)],
        compiler_params=_cparams(1, 40),
        name="matmul_residual",
    )(r, *a_list, *w_list)


def _s5_kernel(u_ref, a_ref, b_ref, c_ref, d_ref, gw_ref, gb_ref, o_ref, s_ref, x_ref,
               *, steps, n_slab):
    half = SSM_SLAB_GROUPS * SSM_STATE
    slab = 2 * half

    @pl.when(pl.program_id(0) == 0)
    def _():
        x_ref[...] = jnp.zeros_like(x_ref)

    u = u_ref[...]
    ub = u.astype(BF16)
    for s in range(n_slab):
        s_ref[:, s * slab:(s + 1) * slab] = _dot(ub[:, s * LANES:(s + 1) * LANES], b_ref[s])

    for s in range(n_slab):
        re = slice(s * slab, s * slab + half)
        im = slice(s * slab + half, (s + 1) * slab)
        ar = a_ref[:, re]
        ai = a_ref[:, im]

        def step(t, carry, re=re, im=im, ar=ar, ai=ai):
            xr, xi = carry
            rows = pl.ds(pl.multiple_of(t * SUBLANES, SUBLANES), SUBLANES)
            nr = ar * xr - ai * xi + s_ref[rows, re]
            ni = ar * xi + ai * xr + s_ref[rows, im]
            s_ref[rows, re] = nr
            s_ref[rows, im] = ni
            return nr, ni

        xr, xi = lax.fori_loop(0, steps, step, (x_ref[:, re], x_ref[:, im]), unroll=8)
        x_ref[:, re] = xr
        x_ref[:, im] = xi

    y = jnp.concatenate(
        [_dot(s_ref[:, s * slab:(s + 1) * slab].astype(BF16), c_ref[s]) for s in range(n_slab)],
        axis=1)
    y = y + d_ref[...] * u
    z = jax.nn.gelu(y)
    gate = jax.nn.sigmoid(_dot(z.astype(BF16), gw_ref[...]) + gb_ref[...])
    o_ref[...] = (z * gate).astype(o_ref.dtype)


def s5_mixer(u_tb, a_mat, b_mat, c_mat, d_vec, glu_w, glu_b, steps=64):
    rows, width = u_tb.shape
    n_slab = width // LANES
    n_state = 2 * n_slab * SSM_SLAB_GROUPS * SSM_STATE
    tm = steps * SUBLANES
    return pl.pallas_call(
        functools.partial(_s5_kernel, steps=steps, n_slab=n_slab),
        grid=(rows // tm,),
        in_specs=[pl.BlockSpec((tm, width), lambda i: (i, 0)),
                  pl.BlockSpec(a_mat.shape, lambda i: (0, 0)),
                  pl.BlockSpec(b_mat.shape, lambda i: (0, 0, 0)),
                  pl.BlockSpec(c_mat.shape, lambda i: (0, 0, 0)),
                  pl.BlockSpec((1, width), lambda i: (0, 0)),
                  pl.BlockSpec(glu_w.shape, lambda i: (0, 0)),
                  pl.BlockSpec((1, width), lambda i: (0, 0))],
        out_specs=pl.BlockSpec((tm, width), lambda i: (i, 0)),
        out_shape=jax.ShapeDtypeStruct((rows, width), BF16),
        scratch_shapes=[pltpu.VMEM((tm, n_state), F32), pltpu.VMEM((SUBLANES, n_state), F32)],
        compiler_params=_cparams(1, 40),
        name="s5_mixer",
    )(u_tb, a_mat, b_mat, c_mat, d_vec, glu_w, glu_b)


def _s5_params(a_re, a_im, b_re, b_im, c_re, c_im, log_step):
    n_grp = a_re.shape[0]
    n_slab = n_grp // SSM_SLAB_GROUPS
    lam = lax.complex(a_re.astype(F32), a_im.astype(F32))
    step = jnp.exp(log_step.astype(F32))[:, None]
    lam_bar = jnp.exp(lam * step)
    b_bar = ((lam_bar - 1.0) / lam)[:, :, None] * lax.complex(b_re.astype(F32), b_im.astype(F32))
    eye = jnp.eye(SSM_SLAB_GROUPS, dtype=F32)

    def slabbed(t):
        return t.reshape((n_slab, SSM_SLAB_GROUPS) + t.shape[1:])

    a_mat = jnp.concatenate(
        [slabbed(jnp.real(lam_bar)).reshape(n_slab, -1), slabbed(jnp.imag(lam_bar)).reshape(n_slab, -1)],
        axis=1).reshape(1, -1)
    a_mat = jnp.broadcast_to(a_mat, (SUBLANES, a_mat.shape[1]))
    b_parts = [jnp.einsum('sgnh,gk->sghkn', slabbed(part(b_bar)), eye)
               for part in (jnp.real, jnp.imag)]
    b_mat = jnp.stack(b_parts, axis=3).reshape(n_slab, LANES, -1)
    c_parts = [jnp.einsum('sghn,gk->sgnkh', slabbed(part), eye)
               for part in (c_re.astype(F32), -c_im.astype(F32))]
    c_mat = jnp.stack(c_parts, axis=1).reshape(n_slab, -1, LANES)
    return a_mat, b_mat.astype(BF16), c_mat.astype(BF16)


def _by_head(lane, cols):
    out = cols[-1]
    for h in reversed(range(len(cols) - 1)):
        out = jnp.where(lane // HEAD_DIM == h, cols[h], out)
    return out


def _moba_kernel(q_ref, k_ref, v_ref, tb_ref, o_ref, mean_ref, km_ref, *, n_blk):
    blk = MOBA_BLOCK
    heads = MOBA_HEADS
    width = heads * HEAD_DIM
    rows_all = heads * blk
    qb = pl.program_id(2)

    @pl.when(qb == 0)
    def _():
        mean_ref[...] = jnp.zeros_like(mean_ref)
        for n in range(n_blk):
            kb = k_ref[0, n * blk:(n + 1) * blk, :].astype(F32)
            mean_ref[n:n + 1, :] = jnp.sum(kb, axis=0, keepdims=True) / blk
        mean = mean_ref[...]
        hi = mean.astype(BF16)
        km_ref[:, :width] = hi
        km_ref[:, width:] = (mean - hi.astype(F32)).astype(BF16)

    q = q_ref[0]
    lane = lax.broadcasted_iota(I32, (blk, width), 1)
    q_stack = jnp.concatenate(
        [jnp.where(lane // HEAD_DIM == h, q, jnp.zeros_like(q)) for h in range(heads)], axis=0)

    blane = lax.broadcasted_iota(I32, (rows_all, LANES), 1)
    gate = _dot_nt(jnp.concatenate([q_stack, q_stack], axis=1), km_ref[...])
    gate = jnp.where(blane < qb, gate, -jnp.inf)
    sel = jnp.zeros((rows_all, LANES), F32)
    blane_f = blane.astype(F32)
    for r in range(MOBA_TOPK):
        top = jnp.max(gate, axis=1, keepdims=True)
        idx = jnp.min(jnp.where(gate == top, blane_f, float(LANES)), axis=1, keepdims=True)
        hit = blane_f == idx
        sel = jnp.where(jnp.logical_and(hit, r < qb), 1.0, sel)
        gate = jnp.where(hit, -jnp.inf, gate)

    qs = q_stack * (HEAD_DIM ** -0.5)
    row = lax.broadcasted_iota(I32, (rows_all, blk), 0)
    col = lax.broadcasted_iota(I32, (rows_all, blk), 1)

    def rows_of(n):
        return pl.ds(pl.multiple_of(n * blk, blk), blk)

    def weighted_values(p, n):
        vn = v_ref[0, rows_of(n), :]
        p = p.astype(BF16)
        p_cat = jnp.concatenate([p[h * blk:(h + 1) * blk] for h in range(heads)], axis=1)
        v_stack = jnp.concatenate(
            [jnp.where(lane // HEAD_DIM == h, vn, jnp.zeros_like(vn)) for h in range(heads)], axis=0)
        return _dot(p_cat, v_stack)

    def per_head(x):
        return _by_head(lane, [x[h * blk:(h + 1) * blk] for h in range(heads)])

    s = _dot_nt(qs, k_ref[0, rows_of(qb), :]) + tb_ref[0, 0]
    s = jnp.where(col <= (row & (blk - 1)), s, NEG)
    m = jnp.max(s, axis=1, keepdims=True)
    p = jnp.exp(s - m)
    l = jnp.sum(p, axis=1, keepdims=True)
    acc = weighted_values(p, qb)

    def past_blocks(first, per_trip):
        def body(i, carry):
            m, l, acc = carry
            for u in range(per_trip):
                n = first + per_trip * i + u
                chosen = jnp.sum(jnp.where(blane == n, sel, 0.0), axis=1, keepdims=True) > 0.0
                s = _dot_nt(qs, k_ref[0, rows_of(n), :]) + tb_ref[0, jnp.minimum(qb - n, 2)]
                s = jnp.where(chosen, s, NEG)
                m_new = jnp.maximum(m, jnp.max(s, axis=1, keepdims=True))
                alpha = jnp.exp(m - m_new)
                p = jnp.exp(s - m_new)
                l = alpha * l + jnp.sum(p, axis=1, keepdims=True)
                acc = per_head(alpha) * acc + weighted_values(p, n)
                m = m_new
            return m, l, acc
        return body

    odd = qb & 1
    carry = lax.fori_loop(0, odd, past_blocks(0, 1), (m, l, acc))
    _, l, acc = lax.fori_loop(0, qb // 2, past_blocks(odd, 2), carry)
    o_ref[0] = (acc / per_head(l)).astype(o_ref.dtype)


def _rel_bucket(dist):
    exact = REL_BUCKETS // 2
    n = jnp.maximum(dist, 0)
    nf = jnp.maximum(n, 1).astype(F32)
    log_ratio = jnp.log(nf / exact) / math.log(REL_MAX_DIST / exact)
    large = exact + (log_ratio * (REL_BUCKETS - exact)).astype(I32)
    large = jnp.minimum(large, REL_BUCKETS - 1)
    return jnp.where(n < exact, n, large)


def _moba_bias_tables(rel_bias):
    assert REL_MAX_DIST <= MOBA_BLOCK + 1
    i = jnp.arange(MOBA_BLOCK)[:, None]
    j = jnp.arange(MOBA_BLOCK)[None, :]
    dist = jnp.arange(3)[:, None, None] * MOBA_BLOCK + (i - j)[None]
    onehot = (_rel_bucket(dist)[..., None] == jnp.arange(REL_BUCKETS)).astype(F32)
    return jnp.einsum('oijb,bh->hoij', onehot, rel_bias.astype(F32), precision=lax.Precision.HIGHEST)


def moba_mixer(qkv, rel_bias, width):
    bsz, seq, _ = qkv.shape
    blk = MOBA_BLOCK
    n_blk = seq // blk
    gw = MOBA_HEADS * HEAD_DIM
    n_grp = width // gw
    assert seq % blk == 0 and n_blk <= LANES
    tables = _moba_bias_tables(rel_bias).reshape(n_grp, MOBA_HEADS, 3, blk, blk)
    tables = tables.transpose(0, 2, 1, 3, 4).reshape(n_grp, 3, MOBA_HEADS * blk, blk)
    return pl.pallas_call(
        functools.partial(_moba_kernel, n_blk=n_blk),
        grid=(bsz, n_grp, n_blk),
        in_specs=[pl.BlockSpec((1, blk, gw), lambda b, p, i: (b, i, p)),
                  pl.BlockSpec((1, seq, gw), lambda b, p, i: (b, 0, n_grp + p)),
                  pl.BlockSpec((1, seq, gw), lambda b, p, i: (b, 0, 2 * n_grp + p)),
                  pl.BlockSpec((1, 3, MOBA_HEADS * blk, blk), lambda b, p, i: (p, 0, 0, 0))],
        out_specs=pl.BlockSpec((1, blk, gw), lambda b, p, i: (b, i, p)),
        out_shape=jax.ShapeDtypeStruct((bsz, seq, width), BF16),
        scratch_shapes=[pltpu.VMEM((LANES, gw), F32), pltpu.VMEM((LANES, 2 * gw), BF16)],
        compiler_params=_cparams(3, 48),
        name="moba_mixer",
    )(qkv, qkv, qkv, tables)


def _sb_kernel(q_ref, k_ref, v_ref, tri_ref, o_ref):
    blk = SB_BLOCK
    width = SB_HEADS * HEAD_DIM
    n_sub = blk // LANES
    rows_all = SB_HEADS * blk
    qb = pl.program_id(2)
    q = q_ref[0]
    lane = lax.broadcasted_iota(I32, (blk, width), 1)
    q_stack = jnp.concatenate(
        [jnp.where(lane // HEAD_DIM == h, q, jnp.zeros_like(q)) for h in range(SB_HEADS)],
        axis=0)
    row = lax.broadcasted_iota(I32, (rows_all, blk), 0)
    col = lax.broadcasted_iota(I32, (rows_all, blk), 1)
    past = col < (row & (blk - 1))

    def rows_of(n):
        return pl.ds(pl.multiple_of(n * blk, blk), blk)

    def logits(n):
        return _dot_nt(q_stack, k_ref[0, rows_of(n), :])

    def weights(z, carried, diagonal):
        neg_abs = lax.bitcast_convert_type(
            lax.bitcast_convert_type(z, jnp.uint32) | jnp.uint32(0x80000000), F32)
        drop = jnp.maximum(z, 0.0) + jnp.log2(1.0 + jnp.exp2(neg_abs))
        if diagonal:
            drop = jnp.where(past, drop, 0.0)
        hi32 = lax.bitcast_convert_type(
            lax.bitcast_convert_type(drop, jnp.uint32) & jnp.uint32(0xFFFF0000), F32)
        hi = hi32.astype(BF16)
        lo = (drop - hi32).astype(BF16)
        lhs = jnp.concatenate(
            [jnp.concatenate([hi[:, c * LANES:(c + 1) * LANES], lo[:, c * LANES:(c + 1) * LANES]], axis=1)
             for c in range(n_sub)], axis=0)
        sums = _dot(lhs, tri_ref[...])
        newer = carried
        from_key = [None] * n_sub
        for c in reversed(range(n_sub)):
            within = sums[c * rows_all:(c + 1) * rows_all, :LANES]
            total = sums[c * rows_all:(c + 1) * rows_all, LANES:]
            from_key[c] = within if newer is None else within + newer
            newer = total if newer is None else newer + total
        w = jnp.exp2(z - jnp.concatenate(from_key, axis=1))
        if diagonal:
            w = jnp.where(past, w, 0.0)
        w = w.astype(BF16)
        return jnp.concatenate([w[h * blk:(h + 1) * blk] for h in range(SB_HEADS)], axis=1), newer

    def weighted_values(w_cat, n):
        vn = v_ref[0, rows_of(n), :]
        v_stack = jnp.concatenate(
            [jnp.where(lane // HEAD_DIM == h, vn, jnp.zeros_like(vn)) for h in range(SB_HEADS)], axis=0)
        return _dot(w_cat, v_stack)

    w_cat, carried = weights(logits(qb), None, True)
    acc = weighted_values(w_cat, qb)

    def older_blocks(first, per_trip):
        def body(i, carry):
            acc, carried = carry
            for u in range(per_trip):
                n = first - per_trip * i - u
                w_cat, carried = weights(logits(n), carried, False)
                acc = acc + weighted_values(w_cat, n)
            return acc, carried
        return body

    odd = qb & 1
    acc, carried = lax.fori_loop(0, odd, older_blocks(qb - 1, 1), (acc, carried))
    acc, _ = lax.fori_loop(0, qb // 2, older_blocks(qb - 1 - odd, 2), (acc, carried))
    o_ref[0] = acc.astype(o_ref.dtype)


def stick_breaking_mixer(qkv, width):
    bsz, seq, _ = qkv.shape
    blk = SB_BLOCK
    gw = SB_HEADS * HEAD_DIM
    n_grp = width // gw
    tri = (jnp.arange(LANES)[:, None] >= jnp.arange(LANES)[None, :]).astype(BF16)
    tri = jnp.concatenate([tri, jnp.ones((LANES, LANES), BF16)], axis=1)
    tri = jnp.concatenate([tri, tri], axis=0)
    return pl.pallas_call(
        _sb_kernel,
        grid=(bsz, n_grp, seq // blk),
        in_specs=[pl.BlockSpec((1, blk, gw), lambda b, p, i: (b, i, p)),
                  pl.BlockSpec((1, seq, gw), lambda b, p, i: (b, 0, n_grp + p)),
                  pl.BlockSpec((1, seq, gw), lambda b, p, i: (b, 0, 2 * n_grp + p)),
                  pl.BlockSpec(tri.shape, lambda b, p, i: (0, 0))],
        out_specs=pl.BlockSpec((1, blk, gw), lambda b, p, i: (b, i, p)),
        out_shape=jax.ShapeDtypeStruct((bsz, seq, width), BF16),
        compiler_params=_cparams(3, 48),
        name="stick_breaking",
    )(qkv, qkv, qkv, tri)


def _router_kernel(r_ref, g_ref, w_ref, b_ref, idx_ref, gate_ref, rank_ref, cnt_ref, run_ref):
    tm = r_ref.shape[0]

    @pl.when(pl.program_id(0) == 0)
    def _():
        run_ref[...] = jnp.zeros_like(run_ref)

    h = _rms(r_ref[...], g_ref[...])
    h_hi = h.astype(BF16)
    h_lo = (h - h_hi.astype(F32)).astype(BF16)
    logits = _dot(jnp.concatenate([h_hi, h_hi, h_lo], axis=1), w_ref[...]) + b_ref[...]
    lane = lax.broadcasted_iota(I32, (tm, LANES), 1)
    lane_f = lane.astype(F32)
    tops, hits = [], []
    for _ in range(TOP_K):
        top = jnp.max(logits, axis=1, keepdims=True)
        idx = jnp.min(jnp.where(logits == top, lane_f, float(LANES)), axis=1, keepdims=True)
        hit = lane_f == idx
        logits = jnp.where(hit, -jnp.inf, logits)
        tops.append(top)
        hits.append(hit)
    exps = [jnp.exp(t - tops[0]) for t in tops]
    denom = exps[0]
    for e in exps[1:]:
        denom = denom + e

    member = jnp.zeros((tm, LANES), F32)
    for hit in hits:
        member = jnp.where(hit, 1.0, member)
    before = (lax.broadcasted_iota(I32, (tm, tm), 1) < lax.broadcasted_iota(I32, (tm, tm), 0))
    ahead = _dot(before.astype(BF16), member.astype(BF16)) + run_ref[...]

    idx_out = jnp.zeros((tm, LANES), F32)
    gate_out = jnp.zeros((tm, LANES), F32)
    rank_out = jnp.zeros((tm, LANES), F32)
    for k in range(TOP_K):
        idx_k = jnp.sum(jnp.where(hits[k], lane_f, 0.0), axis=1, keepdims=True)
        rank_k = jnp.sum(jnp.where(hits[k], ahead, 0.0), axis=1, keepdims=True)
        idx_out = jnp.where(lane == k, idx_k, idx_out)
        gate_out = jnp.where(lane == k, exps[k] / denom, gate_out)
        rank_out = jnp.where(lane == k, rank_k, rank_out)
    idx_ref[...] = idx_out.astype(I32)
    gate_ref[...] = gate_out
    rank_ref[...] = rank_out.astype(I32)
    run_ref[...] = run_ref[...] + jnp.sum(member, axis=0, keepdims=True)
    cnt_ref[...] = run_ref[...].astype(I32)


def moe_router(r, g, router_w, router_b, tm=512):
    n, d = r.shape
    n_exp = router_w.shape[1]
    w_pad = jnp.pad(router_w.astype(F32), ((0, 0), (0, LANES - n_exp)))
    w_hi = w_pad.astype(BF16)
    w_lo = (w_pad - w_hi.astype(F32)).astype(BF16)
    w_split = jnp.concatenate([w_hi, w_lo, w_hi], axis=0)
    b_pad = jnp.pad(router_b.astype(F32).reshape(1, n_exp), ((0, 0), (0, LANES - n_exp)),
                    constant_values=-jnp.inf)
    tile = pl.BlockSpec((tm, LANES), lambda i: (i, 0))
    idx, gates, rank, counts = pl.pallas_call(
        _router_kernel,
        grid=(n // tm,),
        in_specs=[pl.BlockSpec((tm, d), lambda i: (i, 0)),
                  pl.BlockSpec((1, d), lambda i: (0, 0)),
                  pl.BlockSpec((3 * d, LANES), lambda i: (0, 0)),
                  pl.BlockSpec((1, LANES), lambda i: (0, 0))],
        out_specs=[tile, tile, tile, pl.BlockSpec((1, LANES), lambda i: (0, 0))],
        out_shape=[jax.ShapeDtypeStruct((n, LANES), I32), jax.ShapeDtypeStruct((n, LANES), F32),
                   jax.ShapeDtypeStruct((n, LANES), I32), jax.ShapeDtypeStruct((1, LANES), I32)],
        scratch_shapes=[pltpu.VMEM((1, LANES), F32)],
        compiler_params=_cparams(1, 32),
        name="moe_router",
    )(r, g.reshape(1, d), w_split, b_pad)
    return idx[:, :TOP_K], gates, rank[:, :TOP_K], counts[0, :n_exp]


def _expert_kernel(be_ref, tok0_ref, tok1_ref, tok_ahead_ref, r_hbm, g_ref, wg_ref, bg_ref, wu_ref, bu_ref,
                   wd_ref, bd_ref, ys_hbm, buf_ref, wg_bf, wu_bf, wd_bf, y_buf, sem, out_sem):
    b = pl.program_id(0)
    last = pl.num_programs(0) - 1
    out_slot = lax.rem(b, 2)
    cur = lax.rem(b, GATHER_BUFS)
    ahead = lax.rem(b + GATHER_BUFS - 1, GATHER_BUFS)
    tiles = MOE_ROWS // SUBLANES
    d_tiles = r_hbm.shape[1]

    def row_copy(tok, buf, j):
        return pltpu.make_async_copy(
            r_hbm.at[tok], buf_ref.at[buf, pl.ds(j * d_tiles, d_tiles)], sem.at[buf])

    def start_gather_loop(toks, buf):
        def start(t, c):
            for sub in range(SUBLANES):
                j = t * SUBLANES + sub
                row_copy(toks[0, 0, j], buf, j).start()
            return c
        lax.fori_loop(0, tiles, start, 0)

    def wait_gather(buf):
        def wait(t, c):
            for sub in range(SUBLANES):
                row_copy(0, buf, 0).wait()
            return c
        lax.fori_loop(0, tiles, wait, 0)

    def out_copy(slot, blk, c):
        return pltpu.make_async_copy(
            y_buf.at[slot, :, pl.ds(c * LANES, LANES)],
            ys_hbm.at[pl.ds(blk * MOE_ROWS, MOE_ROWS), c, :], out_sem.at[slot])

    def wait_out(slot):
        for c in range(d_tiles):
            out_copy(slot, 0, c).wait()

    @pl.when(b == 0)
    def _():
        start_gather_loop(tok0_ref, 0)
        start_gather_loop(tok1_ref, 1)

    @pl.when(b >= 2)
    def _():
        wait_out(out_slot)

    @pl.when(jnp.logical_or(b == 0, be_ref[b] != be_ref[jnp.maximum(b - 1, 0)]))
    def _():
        wg_bf[...] = wg_ref[0].astype(BF16)
        wu_bf[...] = wu_ref[0].astype(BF16)
        wd_bf[...] = wd_ref[0].astype(BF16)

    wait_gather(cur)

    h = _rms(_load_rows_from_tiles(buf_ref.at[cur], MOE_ROWS, d_tiles), g_ref[...]).astype(BF16)
    gate = jnp.minimum(_dot(h, wg_bf[...]) + bg_ref[0], SWIGLU_LIMIT)
    up = jnp.clip(_dot(h, wu_bf[...]) + bu_ref[0], -SWIGLU_LIMIT, SWIGLU_LIMIT)
    act = (gate * jax.nn.sigmoid(SWIGLU_ALPHA * gate) * (up + 1.0)).astype(BF16)
    for j in range(MOE_ROWS):
        row_copy(tok_ahead_ref[0, 0, j], ahead, j).start(priority=j % DMA_QUEUES)
    y_buf[out_slot] = _dot(act, wd_bf[...]) + bd_ref[0]
    for c in range(d_tiles):
        out_copy(out_slot, b, c).start()

    @pl.when(b == last)
    def _():
        for back in range(1, GATHER_BUFS):
            wait_gather(lax.rem(b + back, GATHER_BUFS))
        wait_out(1 - out_slot)
        wait_out(out_slot)


def moe_experts(r_tiles, g, slot_tok, block_expert, layer, w_gate, b_gate, w_up, b_up, w_down, b_down):
    n, d_tiles, _ = r_tiles.shape
    d = d_tiles * LANES
    n_layers, n_exp, _, d_ff = w_gate.shape
    n_blocks = block_expert.shape[0]
    assert n_blocks >= GATHER_BUFS
    toks = slot_tok.reshape(n_blocks, 1, MOE_ROWS)

    def expert4(i, be):
        return (layer, be[i], 0, 0)

    def tok_block(index):
        return pl.BlockSpec((1, 1, MOE_ROWS), index, memory_space=pltpu.SMEM)

    grid_spec = pltpu.PrefetchScalarGridSpec(
        num_scalar_prefetch=1,
        grid=(n_blocks,),
        in_specs=[tok_block(lambda i, be: (0, 0, 0)),
                  tok_block(lambda i, be: (1, 0, 0)),
                  tok_block(lambda i, be: (jnp.minimum(i + GATHER_BUFS - 1, n_blocks - 1), 0, 0)),
                  pl.BlockSpec(memory_space=pl.ANY),
                  pl.BlockSpec((1, d), lambda i, be: (0, 0)),
                  pl.BlockSpec((None, 1, d, d_ff), expert4), pl.BlockSpec((None, 1, 1, d_ff), expert4),
                  pl.BlockSpec((None, 1, d, d_ff), expert4), pl.BlockSpec((None, 1, 1, d_ff), expert4),
                  pl.BlockSpec((None, 1, d_ff, d), expert4), pl.BlockSpec((None, 1, 1, d), expert4)],
        out_specs=pl.BlockSpec(memory_space=pl.ANY),
        scratch_shapes=[pltpu.VMEM((GATHER_BUFS, MOE_ROWS * d_tiles, LANES), F32),
                        pltpu.VMEM((d, d_ff), BF16), pltpu.VMEM((d, d_ff), BF16), pltpu.VMEM((d_ff, d), BF16),
                        pltpu.VMEM((2, MOE_ROWS, d), F32),
                        pltpu.SemaphoreType.DMA((GATHER_BUFS,)), pltpu.SemaphoreType.DMA((2,))])
    return pl.pallas_call(
        _expert_kernel,
        grid_spec=grid_spec,
        out_shape=jax.ShapeDtypeStruct((n_blocks * MOE_ROWS, d_tiles, LANES), F32),
        compiler_params=_cparams(1, 56),
        name="moe_experts",
    )(block_expert, toks, toks, toks, r_tiles, g.reshape(1, d),
      w_gate, b_gate.reshape(n_layers, n_exp, 1, d_ff), w_up, b_up.reshape(n_layers, n_exp, 1, d_ff),
      w_down, b_down.reshape(n_layers, n_exp, 1, d))


def _combine_kernel(dest0_ref, dest1_ref, dest_ahead_ref, ys_hbm, gate_ref, r_ref, p_ref, gp_ref, wp_ref,
                    wpg_ref, gf_ref, o_ref, buf_ref, sem, *, tile, final_norm):
    i = pl.program_id(0)
    last = pl.num_programs(0) - 1
    cur = lax.rem(i, GATHER_BUFS)
    ahead = lax.rem(i + GATHER_BUFS - 1, GATHER_BUFS)
    tiles = tile // SUBLANES
    d_tiles = ys_hbm.shape[1]

    def row_copy(slot, buf, k, j):
        return pltpu.make_async_copy(
            ys_hbm.at[slot], buf_ref.at[buf, k, pl.ds(j * d_tiles, d_tiles)], sem.at[buf])

    def start_gather_loop(dests, buf):
        def start(t, c):
            for sub in range(SUBLANES):
                for k in range(TOP_K):
                    j = t * SUBLANES + sub
                    row_copy(dests[0, 0, j * TOP_K + k], buf, k, j).start()
            return c
        lax.fori_loop(0, tiles, start, 0)

    def wait_gather(buf):
        def wait(t, c):
            for sub in range(SUBLANES):
                for k in range(TOP_K):
                    row_copy(0, buf, k, 0).wait()
            return c
        lax.fori_loop(0, tiles, wait, 0)

    @pl.when(i == 0)
    def _():
        start_gather_loop(dest0_ref, 0)
        start_gather_loop(dest1_ref, 1)

    wait_gather(cur)
    gates = gate_ref[...]
    r = r_ref[...]
    for k in range(TOP_K):
        r = r + _load_rows_from_tiles(buf_ref.at[cur, k], tile, d_tiles) * gates[:, k:k + 1]
    h = _rms(r, gp_ref[...]).astype(BF16)
    gate = jax.nn.sigmoid(_dot(h, wpg_ref[...]))
    r = r + _dot(p_ref[...].astype(BF16), wp_ref[...]) * gate
    if final_norm:
        r = _rms(r, gf_ref[...])
    o_ref[...] = r
    for j in range(tile):
        for k in range(TOP_K):
            row_copy(dest_ahead_ref[0, 0, j * TOP_K + k], ahead, k, j).start(priority=k % DMA_QUEUES)

    @pl.when(i == last)
    def _():
        for back in range(1, GATHER_BUFS):
            wait_gather(lax.rem(i + back, GATHER_BUFS))


def moe_combine_embed(ys, dest, gates, r, p, g_ple, w_ple, w_ple_gate, g_final, final_norm, tile=256):
    n, d = r.shape
    pd = p.shape[1]
    n_tiles = n // tile
    assert n_tiles >= GATHER_BUFS
    dest_tiles = dest.reshape(n_tiles, 1, tile * TOP_K)

    def dest_block(index):
        return pl.BlockSpec((1, 1, tile * TOP_K), index, memory_space=pltpu.SMEM)

    return pl.pallas_call(
        functools.partial(_combine_kernel, tile=tile, final_norm=final_norm),
        grid=(n_tiles,),
        in_specs=[dest_block(lambda i: (0, 0, 0)),
                  dest_block(lambda i: (1, 0, 0)),
                  dest_block(lambda i: (jnp.minimum(i + GATHER_BUFS - 1, n_tiles - 1), 0, 0)),
                  pl.BlockSpec(memory_space=pl.ANY),
                  pl.BlockSpec((tile, LANES), lambda i: (i, 0)),
                  pl.BlockSpec((tile, d), lambda i: (i, 0)),
                  pl.BlockSpec((tile, pd), lambda i: (i, 0)),
                  pl.BlockSpec((1, d), lambda i: (0, 0)),
                  pl.BlockSpec((pd, d), lambda i: (0, 0)),
                  pl.BlockSpec((d, d), lambda i: (0, 0)),
                  pl.BlockSpec((1, d), lambda i: (0, 0))],
        out_specs=pl.BlockSpec((tile, d), lambda i: (i, 0)),
        out_shape=jax.ShapeDtypeStruct((n, d), F32),
        scratch_shapes=[pltpu.VMEM((GATHER_BUFS, TOP_K, tile * (d // LANES), LANES), F32),
                        pltpu.SemaphoreType.DMA((GATHER_BUFS,))],
        compiler_params=_cparams(1, 48),
        name="moe_combine_embed",
    )(dest_tiles, dest_tiles, dest_tiles, ys, gates, r, p, g_ple.reshape(1, d), w_ple, w_ple_gate,
      g_final.reshape(1, d))


def _slot_layout(idx, rank, counts, n_blocks):
    n_exp = counts.shape[0]
    padded = (counts + MOE_ROWS - 1) // MOE_ROWS * MOE_ROWS
    pad_end = jnp.cumsum(padded)
    pad_start = pad_end - padded
    cnt_start = jnp.cumsum(counts) - counts
    start_of = jnp.sum(jnp.where(idx[..., None] == jnp.arange(n_exp), pad_start, 0), axis=-1)
    dest = (start_of + rank).astype(I32).reshape(-1)
    block_first_row = jnp.arange(n_blocks) * MOE_ROWS
    block_expert = jnp.minimum(
        jnp.sum(pad_end[None, :] <= block_first_row[:, None], axis=1), n_exp - 1).astype(I32)
    sorted_tok = (jnp.argsort(dest) // TOP_K).astype(I32)
    in_expert = block_first_row[:, None] + jnp.arange(MOE_ROWS)[None, :] - pad_start[block_expert][:, None]
    real = in_expert < counts[block_expert][:, None]
    compact = jnp.clip(in_expert + cnt_start[block_expert][:, None], 0, dest.shape[0] - 1)
    slot_tok = jnp.where(real, sorted_tok[compact], 0).astype(I32)
    return dest, slot_tok, block_expert


def moe_and_embed(r, r_tiles, p, g_ffn, router_w, router_b, layer, w_gate, b_gate, w_up, b_up, w_down, b_down,
                  g_ple, w_ple, w_ple_gate, g_final, final_norm):
    n, _ = r.shape
    n_blocks = n * TOP_K // MOE_ROWS + router_w.shape[1]
    idx, gates, rank, counts = moe_router(r, g_ffn, router_w, router_b)
    dest, slot_tok, block_expert = _slot_layout(idx, rank, counts, n_blocks)
    ys = moe_experts(r_tiles, g_ffn, slot_tok, block_expert, layer, w_gate.astype(F32), b_gate.astype(F32),
                     w_up.astype(F32), b_up.astype(F32), w_down.astype(F32), b_down.astype(F32))
    return moe_combine_embed(ys, dest, gates, r, p, g_ple, w_ple.astype(BF16), w_ple_gate.astype(BF16),
                             g_final, final_norm)


def kernel(x, p, norm_mix, norm_ffn, norm_ple, norm_final, w_in_ab, ssm_a_re, ssm_a_im, ssm_b_re, ssm_b_im, ssm_c_re, ssm_c_im, ssm_d, ssm_log_step, glu_w, glu_b, w_out_ab, rel_bias, w_in_c, w_out_c, router_w, router_b, w_gate, b_gate, w_up, b_up, w_down, b_down, w_ple, w_ple_gate):
    bsz, seq, d = x.shape
    n = bsz * seq
    depth = p.shape[0]
    assert bsz == SUBLANES
    r = x.reshape(n, d).astype(F32)
    for i in range(depth):
        j = i // 2
        if i % 2 == 0:
            ssm_w = ssm_d.shape[1] * ssm_d.shape[2]
            moba_w = (w_in_ab.shape[2] - ssm_w) // 3
            u, qkv = norm_matmul(r, norm_mix[i], w_in_ab[j].astype(BF16),
                                 (ssm_w, 3 * moba_w), (F32, BF16))
            a_mat, b_mat, c_mat = _s5_params(ssm_a_re[j], ssm_a_im[j], ssm_b_re[j], ssm_b_im[j],
                                             ssm_c_re[j], ssm_c_im[j], ssm_log_step[j])
            u_tb = u.reshape(bsz, seq, ssm_w).transpose(1, 0, 2).reshape(n, ssm_w)
            y_a = s5_mixer(u_tb, a_mat, b_mat, c_mat, ssm_d[j].reshape(1, ssm_w).astype(F32),
                           glu_w[j].astype(BF16), glu_b[j].reshape(1, ssm_w).astype(F32))
            y_a = y_a.reshape(seq, bsz, ssm_w).transpose(1, 0, 2).reshape(n, ssm_w)
            y_b = moba_mixer(qkv.reshape(bsz, seq, 3 * moba_w), rel_bias, moba_w).reshape(n, moba_w)
            w_out = w_out_ab[j].astype(BF16)
            r, r_tiles = matmul_residual(r, [y_a, y_b], [w_out[:ssm_w], w_out[ssm_w:]])
        else:
            q_scale = jnp.where(jnp.arange(3 * d) < d, HEAD_DIM ** -0.5 * math.log2(math.e), 1.0)
            w_in = (w_in_c[j].astype(F32) * q_scale).astype(BF16)
            (qkv,) = norm_matmul(r, norm_mix[i], w_in, (3 * d,), (BF16,))
            y_c = stick_breaking_mixer(qkv.reshape(bsz, seq, 3 * d), d).reshape(n, d)
            r, r_tiles = matmul_residual(r, [y_c], [w_out_c[j].astype(BF16)])
        r = moe_and_embed(r, r_tiles, p[i].reshape(n, -1), norm_ffn[i], router_w[i], router_b[i],
                          i, w_gate, b_gate, w_up, b_up, w_down, b_down,
                          norm_ple[i], w_ple[i], w_ple_gate[i], norm_final, i == depth - 1)
    return r.reshape(bsz, seq, d).astype(x.dtype)
```

```python
import functools
import math

import jax
import jax.numpy as jnp
from jax import lax
from jax.experimental import pallas as pl
from jax.experimental.pallas import tpu as pltpu

F32 = jnp.float32
BF16 = jnp.bfloat16
I32 = jnp.int32

RMS_EPS = 1e-6
HEAD_DIM = 64
LANES = 128
SUBLANES = 8
SSM_GROUP = 16
SSM_STATE = 64
SSM_SLAB_GROUPS = LANES // SSM_GROUP
MOBA_BLOCK = 256
MOBA_TOPK = 3
MOBA_HEADS = 4
REL_BUCKETS = 32
REL_MAX_DIST = 128
SB_BLOCK = 256
SB_HEADS = 4
N_EXPERTS = 32
TOP_K = 4
SWIGLU_LIMIT = 7.0
SWIGLU_ALPHA = 1.702
MOE_ROWS = 256
GATHER_BUFS = 3
DMA_QUEUES = 2
NEG = -1e30
MIB = 1024 * 1024

_NT = (((1,), (1,)), ((), ()))


def _cparams(n_axes, vmem_mib):
    return pltpu.CompilerParams(
        dimension_semantics=("arbitrary",) * n_axes, vmem_limit_bytes=vmem_mib * MIB)


def _rms(x, g):
    ms = jnp.mean(x * x, axis=-1, keepdims=True)
    return x * lax.rsqrt(ms + RMS_EPS) * g


def _dot(a, b):
    return jnp.dot(a, b, preferred_element_type=F32)


def _dot_nt(a, b):
    return lax.dot_general(a, b, _NT, preferred_element_type=F32)


def _norm_matmul_kernel(x_ref, g_ref, w_ref, *o_refs, splits):
    h = _rms(x_ref[...], g_ref[...]).astype(BF16)
    y = _dot(h, w_ref[...])
    off = 0
    for o_ref, s in zip(o_refs, splits):
        o_ref[...] = y[:, off:off + s].astype(o_ref.dtype)
        off += s


def norm_matmul(x, g, w, splits, dtypes, tm=512):
    n, d = x.shape
    nout = w.shape[1]
    return pl.pallas_call(
        functools.partial(_norm_matmul_kernel, splits=splits),
        grid=(n // tm,),
        in_specs=[pl.BlockSpec((tm, d), lambda i: (i, 0)),
                  pl.BlockSpec((1, d), lambda i: (0, 0)),
                  pl.BlockSpec((d, nout), lambda i: (0, 0))],
        out_specs=[pl.BlockSpec((tm, s), lambda i: (i, 0)) for s in splits],
        out_shape=[jax.ShapeDtypeStruct((n, s), dt) for s, dt in zip(splits, dtypes)],
        compiler_params=_cparams(1, 48),
        name="norm_matmul",
    )(x, g.reshape(1, d), w)


def _load_rows_from_tiles(x2_ref, rows, d_tiles):
    return jnp.concatenate([x2_ref[pl.ds(c, rows, stride=d_tiles), :] for c in range(d_tiles)], axis=1)


def _bits(x):
    return lax.bitcast_convert_type(x, jnp.uint32)


def _store_rows_as_bf16_pair_tiles(o3_ref, x):
    for s in range(o3_ref.shape[1]):
        lo = x[:, 2 * s * LANES:(2 * s + 1) * LANES].astype(BF16).astype(F32)
        hi = x[:, (2 * s + 1) * LANES:(2 * s + 2) * LANES].astype(BF16).astype(F32)
        o3_ref[:, s, :] = (_bits(lo) >> 16) | (_bits(hi) & jnp.uint32(0xFFFF0000))


def _load_rows_from_bf16_pair_tiles(x2_ref, rows, pair_tiles):
    chunks = []
    for s in range(pair_tiles):
        word = x2_ref[pl.ds(s, rows, stride=pair_tiles), :]
        chunks.append(lax.bitcast_convert_type(word << 16, F32))
        chunks.append(lax.bitcast_convert_type(word & jnp.uint32(0xFFFF0000), F32))
    return jnp.concatenate(chunks, axis=1).astype(BF16)


def _matmul_residual_kernel(r_ref, g_ref, *refs, n_in):
    acc = r_ref[...]
    for a_ref, w_ref in zip(refs[:n_in], refs[n_in:2 * n_in]):
        acc = acc + _dot(a_ref[...], w_ref[...])
    refs[2 * n_in][...] = acc
    _store_rows_as_bf16_pair_tiles(refs[2 * n_in + 1], _rms(acc, g_ref[...]))


def matmul_residual(r, g_next, a_list, w_list, tm=512):
    n, d = r.shape
    n_in = len(a_list)
    in_specs = [pl.BlockSpec((tm, d), lambda i: (i, 0)), pl.BlockSpec((1, d), lambda i: (0, 0))]
    in_specs += [pl.BlockSpec((tm, a.shape[1]), lambda i: (i, 0)) for a in a_list]
    in_specs += [pl.BlockSpec(w.shape, lambda i: (0, 0)) for w in w_list]
    return pl.pallas_call(
        functools.partial(_matmul_residual_kernel, n_in=n_in),
        grid=(n // tm,),
        in_specs=in_specs,
        out_specs=[pl.BlockSpec((tm, d), lambda i: (i, 0)),
                   pl.BlockSpec((tm, d // (2 * LANES), LANES), lambda i: (i, 0, 0))],
        out_shape=[jax.ShapeDtypeStruct((n, d), F32),
                   jax.ShapeDtypeStruct((n, d // (2 * LANES), LANES), jnp.uint32)],
        compiler_params=_cparams(1, 40),
        name="matmul_residual",
    )(r, g_next.reshape(1, d), *a_list, *w_list)


def _s5_kernel(u_ref, a_ref, b_ref, c_ref, d_ref, gw_ref, gb_ref, o_ref, s_ref, x_ref,
               *, steps, n_slab):
    half = SSM_SLAB_GROUPS * SSM_STATE
    slab = 2 * half

    @pl.when(pl.program_id(0) == 0)
    def _():
        x_ref[...] = jnp.zeros_like(x_ref)

    u = u_ref[...]
    ub = u.astype(BF16)
    for s in range(n_slab):
        s_ref[:, s * slab:(s + 1) * slab] = _dot(ub[:, s * LANES:(s + 1) * LANES], b_ref[s])

    for s in range(n_slab):
        re = slice(s * slab, s * slab + half)
        im = slice(s * slab + half, (s + 1) * slab)
        ar = a_ref[:, re]
        ai = a_ref[:, im]

        def step(t, carry, re=re, im=im, ar=ar, ai=ai):
            xr, xi = carry
            rows = pl.ds(pl.multiple_of(t * SUBLANES, SUBLANES), SUBLANES)
            nr = ar * xr - ai * xi + s_ref[rows, re]
            ni = ar * xi + ai * xr + s_ref[rows, im]
            s_ref[rows, re] = nr
            s_ref[rows, im] = ni
            return nr, ni

        xr, xi = lax.fori_loop(0, steps, step, (x_ref[:, re], x_ref[:, im]), unroll=8)
        x_ref[:, re] = xr
        x_ref[:, im] = xi

    y = jnp.concatenate(
        [_dot(s_ref[:, s * slab:(s + 1) * slab].astype(BF16), c_ref[s]) for s in range(n_slab)],
        axis=1)
    y = y + d_ref[...] * u
    z = jax.nn.gelu(y)
    gate = jax.nn.sigmoid(_dot(z.astype(BF16), gw_ref[...]) + gb_ref[...])
    o_ref[...] = (z * gate).astype(o_ref.dtype)


def s5_mixer(u_tb, a_mat, b_mat, c_mat, d_vec, glu_w, glu_b, steps=64):
    rows, width = u_tb.shape
    n_slab = width // LANES
    n_state = 2 * n_slab * SSM_SLAB_GROUPS * SSM_STATE
    tm = steps * SUBLANES
    return pl.pallas_call(
        functools.partial(_s5_kernel, steps=steps, n_slab=n_slab),
        grid=(rows // tm,),
        in_specs=[pl.BlockSpec((tm, width), lambda i: (i, 0)),
                  pl.BlockSpec(a_mat.shape, lambda i: (0, 0)),
                  pl.BlockSpec(b_mat.shape, lambda i: (0, 0, 0)),
                  pl.BlockSpec(c_mat.shape, lambda i: (0, 0, 0)),
                  pl.BlockSpec((1, width), lambda i: (0, 0)),
                  pl.BlockSpec(glu_w.shape, lambda i: (0, 0)),
                  pl.BlockSpec((1, width), lambda i: (0, 0))],
        out_specs=pl.BlockSpec((tm, width), lambda i: (i, 0)),
        out_shape=jax.ShapeDtypeStruct((rows, width), BF16),
        scratch_shapes=[pltpu.VMEM((tm, n_state), F32), pltpu.VMEM((SUBLANES, n_state), F32)],
        compiler_params=_cparams(1, 40),
        name="s5_mixer",
    )(u_tb, a_mat, b_mat, c_mat, d_vec, glu_w, glu_b)


def _s5_params(a_re, a_im, b_re, b_im, c_re, c_im, log_step):
    n_grp = a_re.shape[0]
    n_slab = n_grp // SSM_SLAB_GROUPS
    lam = lax.complex(a_re.astype(F32), a_im.astype(F32))
    step = jnp.exp(log_step.astype(F32))[:, None]
    lam_bar = jnp.exp(lam * step)
    b_bar = ((lam_bar - 1.0) / lam)[:, :, None] * lax.complex(b_re.astype(F32), b_im.astype(F32))
    eye = jnp.eye(SSM_SLAB_GROUPS, dtype=F32)

    def slabbed(t):
        return t.reshape((n_slab, SSM_SLAB_GROUPS) + t.shape[1:])

    a_mat = jnp.concatenate(
        [slabbed(jnp.real(lam_bar)).reshape(n_slab, -1), slabbed(jnp.imag(lam_bar)).reshape(n_slab, -1)],
        axis=1).reshape(1, -1)
    a_mat = jnp.broadcast_to(a_mat, (SUBLANES, a_mat.shape[1]))
    b_parts = [jnp.einsum('sgnh,gk->sghkn', slabbed(part(b_bar)), eye)
               for part in (jnp.real, jnp.imag)]
    b_mat = jnp.stack(b_parts, axis=3).reshape(n_slab, LANES, -1)
    c_parts = [jnp.einsum('sghn,gk->sgnkh', slabbed(part), eye)
               for part in (c_re.astype(F32), -c_im.astype(F32))]
    c_mat = jnp.stack(c_parts, axis=1).reshape(n_slab, -1, LANES)
    return a_mat, b_mat.astype(BF16), c_mat.astype(BF16)


def _by_head(lane, cols):
    out = cols[-1]
    for h in reversed(range(len(cols) - 1)):
        out = jnp.where(lane // HEAD_DIM == h, cols[h], out)
    return out


def _moba_kernel(q_ref, k_ref, v_ref, tb_ref, o_ref, mean_ref, km_ref, *, n_blk):
    blk = MOBA_BLOCK
    heads = MOBA_HEADS
    width = heads * HEAD_DIM
    rows_all = heads * blk
    qb = pl.program_id(2)

    @pl.when(qb == 0)
    def _():
        mean_ref[...] = jnp.zeros_like(mean_ref)
        for n in range(n_blk):
            kb = k_ref[0, n * blk:(n + 1) * blk, :].astype(F32)
            mean_ref[n:n + 1, :] = jnp.sum(kb, axis=0, keepdims=True) / blk
        mean = mean_ref[...]
        hi = mean.astype(BF16)
        km_ref[:, :width] = hi
        km_ref[:, width:] = (mean - hi.astype(F32)).astype(BF16)

    q = q_ref[0]
    lane = lax.broadcasted_iota(I32, (blk, width), 1)
    q_stack = jnp.concatenate(
        [jnp.where(lane // HEAD_DIM == h, q, jnp.zeros_like(q)) for h in range(heads)], axis=0)

    blane = lax.broadcasted_iota(I32, (rows_all, LANES), 1)
    gate = _dot_nt(jnp.concatenate([q_stack, q_stack], axis=1), km_ref[...])
    gate = jnp.where(blane < qb, gate, -jnp.inf)
    sel = jnp.zeros((rows_all, LANES), F32)
    blane_f = blane.astype(F32)
    for r in range(MOBA_TOPK):
        top = jnp.max(gate, axis=1, keepdims=True)
        idx = jnp.min(jnp.where(gate == top, blane_f, float(LANES)), axis=1, keepdims=True)
        hit = blane_f == idx
        sel = jnp.where(jnp.logical_and(hit, r < qb), 1.0, sel)
        gate = jnp.where(hit, -jnp.inf, gate)

    qs = q_stack * (HEAD_DIM ** -0.5)
    row = lax.broadcasted_iota(I32, (rows_all, blk), 0)
    col = lax.broadcasted_iota(I32, (rows_all, blk), 1)

    def rows_of(n):
        return pl.ds(pl.multiple_of(n * blk, blk), blk)

    def weighted_values(p, n):
        vn = v_ref[0, rows_of(n), :]
        p = p.astype(BF16)
        p_cat = jnp.concatenate([p[h * blk:(h + 1) * blk] for h in range(heads)], axis=1)
        v_stack = jnp.concatenate(
            [jnp.where(lane // HEAD_DIM == h, vn, jnp.zeros_like(vn)) for h in range(heads)], axis=0)
        return _dot(p_cat, v_stack)

    def per_head(x):
        return _by_head(lane, [x[h * blk:(h + 1) * blk] for h in range(heads)])

    s = _dot_nt(qs, k_ref[0, rows_of(qb), :]) + tb_ref[0, 0]
    s = jnp.where(col <= (row & (blk - 1)), s, NEG)
    m = jnp.max(s, axis=1, keepdims=True)
    p = jnp.exp(s - m)
    l = jnp.sum(p, axis=1, keepdims=True)
    acc = weighted_values(p, qb)

    def past_blocks(first, per_trip):
        def body(i, carry):
            m, l, acc = carry
            for u in range(per_trip):
                n = first + per_trip * i + u
                chosen = jnp.sum(jnp.where(blane == n, sel, 0.0), axis=1, keepdims=True) > 0.0
                s = _dot_nt(qs, k_ref[0, rows_of(n), :]) + tb_ref[0, jnp.minimum(qb - n, 2)]
                s = jnp.where(chosen, s, NEG)
                m_new = jnp.maximum(m, jnp.max(s, axis=1, keepdims=True))
                alpha = jnp.exp(m - m_new)
                p = jnp.exp(s - m_new)
                l = alpha * l + jnp.sum(p, axis=1, keepdims=True)
                acc = per_head(alpha) * acc + weighted_values(p, n)
                m = m_new
            return m, l, acc
        return body

    odd = qb & 1
    carry = lax.fori_loop(0, odd, past_blocks(0, 1), (m, l, acc))
    _, l, acc = lax.fori_loop(0, qb // 2, past_blocks(odd, 2), carry)
    o_ref[0] = (acc / per_head(l)).astype(o_ref.dtype)


def _rel_bucket(dist):
    exact = REL_BUCKETS // 2
    n = jnp.maximum(dist, 0)
    nf = jnp.maximum(n, 1).astype(F32)
    log_ratio = jnp.log(nf / exact) / math.log(REL_MAX_DIST / exact)
    large = exact + (log_ratio * (REL_BUCKETS - exact)).astype(I32)
    large = jnp.minimum(large, REL_BUCKETS - 1)
    return jnp.where(n < exact, n, large)


def _moba_bias_tables(rel_bias):
    assert REL_MAX_DIST <= MOBA_BLOCK + 1
    i = jnp.arange(MOBA_BLOCK)[:, None]
    j = jnp.arange(MOBA_BLOCK)[None, :]
    dist = jnp.arange(3)[:, None, None] * MOBA_BLOCK + (i - j)[None]
    onehot = (_rel_bucket(dist)[..., None] == jnp.arange(REL_BUCKETS)).astype(F32)
    return jnp.einsum('oijb,bh->hoij', onehot, rel_bias.astype(F32), precision=lax.Precision.HIGHEST)


def moba_mixer(qkv, rel_bias, width):
    bsz, seq, _ = qkv.shape
    blk = MOBA_BLOCK
    n_blk = seq // blk
    gw = MOBA_HEADS * HEAD_DIM
    n_grp = width // gw
    assert seq % blk == 0 and n_blk <= LANES
    tables = _moba_bias_tables(rel_bias).reshape(n_grp, MOBA_HEADS, 3, blk, blk)
    tables = tables.transpose(0, 2, 1, 3, 4).reshape(n_grp, 3, MOBA_HEADS * blk, blk)
    return pl.pallas_call(
        functools.partial(_moba_kernel, n_blk=n_blk),
        grid=(bsz, n_grp, n_blk),
        in_specs=[pl.BlockSpec((1, blk, gw), lambda b, p, i: (b, i, p)),
                  pl.BlockSpec((1, seq, gw), lambda b, p, i: (b, 0, n_grp + p)),
                  pl.BlockSpec((1, seq, gw), lambda b, p, i: (b, 0, 2 * n_grp + p)),
                  pl.BlockSpec((1, 3, MOBA_HEADS * blk, blk), lambda b, p, i: (p, 0, 0, 0))],
        out_specs=pl.BlockSpec((1, blk, gw), lambda b, p, i: (b, i, p)),
        out_shape=jax.ShapeDtypeStruct((bsz, seq, width), BF16),
        scratch_shapes=[pltpu.VMEM((LANES, gw), F32), pltpu.VMEM((LANES, 2 * gw), BF16)],
        compiler_params=_cparams(3, 48),
        name="moba_mixer",
    )(qkv, qkv, qkv, tables)


def _sb_kernel(q_ref, k_ref, v_ref, tri_ref, o_ref):
    blk = SB_BLOCK
    width = SB_HEADS * HEAD_DIM
    n_sub = blk // LANES
    rows_all = SB_HEADS * blk
    qb = pl.program_id(2)
    q = q_ref[0]
    lane = lax.broadcasted_iota(I32, (blk, width), 1)
    q_stack = jnp.concatenate(
        [jnp.where(lane // HEAD_DIM == h, q, jnp.zeros_like(q)) for h in range(SB_HEADS)],
        axis=0)
    row = lax.broadcasted_iota(I32, (rows_all, blk), 0)
    col = lax.broadcasted_iota(I32, (rows_all, blk), 1)
    past = col < (row & (blk - 1))

    def rows_of(n):
        return pl.ds(pl.multiple_of(n * blk, blk), blk)

    def logits(n):
        return _dot_nt(q_stack, k_ref[0, rows_of(n), :])

    def weights(z, carried, diagonal):
        neg_abs = lax.bitcast_convert_type(
            lax.bitcast_convert_type(z, jnp.uint32) | jnp.uint32(0x80000000), F32)
        drop = jnp.maximum(z, 0.0) + jnp.log2(1.0 + jnp.exp2(neg_abs))
        if diagonal:
            drop = jnp.where(past, drop, 0.0)
        hi32 = lax.bitcast_convert_type(
            lax.bitcast_convert_type(drop, jnp.uint32) & jnp.uint32(0xFFFF0000), F32)
        hi = hi32.astype(BF16)
        lo = (drop - hi32).astype(BF16)
        lhs = jnp.concatenate(
            [jnp.concatenate([hi[:, c * LANES:(c + 1) * LANES], lo[:, c * LANES:(c + 1) * LANES]], axis=1)
             for c in range(n_sub)], axis=0)
        sums = _dot(lhs, tri_ref[...])
        newer = carried
        from_key = [None] * n_sub
        for c in reversed(range(n_sub)):
            within = sums[c * rows_all:(c + 1) * rows_all, :LANES]
            total = sums[c * rows_all:(c + 1) * rows_all, LANES:]
            from_key[c] = within if newer is None else within + newer
            newer = total if newer is None else newer + total
        w = jnp.exp2(z - jnp.concatenate(from_key, axis=1))
        if diagonal:
            w = jnp.where(past, w, 0.0)
        w = w.astype(BF16)
        return jnp.concatenate([w[h * blk:(h + 1) * blk] for h in range(SB_HEADS)], axis=1), newer

    def weighted_values(w_cat, n):
        vn = v_ref[0, rows_of(n), :]
        v_stack = jnp.concatenate(
            [jnp.where(lane // HEAD_DIM == h, vn, jnp.zeros_like(vn)) for h in range(SB_HEADS)], axis=0)
        return _dot(w_cat, v_stack)

    w_cat, carried = weights(logits(qb), None, True)
    acc = weighted_values(w_cat, qb)

    def older_blocks(first, per_trip):
        def body(i, carry):
            acc, carried = carry
            for u in range(per_trip):
                n = first - per_trip * i - u
                w_cat, carried = weights(logits(n), carried, False)
                acc = acc + weighted_values(w_cat, n)
            return acc, carried
        return body

    odd = qb & 1
    acc, carried = lax.fori_loop(0, odd, older_blocks(qb - 1, 1), (acc, carried))
    acc, _ = lax.fori_loop(0, qb // 2, older_blocks(qb - 1 - odd, 2), (acc, carried))
    o_ref[0] = acc.astype(o_ref.dtype)


def stick_breaking_mixer(qkv, width):
    bsz, seq, _ = qkv.shape
    blk = SB_BLOCK
    gw = SB_HEADS * HEAD_DIM
    n_grp = width // gw
    tri = (jnp.arange(LANES)[:, None] >= jnp.arange(LANES)[None, :]).astype(BF16)
    tri = jnp.concatenate([tri, jnp.ones((LANES, LANES), BF16)], axis=1)
    tri = jnp.concatenate([tri, tri], axis=0)
    return pl.pallas_call(
        _sb_kernel,
        grid=(bsz, n_grp, seq // blk),
        in_specs=[pl.BlockSpec((1, blk, gw), lambda b, p, i: (b, i, p)),
                  pl.BlockSpec((1, seq, gw), lambda b, p, i: (b, 0, n_grp + p)),
                  pl.BlockSpec((1, seq, gw), lambda b, p, i: (b, 0, 2 * n_grp + p)),
                  pl.BlockSpec(tri.shape, lambda b, p, i: (0, 0))],
        out_specs=pl.BlockSpec((1, blk, gw), lambda b, p, i: (b, i, p)),
        out_shape=jax.ShapeDtypeStruct((bsz, seq, width), BF16),
        compiler_params=_cparams(3, 48),
        name="stick_breaking",
    )(qkv, qkv, qkv, tri)


def _router_kernel(r_ref, g_ref, w_ref, b_ref, idx_ref, gate_ref, rank_ref, cnt_ref, run_ref):
    tm = r_ref.shape[0]

    @pl.when(pl.program_id(0) == 0)
    def _():
        run_ref[...] = jnp.zeros_like(run_ref)

    h = _rms(r_ref[...], g_ref[...])
    h_hi = h.astype(BF16)
    h_lo = (h - h_hi.astype(F32)).astype(BF16)
    logits = _dot(jnp.concatenate([h_hi, h_hi, h_lo], axis=1), w_ref[...]) + b_ref[...]
    lane = lax.broadcasted_iota(I32, (tm, LANES), 1)
    lane_f = lane.astype(F32)
    tops, hits = [], []
    for _ in range(TOP_K):
        top = jnp.max(logits, axis=1, keepdims=True)
        idx = jnp.min(jnp.where(logits == top, lane_f, float(LANES)), axis=1, keepdims=True)
        hit = lane_f == idx
        logits = jnp.where(hit, -jnp.inf, logits)
        tops.append(top)
        hits.append(hit)
    exps = [jnp.exp(t - tops[0]) for t in tops]
    denom = exps[0]
    for e in exps[1:]:
        denom = denom + e

    member = jnp.zeros((tm, LANES), F32)
    for hit in hits:
        member = jnp.where(hit, 1.0, member)
    before = (lax.broadcasted_iota(I32, (tm, tm), 1) < lax.broadcasted_iota(I32, (tm, tm), 0))
    ahead = _dot(before.astype(BF16), member.astype(BF16)) + run_ref[...]

    idx_out = jnp.zeros((tm, LANES), F32)
    gate_out = jnp.zeros((tm, LANES), F32)
    rank_out = jnp.zeros((tm, LANES), F32)
    for k in range(TOP_K):
        idx_k = jnp.sum(jnp.where(hits[k], lane_f, 0.0), axis=1, keepdims=True)
        rank_k = jnp.sum(jnp.where(hits[k], ahead, 0.0), axis=1, keepdims=True)
        idx_out = jnp.where(lane == k, idx_k, idx_out)
        gate_out = jnp.where(lane == k, exps[k] / denom, gate_out)
        rank_out = jnp.where(lane == k, rank_k, rank_out)
    idx_ref[...] = idx_out.astype(I32)
    gate_ref[...] = gate_out
    rank_ref[...] = rank_out.astype(I32)
    run_ref[...] = run_ref[...] + jnp.sum(member, axis=0, keepdims=True)
    cnt_ref[...] = run_ref[...].astype(I32)


def moe_router(r, g, router_w, router_b, tm=512):
    n, d = r.shape
    n_exp = router_w.shape[1]
    w_pad = jnp.pad(router_w.astype(F32), ((0, 0), (0, LANES - n_exp)))
    w_hi = w_pad.astype(BF16)
    w_lo = (w_pad - w_hi.astype(F32)).astype(BF16)
    w_split = jnp.concatenate([w_hi, w_lo, w_hi], axis=0)
    b_pad = jnp.pad(router_b.astype(F32).reshape(1, n_exp), ((0, 0), (0, LANES - n_exp)),
                    constant_values=-jnp.inf)
    tile = pl.BlockSpec((tm, LANES), lambda i: (i, 0))
    idx, gates, rank, counts = pl.pallas_call(
        _router_kernel,
        grid=(n // tm,),
        in_specs=[pl.BlockSpec((tm, d), lambda i: (i, 0)),
                  pl.BlockSpec((1, d), lambda i: (0, 0)),
                  pl.BlockSpec((3 * d, LANES), lambda i: (0, 0)),
                  pl.BlockSpec((1, LANES), lambda i: (0, 0))],
        out_specs=[tile, tile, tile, pl.BlockSpec((1, LANES), lambda i: (0, 0))],
        out_shape=[jax.ShapeDtypeStruct((n, LANES), I32), jax.ShapeDtypeStruct((n, LANES), F32),
                   jax.ShapeDtypeStruct((n, LANES), I32), jax.ShapeDtypeStruct((1, LANES), I32)],
        scratch_shapes=[pltpu.VMEM((1, LANES), F32)],
        compiler_params=_cparams(1, 32),
        name="moe_router",
    )(r, g.reshape(1, d), w_split, b_pad)
    return idx[:, :TOP_K], gates, rank[:, :TOP_K], counts[0, :n_exp]


def _expert_kernel(be_ref, tok0_ref, tok1_ref, tok_ahead_ref, h_hbm, wg_ref, bg_ref, wu_ref, bu_ref,
                   wd_ref, bd_ref, ys_hbm, buf_ref, wg_bf, wu_bf, wd_bf, y_buf, sem, out_sem):
    b = pl.program_id(0)
    last = pl.num_programs(0) - 1
    out_slot = lax.rem(b, 2)
    cur = lax.rem(b, GATHER_BUFS)
    ahead = lax.rem(b + GATHER_BUFS - 1, GATHER_BUFS)
    tiles = MOE_ROWS // SUBLANES
    pair_tiles = h_hbm.shape[1]
    d_tiles = 2 * pair_tiles

    def row_copy(tok, buf, j):
        return pltpu.make_async_copy(
            h_hbm.at[tok], buf_ref.at[buf, pl.ds(j * pair_tiles, pair_tiles)], sem.at[buf])

    def start_gather_loop(toks, buf):
        def start(t, c):
            for sub in range(SUBLANES):
                j = t * SUBLANES + sub
                row_copy(toks[0, 0, j], buf, j).start()
            return c
        lax.fori_loop(0, tiles, start, 0)

    def wait_gather(buf):
        def wait(t, c):
            for sub in range(SUBLANES):
                row_copy(0, buf, 0).wait()
            return c
        lax.fori_loop(0, tiles, wait, 0)

    def out_copy(slot, blk, c):
        return pltpu.make_async_copy(
            y_buf.at[slot, :, pl.ds(c * LANES, LANES)],
            ys_hbm.at[pl.ds(blk * MOE_ROWS, MOE_ROWS), c, :], out_sem.at[slot])

    def wait_out(slot):
        for c in range(d_tiles):
            out_copy(slot, 0, c).wait()

    @pl.when(b == 0)
    def _():
        start_gather_loop(tok0_ref, 0)
        start_gather_loop(tok1_ref, 1)

    @pl.when(b >= 2)
    def _():
        wait_out(out_slot)

    @pl.when(jnp.logical_or(b == 0, be_ref[b] != be_ref[jnp.maximum(b - 1, 0)]))
    def _():
        wg_bf[...] = wg_ref[0].astype(BF16)
        wu_bf[...] = wu_ref[0].astype(BF16)
        wd_bf[...] = wd_ref[0].astype(BF16)

    wait_gather(cur)

    h = _load_rows_from_bf16_pair_tiles(buf_ref.at[cur], MOE_ROWS, pair_tiles)
    gate = jnp.minimum(_dot(h, wg_bf[...]) + bg_ref[0], SWIGLU_LIMIT)
    up = jnp.clip(_dot(h, wu_bf[...]) + bu_ref[0], -SWIGLU_LIMIT, SWIGLU_LIMIT)
    act = (gate * jax.nn.sigmoid(SWIGLU_ALPHA * gate) * (up + 1.0)).astype(BF16)
    for j in range(MOE_ROWS):
        row_copy(tok_ahead_ref[0, 0, j], ahead, j).start(priority=j % DMA_QUEUES)
    y_buf[out_slot] = _dot(act, wd_bf[...]) + bd_ref[0]
    for c in range(d_tiles):
        out_copy(out_slot, b, c).start()

    @pl.when(b == last)
    def _():
        for back in range(1, GATHER_BUFS):
            wait_gather(lax.rem(b + back, GATHER_BUFS))
        wait_out(1 - out_slot)
        wait_out(out_slot)


def moe_experts(h_tiles, slot_tok, block_expert, layer, w_gate, b_gate, w_up, b_up, w_down, b_down):
    n, pair_tiles, _ = h_tiles.shape
    d_tiles = 2 * pair_tiles
    d = d_tiles * LANES
    n_layers, n_exp, _, d_ff = w_gate.shape
    n_blocks = block_expert.shape[0]
    assert n_blocks >= GATHER_BUFS
    toks = slot_tok.reshape(n_blocks, 1, MOE_ROWS)

    def expert4(i, be):
        return (layer, be[i], 0, 0)

    def tok_block(index):
        return pl.BlockSpec((1, 1, MOE_ROWS), index, memory_space=pltpu.SMEM)

    grid_spec = pltpu.PrefetchScalarGridSpec(
        num_scalar_prefetch=1,
        grid=(n_blocks,),
        in_specs=[tok_block(lambda i, be: (0, 0, 0)),
                  tok_block(lambda i, be: (1, 0, 0)),
                  tok_block(lambda i, be: (jnp.minimum(i + GATHER_BUFS - 1, n_blocks - 1), 0, 0)),
                  pl.BlockSpec(memory_space=pl.ANY),
                  pl.BlockSpec((None, 1, d, d_ff), expert4), pl.BlockSpec((None, 1, 1, d_ff), expert4),
                  pl.BlockSpec((None, 1, d, d_ff), expert4), pl.BlockSpec((None, 1, 1, d_ff), expert4),
                  pl.BlockSpec((None, 1, d_ff, d), expert4), pl.BlockSpec((None, 1, 1, d), expert4)],
        out_specs=pl.BlockSpec(memory_space=pl.ANY),
        scratch_shapes=[pltpu.VMEM((GATHER_BUFS, MOE_ROWS * pair_tiles, LANES), jnp.uint32),
                        pltpu.VMEM((d, d_ff), BF16), pltpu.VMEM((d, d_ff), BF16), pltpu.VMEM((d_ff, d), BF16),
                        pltpu.VMEM((2, MOE_ROWS, d), F32),
                        pltpu.SemaphoreType.DMA((GATHER_BUFS,)), pltpu.SemaphoreType.DMA((2,))])
    return pl.pallas_call(
        _expert_kernel,
        grid_spec=grid_spec,
        out_shape=jax.ShapeDtypeStruct((n_blocks * MOE_ROWS, d_tiles, LANES), F32),
        compiler_params=_cparams(1, 56),
        name="moe_experts",
    )(block_expert, toks, toks, toks, h_tiles,
      w_gate, b_gate.reshape(n_layers, n_exp, 1, d_ff), w_up, b_up.reshape(n_layers, n_exp, 1, d_ff),
      w_down, b_down.reshape(n_layers, n_exp, 1, d))


def _combine_kernel(dest0_ref, dest1_ref, dest_ahead_ref, ys_hbm, gate_ref, r_ref, p_ref, gp_ref, wp_ref,
                    wpg_ref, gf_ref, o_ref, buf_ref, sem, *, tile, final_norm):
    i = pl.program_id(0)
    last = pl.num_programs(0) - 1
    cur = lax.rem(i, GATHER_BUFS)
    ahead = lax.rem(i + GATHER_BUFS - 1, GATHER_BUFS)
    tiles = tile // SUBLANES
    d_tiles = ys_hbm.shape[1]

    def row_copy(slot, buf, k, j):
        return pltpu.make_async_copy(
            ys_hbm.at[slot], buf_ref.at[buf, k, pl.ds(j * d_tiles, d_tiles)], sem.at[buf])

    def start_gather_loop(dests, buf):
        def start(t, c):
            for sub in range(SUBLANES):
                for k in range(TOP_K):
                    j = t * SUBLANES + sub
                    row_copy(dests[0, 0, j * TOP_K + k], buf, k, j).start()
            return c
        lax.fori_loop(0, tiles, start, 0)

    def wait_gather(buf):
        def wait(t, c):
            for sub in range(SUBLANES):
                for k in range(TOP_K):
                    row_copy(0, buf, k, 0).wait()
            return c
        lax.fori_loop(0, tiles, wait, 0)

    @pl.when(i == 0)
    def _():
        start_gather_loop(dest0_ref, 0)
        start_gather_loop(dest1_ref, 1)

    wait_gather(cur)
    gates = gate_ref[...]
    r = r_ref[...]
    for k in range(TOP_K):
        r = r + _load_rows_from_tiles(buf_ref.at[cur, k], tile, d_tiles) * gates[:, k:k + 1]
    h = _rms(r, gp_ref[...]).astype(BF16)
    gate = jax.nn.sigmoid(_dot(h, wpg_ref[...]))
    r = r + _dot(p_ref[...].astype(BF16), wp_ref[...]) * gate
    if final_norm:
        r = _rms(r, gf_ref[...])
    o_ref[...] = r
    for j in range(tile):
        for k in range(TOP_K):
            row_copy(dest_ahead_ref[0, 0, j * TOP_K + k], ahead, k, j).start(priority=k % DMA_QUEUES)

    @pl.when(i == last)
    def _():
        for back in range(1, GATHER_BUFS):
            wait_gather(lax.rem(i + back, GATHER_BUFS))


def moe_combine_embed(ys, dest, gates, r, p, g_ple, w_ple, w_ple_gate, g_final, final_norm, tile=256):
    n, d = r.shape
    pd = p.shape[1]
    n_tiles = n // tile
    assert n_tiles >= GATHER_BUFS
    dest_tiles = dest.reshape(n_tiles, 1, tile * TOP_K)

    def dest_block(index):
        return pl.BlockSpec((1, 1, tile * TOP_K), index, memory_space=pltpu.SMEM)

    return pl.pallas_call(
        functools.partial(_combine_kernel, tile=tile, final_norm=final_norm),
        grid=(n_tiles,),
        in_specs=[dest_block(lambda i: (0, 0, 0)),
                  dest_block(lambda i: (1, 0, 0)),
                  dest_block(lambda i: (jnp.minimum(i + GATHER_BUFS - 1, n_tiles - 1), 0, 0)),
                  pl.BlockSpec(memory_space=pl.ANY),
                  pl.BlockSpec((tile, LANES), lambda i: (i, 0)),
                  pl.BlockSpec((tile, d), lambda i: (i, 0)),
                  pl.BlockSpec((tile, pd), lambda i: (i, 0)),
                  pl.BlockSpec((1, d), lambda i: (0, 0)),
                  pl.BlockSpec((pd, d), lambda i: (0, 0)),
                  pl.BlockSpec((d, d), lambda i: (0, 0)),
                  pl.BlockSpec((1, d), lambda i: (0, 0))],
        out_specs=pl.BlockSpec((tile, d), lambda i: (i, 0)),
        out_shape=jax.ShapeDtypeStruct((n, d), F32),
        scratch_shapes=[pltpu.VMEM((GATHER_BUFS, TOP_K, tile * (d // LANES), LANES), F32),
                        pltpu.SemaphoreType.DMA((GATHER_BUFS,))],
        compiler_params=_cparams(1, 48),
        name="moe_combine_embed",
    )(dest_tiles, dest_tiles, dest_tiles, ys, gates, r, p, g_ple.reshape(1, d), w_ple, w_ple_gate,
      g_final.reshape(1, d))


def _slot_layout(idx, rank, counts, n_blocks):
    n_exp = counts.shape[0]
    padded = (counts + MOE_ROWS - 1) // MOE_ROWS * MOE_ROWS
    pad_end = jnp.cumsum(padded)
    pad_start = pad_end - padded
    cnt_start = jnp.cumsum(counts) - counts
    start_of = jnp.sum(jnp.where(idx[..., None] == jnp.arange(n_exp), pad_start, 0), axis=-1)
    dest = (start_of + rank).astype(I32).reshape(-1)
    block_first_row = jnp.arange(n_blocks) * MOE_ROWS
    block_expert = jnp.minimum(
        jnp.sum(pad_end[None, :] <= block_first_row[:, None], axis=1), n_exp - 1).astype(I32)
    sorted_tok = (jnp.argsort(dest) // TOP_K).astype(I32)
    in_expert = block_first_row[:, None] + jnp.arange(MOE_ROWS)[None, :] - pad_start[block_expert][:, None]
    real = in_expert < counts[block_expert][:, None]
    compact = jnp.clip(in_expert + cnt_start[block_expert][:, None], 0, dest.shape[0] - 1)
    slot_tok = jnp.where(real, sorted_tok[compact], 0).astype(I32)
    return dest, slot_tok, block_expert


def moe_and_embed(r, h_tiles, p, g_ffn, router_w, router_b, layer, w_gate, b_gate, w_up, b_up, w_down, b_down,
                  g_ple, w_ple, w_ple_gate, g_final, final_norm):
    n, _ = r.shape
    n_blocks = n * TOP_K // MOE_ROWS + router_w.shape[1]
    idx, gates, rank, counts = moe_router(r, g_ffn, router_w, router_b)
    dest, slot_tok, block_expert = _slot_layout(idx, rank, counts, n_blocks)
    ys = moe_experts(h_tiles, slot_tok, block_expert, layer, w_gate.astype(F32), b_gate.astype(F32),
                     w_up.astype(F32), b_up.astype(F32), w_down.astype(F32), b_down.astype(F32))
    return moe_combine_embed(ys, dest, gates, r, p, g_ple, w_ple.astype(BF16), w_ple_gate.astype(BF16),
                             g_final, final_norm)


def kernel(x, p, norm_mix, norm_ffn, norm_ple, norm_final, w_in_ab, ssm_a_re, ssm_a_im, ssm_b_re, ssm_b_im, ssm_c_re, ssm_c_im, ssm_d, ssm_log_step, glu_w, glu_b, w_out_ab, rel_bias, w_in_c, w_out_c, router_w, router_b, w_gate, b_gate, w_up, b_up, w_down, b_down, w_ple, w_ple_gate):
    bsz, seq, d = x.shape
    n = bsz * seq
    depth = p.shape[0]
    assert bsz == SUBLANES
    r = x.reshape(n, d).astype(F32)
    for i in range(depth):
        j = i // 2
        if i % 2 == 0:
            ssm_w = ssm_d.shape[1] * ssm_d.shape[2]
            moba_w = (w_in_ab.shape[2] - ssm_w) // 3
            u, qkv = norm_matmul(r, norm_mix[i], w_in_ab[j].astype(BF16),
                                 (ssm_w, 3 * moba_w), (F32, BF16))
            a_mat, b_mat, c_mat = _s5_params(ssm_a_re[j], ssm_a_im[j], ssm_b_re[j], ssm_b_im[j],
                                             ssm_c_re[j], ssm_c_im[j], ssm_log_step[j])
            u_tb = u.reshape(bsz, seq, ssm_w).transpose(1, 0, 2).reshape(n, ssm_w)
            y_a = s5_mixer(u_tb, a_mat, b_mat, c_mat, ssm_d[j].reshape(1, ssm_w).astype(F32),
                           glu_w[j].astype(BF16), glu_b[j].reshape(1, ssm_w).astype(F32))
            y_a = y_a.reshape(seq, bsz, ssm_w).transpose(1, 0, 2).reshape(n, ssm_w)
            y_b = moba_mixer(qkv.reshape(bsz, seq, 3 * moba_w), rel_bias, moba_w).reshape(n, moba_w)
            w_out = w_out_ab[j].astype(BF16)
            r, h_tiles = matmul_residual(r, norm_ffn[i], [y_a, y_b], [w_out[:ssm_w], w_out[ssm_w:]])
        else:
            q_scale = jnp.where(jnp.arange(3 * d) < d, HEAD_DIM ** -0.5 * math.log2(math.e), 1.0)
            w_in = (w_in_c[j].astype(F32) * q_scale).astype(BF16)
            (qkv,) = norm_matmul(r, norm_mix[i], w_in, (3 * d,), (BF16,))
            y_c = stick_breaking_mixer(qkv.reshape(bsz, seq, 3 * d), d).reshape(n, d)
            r, h_tiles = matmul_residual(r, norm_ffn[i], [y_c], [w_out_c[j].astype(BF16)])
        r = moe_and_embed(r, h_tiles, p[i].reshape(n, -1), norm_ffn[i], router_w[i], router_b[i],
                          i, w_gate, b_gate, w_up, b_up, w_down, b_down,
                          norm_ple[i], w_ple[i], w_ple_gate[i], norm_final, i == depth - 1)
    return r.reshape(bsz, seq, d).astype(x.dtype)
```

```python
import functools
import math

import jax
import jax.numpy as jnp
from jax import lax
from jax.experimental import pallas as pl
from jax.experimental.pallas import tpu as pltpu

F32 = jnp.float32
BF16 = jnp.bfloat16
I32 = jnp.int32

RMS_EPS = 1e-6
HEAD_DIM = 64
LANES = 128
SUBLANES = 8
SSM_GROUP = 16
SSM_STATE = 64
SSM_SLAB_GROUPS = LANES // SSM_GROUP
MOBA_BLOCK = 256
MOBA_TOPK = 3
MOBA_HEADS = 4
REL_BUCKETS = 32
REL_MAX_DIST = 128
SB_BLOCK = 256
SB_HEADS = 4
N_EXPERTS = 32
TOP_K = 4
SWIGLU_LIMIT = 7.0
SWIGLU_ALPHA = 1.702
MOE_ROWS = 256
GATHER_BUFS = 3
DMA_QUEUES = 2
NEG = -1e30
MIB = 1024 * 1024

_NT = (((1,), (1,)), ((), ()))


def _cparams(n_axes, vmem_mib):
    return pltpu.CompilerParams(
        dimension_semantics=("arbitrary",) * n_axes, vmem_limit_bytes=vmem_mib * MIB)


def _rms(x, g):
    ms = jnp.mean(x * x, axis=-1, keepdims=True)
    return x * lax.rsqrt(ms + RMS_EPS) * g


def _dot(a, b):
    return jnp.dot(a, b, preferred_element_type=F32)


def _dot_nt(a, b):
    return lax.dot_general(a, b, _NT, preferred_element_type=F32)


def _norm_matmul_kernel(x_ref, g_ref, w_ref, *o_refs, splits):
    h = _rms(x_ref[...], g_ref[...]).astype(BF16)
    y = _dot(h, w_ref[...])
    off = 0
    for o_ref, s in zip(o_refs, splits):
        o_ref[...] = y[:, off:off + s].astype(o_ref.dtype)
        off += s


def norm_matmul(x, g, w, splits, dtypes, tm=512):
    n, d = x.shape
    nout = w.shape[1]
    return pl.pallas_call(
        functools.partial(_norm_matmul_kernel, splits=splits),
        grid=(n // tm,),
        in_specs=[pl.BlockSpec((tm, d), lambda i: (i, 0)),
                  pl.BlockSpec((1, d), lambda i: (0, 0)),
                  pl.BlockSpec((d, nout), lambda i: (0, 0))],
        out_specs=[pl.BlockSpec((tm, s), lambda i: (i, 0)) for s in splits],
        out_shape=[jax.ShapeDtypeStruct((n, s), dt) for s, dt in zip(splits, dtypes)],
        compiler_params=_cparams(1, 48),
        name="norm_matmul",
    )(x, g.reshape(1, d), w)


def _load_rows_from_tiles(x2_ref, rows, d_tiles):
    return jnp.concatenate([x2_ref[pl.ds(c, rows, stride=d_tiles), :] for c in range(d_tiles)], axis=1)


def _bits(x):
    return lax.bitcast_convert_type(x, jnp.uint32)


def _store_rows_as_bf16_pair_tiles(o3_ref, x):
    for s in range(o3_ref.shape[1]):
        lo = x[:, 2 * s * LANES:(2 * s + 1) * LANES].astype(BF16).astype(F32)
        hi = x[:, (2 * s + 1) * LANES:(2 * s + 2) * LANES].astype(BF16).astype(F32)
        o3_ref[:, s, :] = (_bits(lo) >> 16) | (_bits(hi) & jnp.uint32(0xFFFF0000))


def _load_rows_from_bf16_pair_tiles(x2_ref, rows, pair_tiles):
    chunks = []
    for s in range(pair_tiles):
        word = x2_ref[pl.ds(s, rows, stride=pair_tiles), :]
        chunks.append(lax.bitcast_convert_type(word << 16, F32))
        chunks.append(lax.bitcast_convert_type(word & jnp.uint32(0xFFFF0000), F32))
    return jnp.concatenate(chunks, axis=1).astype(BF16)


def _matmul_residual_kernel(r_ref, g_ref, *refs, n_in):
    acc = r_ref[...]
    for a_ref, w_ref in zip(refs[:n_in], refs[n_in:2 * n_in]):
        acc = acc + _dot(a_ref[...], w_ref[...])
    refs[2 * n_in][...] = acc
    _store_rows_as_bf16_pair_tiles(refs[2 * n_in + 1], _rms(acc, g_ref[...]))


def matmul_residual(r, g_next, a_list, w_list, tm=512):
    n, d = r.shape
    n_in = len(a_list)
    in_specs = [pl.BlockSpec((tm, d), lambda i: (i, 0)), pl.BlockSpec((1, d), lambda i: (0, 0))]
    in_specs += [pl.BlockSpec((tm, a.shape[1]), lambda i: (i, 0)) for a in a_list]
    in_specs += [pl.BlockSpec(w.shape, lambda i: (0, 0)) for w in w_list]
    return pl.pallas_call(
        functools.partial(_matmul_residual_kernel, n_in=n_in),
        grid=(n // tm,),
        in_specs=in_specs,
        out_specs=[pl.BlockSpec((tm, d), lambda i: (i, 0)),
                   pl.BlockSpec((tm, d // (2 * LANES), LANES), lambda i: (i, 0, 0))],
        out_shape=[jax.ShapeDtypeStruct((n, d), F32),
                   jax.ShapeDtypeStruct((n, d // (2 * LANES), LANES), jnp.uint32)],
        compiler_params=_cparams(1, 40),
        name="matmul_residual",
    )(r, g_next.reshape(1, d), *a_list, *w_list)


def _s5_kernel(u_ref, a_ref, b_ref, c_ref, d_ref, gw_ref, gb_ref, o_ref, s_ref, x_ref,
               *, steps, n_slab):
    half = SSM_SLAB_GROUPS * SSM_STATE
    slab = 2 * half

    @pl.when(pl.program_id(0) == 0)
    def _():
        x_ref[...] = jnp.zeros_like(x_ref)

    u = u_ref[...]
    ub = u.astype(BF16)
    for s in range(n_slab):
        s_ref[:, s * slab:(s + 1) * slab] = _dot(ub[:, s * LANES:(s + 1) * LANES], b_ref[s])

    for s in range(n_slab):
        re = slice(s * slab, s * slab + half)
        im = slice(s * slab + half, (s + 1) * slab)
        ar = a_ref[:, re]
        ai = a_ref[:, im]

        def step(t, carry, re=re, im=im, ar=ar, ai=ai):
            xr, xi = carry
            rows = pl.ds(pl.multiple_of(t * SUBLANES, SUBLANES), SUBLANES)
            nr = ar * xr - ai * xi + s_ref[rows, re]
            ni = ar * xi + ai * xr + s_ref[rows, im]
            s_ref[rows, re] = nr
            s_ref[rows, im] = ni
            return nr, ni

        xr, xi = lax.fori_loop(0, steps, step, (x_ref[:, re], x_ref[:, im]), unroll=8)
        x_ref[:, re] = xr
        x_ref[:, im] = xi

    y = jnp.concatenate(
        [_dot(s_ref[:, s * slab:(s + 1) * slab].astype(BF16), c_ref[s]) for s in range(n_slab)],
        axis=1)
    y = y + d_ref[...] * u
    z = jax.nn.gelu(y)
    gate = jax.nn.sigmoid(_dot(z.astype(BF16), gw_ref[...]) + gb_ref[...])
    o_ref[...] = (z * gate).astype(o_ref.dtype)


def s5_mixer(u_tb, a_mat, b_mat, c_mat, d_vec, glu_w, glu_b, steps=64):
    rows, width = u_tb.shape
    n_slab = width // LANES
    n_state = 2 * n_slab * SSM_SLAB_GROUPS * SSM_STATE
    tm = steps * SUBLANES
    return pl.pallas_call(
        functools.partial(_s5_kernel, steps=steps, n_slab=n_slab),
        grid=(rows // tm,),
        in_specs=[pl.BlockSpec((tm, width), lambda i: (i, 0)),
                  pl.BlockSpec(a_mat.shape, lambda i: (0, 0)),
                  pl.BlockSpec(b_mat.shape, lambda i: (0, 0, 0)),
                  pl.BlockSpec(c_mat.shape, lambda i: (0, 0, 0)),
                  pl.BlockSpec((1, width), lambda i: (0, 0)),
                  pl.BlockSpec(glu_w.shape, lambda i: (0, 0)),
                  pl.BlockSpec((1, width), lambda i: (0, 0))],
        out_specs=pl.BlockSpec((tm, width), lambda i: (i, 0)),
        out_shape=jax.ShapeDtypeStruct((rows, width), BF16),
        scratch_shapes=[pltpu.VMEM((tm, n_state), F32), pltpu.VMEM((SUBLANES, n_state), F32)],
        compiler_params=_cparams(1, 40),
        name="s5_mixer",
    )(u_tb, a_mat, b_mat, c_mat, d_vec, glu_w, glu_b)


def _s5_params(a_re, a_im, b_re, b_im, c_re, c_im, log_step):
    n_grp = a_re.shape[0]
    n_slab = n_grp // SSM_SLAB_GROUPS
    lam = lax.complex(a_re.astype(F32), a_im.astype(F32))
    step = jnp.exp(log_step.astype(F32))[:, None]
    lam_bar = jnp.exp(lam * step)
    b_bar = ((lam_bar - 1.0) / lam)[:, :, None] * lax.complex(b_re.astype(F32), b_im.astype(F32))
    eye = jnp.eye(SSM_SLAB_GROUPS, dtype=F32)

    def slabbed(t):
        return t.reshape((n_slab, SSM_SLAB_GROUPS) + t.shape[1:])

    a_mat = jnp.concatenate(
        [slabbed(jnp.real(lam_bar)).reshape(n_slab, -1), slabbed(jnp.imag(lam_bar)).reshape(n_slab, -1)],
        axis=1).reshape(1, -1)
    a_mat = jnp.broadcast_to(a_mat, (SUBLANES, a_mat.shape[1]))
    b_parts = [jnp.einsum('sgnh,gk->sghkn', slabbed(part(b_bar)), eye)
               for part in (jnp.real, jnp.imag)]
    b_mat = jnp.stack(b_parts, axis=3).reshape(n_slab, LANES, -1)
    c_parts = [jnp.einsum('sghn,gk->sgnkh', slabbed(part), eye)
               for part in (c_re.astype(F32), -c_im.astype(F32))]
    c_mat = jnp.stack(c_parts, axis=1).reshape(n_slab, -1, LANES)
    return a_mat, b_mat.astype(BF16), c_mat.astype(BF16)


def _by_head(lane, cols):
    out = cols[-1]
    for h in reversed(range(len(cols) - 1)):
        out = jnp.where(lane // HEAD_DIM == h, cols[h], out)
    return out


def _moba_kernel(q_ref, k_ref, v_ref, tb_ref, o_ref, mean_ref, km_ref, *, n_blk):
    blk = MOBA_BLOCK
    heads = MOBA_HEADS
    width = heads * HEAD_DIM
    rows_all = heads * blk
    qb = pl.program_id(2)

    @pl.when(qb == 0)
    def _():
        mean_ref[...] = jnp.zeros_like(mean_ref)
        for n in range(n_blk):
            kb = k_ref[0, n * blk:(n + 1) * blk, :].astype(F32)
            mean_ref[n:n + 1, :] = jnp.sum(kb, axis=0, keepdims=True) / blk
        mean = mean_ref[...]
        hi = mean.astype(BF16)
        km_ref[:, :width] = hi
        km_ref[:, width:] = (mean - hi.astype(F32)).astype(BF16)

    q = q_ref[0]
    lane = lax.broadcasted_iota(I32, (blk, width), 1)
    q_stack = jnp.concatenate(
        [jnp.where(lane // HEAD_DIM == h, q, jnp.zeros_like(q)) for h in range(heads)], axis=0)

    blane = lax.broadcasted_iota(I32, (rows_all, LANES), 1)
    gate = _dot_nt(jnp.concatenate([q_stack, q_stack], axis=1), km_ref[...])
    gate = jnp.where(blane < qb, gate, -jnp.inf)
    sel = jnp.zeros((rows_all, LANES), F32)
    blane_f = blane.astype(F32)
    for r in range(MOBA_TOPK):
        top = jnp.max(gate, axis=1, keepdims=True)
        idx = jnp.min(jnp.where(gate == top, blane_f, float(LANES)), axis=1, keepdims=True)
        hit = blane_f == idx
        sel = jnp.where(jnp.logical_and(hit, r < qb), 1.0, sel)
        gate = jnp.where(hit, -jnp.inf, gate)

    qs = q_stack * (HEAD_DIM ** -0.5)
    row = lax.broadcasted_iota(I32, (rows_all, blk), 0)
    col = lax.broadcasted_iota(I32, (rows_all, blk), 1)

    def rows_of(n):
        return pl.ds(pl.multiple_of(n * blk, blk), blk)

    def weighted_values(p, n):
        vn = v_ref[0, rows_of(n), :]
        p = p.astype(BF16)
        p_cat = jnp.concatenate([p[h * blk:(h + 1) * blk] for h in range(heads)], axis=1)
        v_stack = jnp.concatenate(
            [jnp.where(lane // HEAD_DIM == h, vn, jnp.zeros_like(vn)) for h in range(heads)], axis=0)
        return _dot(p_cat, v_stack)

    def per_head(x):
        return _by_head(lane, [x[h * blk:(h + 1) * blk] for h in range(heads)])

    s = _dot_nt(qs, k_ref[0, rows_of(qb), :]) + tb_ref[0, 0]
    s = jnp.where(col <= (row & (blk - 1)), s, NEG)
    m = jnp.max(s, axis=1, keepdims=True)
    p = jnp.exp(s - m)
    l = jnp.sum(p, axis=1, keepdims=True)
    acc = weighted_values(p, qb)

    def past_blocks(first, per_trip):
        def body(i, carry):
            m, l, acc = carry
            for u in range(per_trip):
                n = first + per_trip * i + u
                chosen = jnp.sum(jnp.where(blane == n, sel, 0.0), axis=1, keepdims=True) > 0.0
                s = _dot_nt(qs, k_ref[0, rows_of(n), :]) + tb_ref[0, jnp.minimum(qb - n, 2)]
                s = jnp.where(chosen, s, NEG)
                m_new = jnp.maximum(m, jnp.max(s, axis=1, keepdims=True))
                alpha = jnp.exp(m - m_new)
                p = jnp.exp(s - m_new)
                l = alpha * l + jnp.sum(p, axis=1, keepdims=True)
                acc = per_head(alpha) * acc + weighted_values(p, n)
                m = m_new
            return m, l, acc
        return body

    odd = qb & 1
    carry = lax.fori_loop(0, odd, past_blocks(0, 1), (m, l, acc))
    _, l, acc = lax.fori_loop(0, qb // 2, past_blocks(odd, 2), carry)
    o_ref[0] = (acc / per_head(l)).astype(o_ref.dtype)


def _rel_bucket(dist):
    exact = REL_BUCKETS // 2
    n = jnp.maximum(dist, 0)
    nf = jnp.maximum(n, 1).astype(F32)
    log_ratio = jnp.log(nf / exact) / math.log(REL_MAX_DIST / exact)
    large = exact + (log_ratio * (REL_BUCKETS - exact)).astype(I32)
    large = jnp.minimum(large, REL_BUCKETS - 1)
    return jnp.where(n < exact, n, large)


def _moba_bias_tables(rel_bias):
    assert REL_MAX_DIST <= MOBA_BLOCK + 1
    i = jnp.arange(MOBA_BLOCK)[:, None]
    j = jnp.arange(MOBA_BLOCK)[None, :]
    dist = jnp.arange(3)[:, None, None] * MOBA_BLOCK + (i - j)[None]
    onehot = (_rel_bucket(dist)[..., None] == jnp.arange(REL_BUCKETS)).astype(F32)
    return jnp.einsum('oijb,bh->hoij', onehot, rel_bias.astype(F32), precision=lax.Precision.HIGHEST)


def moba_mixer(qkv, rel_bias, width):
    bsz, seq, _ = qkv.shape
    blk = MOBA_BLOCK
    n_blk = seq // blk
    gw = MOBA_HEADS * HEAD_DIM
    n_grp = width // gw
    assert seq % blk == 0 and n_blk <= LANES
    tables = _moba_bias_tables(rel_bias).reshape(n_grp, MOBA_HEADS, 3, blk, blk)
    tables = tables.transpose(0, 2, 1, 3, 4).reshape(n_grp, 3, MOBA_HEADS * blk, blk)
    return pl.pallas_call(
        functools.partial(_moba_kernel, n_blk=n_blk),
        grid=(bsz, n_grp, n_blk),
        in_specs=[pl.BlockSpec((1, blk, gw), lambda b, p, i: (b, i, p)),
                  pl.BlockSpec((1, seq, gw), lambda b, p, i: (b, 0, n_grp + p)),
                  pl.BlockSpec((1, seq, gw), lambda b, p, i: (b, 0, 2 * n_grp + p)),
                  pl.BlockSpec((1, 3, MOBA_HEADS * blk, blk), lambda b, p, i: (p, 0, 0, 0))],
        out_specs=pl.BlockSpec((1, blk, gw), lambda b, p, i: (b, i, p)),
        out_shape=jax.ShapeDtypeStruct((bsz, seq, width), BF16),
        scratch_shapes=[pltpu.VMEM((LANES, gw), F32), pltpu.VMEM((LANES, 2 * gw), BF16)],
        compiler_params=_cparams(3, 48),
        name="moba_mixer",
    )(qkv, qkv, qkv, tables)


def _sb_kernel(q_ref, k_ref, v_ref, tri_ref, o_ref):
    blk = SB_BLOCK
    width = SB_HEADS * HEAD_DIM
    n_sub = blk // LANES
    cols_all = SB_HEADS * blk
    qb = pl.program_id(2)
    feat = lax.broadcasted_iota(I32, (width, blk), 0)
    q_t = q_ref[0].astype(F32).T
    q_stack = jnp.concatenate(
        [jnp.where(feat // HEAD_DIM == h, q_t, 0.0) for h in range(SB_HEADS)], axis=1).astype(BF16)
    key = lax.broadcasted_iota(I32, (blk, cols_all), 0)
    qry = lax.broadcasted_iota(I32, (blk, cols_all), 1) & (blk - 1)
    past = key < qry

    def rows_of(n):
        return pl.ds(pl.multiple_of(n * blk, blk), blk)

    def logits(n):
        return _dot(k_ref[0, rows_of(n), :], q_stack)

    def weights(z, carried, diagonal):
        neg_abs = lax.bitcast_convert_type(
            lax.bitcast_convert_type(z, jnp.uint32) | jnp.uint32(0x80000000), F32)
        drop = jnp.maximum(z, 0.0) + jnp.log2(1.0 + jnp.exp2(neg_abs))
        if diagonal:
            drop = jnp.where(past, drop, 0.0)
        hi32 = lax.bitcast_convert_type(
            lax.bitcast_convert_type(drop, jnp.uint32) & jnp.uint32(0xFFFF0000), F32)
        hi = hi32.astype(BF16)
        lo = (drop - hi32).astype(BF16)
        newer = carried
        from_key = [None] * n_sub
        for c in reversed(range(n_sub)):
            grp = slice(c * LANES, (c + 1) * LANES)
            within = _dot(tri_ref[...], jnp.concatenate([hi[grp], lo[grp]], axis=0))
            total = within[0:1]
            from_key[c] = within if newer is None else within + newer
            newer = total if newer is None else newer + total
        w = jnp.exp2(z - jnp.concatenate(from_key, axis=0))
        if diagonal:
            w = jnp.where(past, w, 0.0)
        return w.astype(BF16), newer

    hd = lax.broadcasted_iota(I32, (width, blk), 0)

    def weighted_values(w, n):
        v_t = v_ref[0, rows_of(n), :].astype(F32).T.astype(BF16)
        v_stack = jnp.concatenate(
            [jnp.where(hd // HEAD_DIM == h, v_t, jnp.zeros_like(v_t)) for h in range(SB_HEADS)], axis=1)
        w_rows = jnp.concatenate([w[:, h * blk:(h + 1) * blk] for h in range(SB_HEADS)], axis=0)
        return _dot(v_stack, w_rows)

    w, carried = weights(logits(qb), None, True)
    acc = weighted_values(w, qb)

    def older_blocks(first, per_trip):
        def body(i, carry):
            acc, carried = carry
            for u in range(per_trip):
                n = first - per_trip * i - u
                w, carried = weights(logits(n), carried, False)
                acc = acc + weighted_values(w, n)
            return acc, carried
        return body

    odd = qb & 1
    acc, carried = lax.fori_loop(0, odd, older_blocks(qb - 1, 1), (acc, carried))
    acc, _ = lax.fori_loop(0, qb // 2, older_blocks(qb - 1 - odd, 2), (acc, carried))
    o_ref[0] = acc.T.astype(o_ref.dtype)


def stick_breaking_mixer(qkv, width):
    bsz, seq, _ = qkv.shape
    blk = SB_BLOCK
    gw = SB_HEADS * HEAD_DIM
    n_grp = width // gw
    tri = (jnp.arange(LANES)[None, :] >= jnp.arange(LANES)[:, None]).astype(BF16)
    tri = jnp.concatenate([tri, tri], axis=1)
    return pl.pallas_call(
        _sb_kernel,
        grid=(bsz, n_grp, seq // blk),
        in_specs=[pl.BlockSpec((1, blk, gw), lambda b, p, i: (b, i, p)),
                  pl.BlockSpec((1, seq, gw), lambda b, p, i: (b, 0, n_grp + p)),
                  pl.BlockSpec((1, seq, gw), lambda b, p, i: (b, 0, 2 * n_grp + p)),
                  pl.BlockSpec(tri.shape, lambda b, p, i: (0, 0))],
        out_specs=pl.BlockSpec((1, blk, gw), lambda b, p, i: (b, i, p)),
        out_shape=jax.ShapeDtypeStruct((bsz, seq, width), BF16),
        compiler_params=_cparams(3, 48),
        name="stick_breaking",
    )(qkv, qkv, qkv, tri)


def _router_kernel(r_ref, g_ref, w_ref, b_ref, idx_ref, gate_ref, rank_ref, cnt_ref, run_ref):
    tm = r_ref.shape[0]

    @pl.when(pl.program_id(0) == 0)
    def _():
        run_ref[...] = jnp.zeros_like(run_ref)

    h = _rms(r_ref[...], g_ref[...])
    h_hi = h.astype(BF16)
    h_lo = (h - h_hi.astype(F32)).astype(BF16)
    logits = _dot(jnp.concatenate([h_hi, h_hi, h_lo], axis=1), w_ref[...]) + b_ref[...]
    lane = lax.broadcasted_iota(I32, (tm, LANES), 1)
    lane_f = lane.astype(F32)
    tops, hits = [], []
    for _ in range(TOP_K):
        top = jnp.max(logits, axis=1, keepdims=True)
        idx = jnp.min(jnp.where(logits == top, lane_f, float(LANES)), axis=1, keepdims=True)
        hit = lane_f == idx
        logits = jnp.where(hit, -jnp.inf, logits)
        tops.append(top)
        hits.append(hit)
    exps = [jnp.exp(t - tops[0]) for t in tops]
    denom = exps[0]
    for e in exps[1:]:
        denom = denom + e

    member = jnp.zeros((tm, LANES), F32)
    for hit in hits:
        member = jnp.where(hit, 1.0, member)
    before = (lax.broadcasted_iota(I32, (tm, tm), 1) < lax.broadcasted_iota(I32, (tm, tm), 0))
    ahead = _dot(before.astype(BF16), member.astype(BF16)) + run_ref[...]

    idx_out = jnp.zeros((tm, LANES), F32)
    gate_out = jnp.zeros((tm, LANES), F32)
    rank_out = jnp.zeros((tm, LANES), F32)
    for k in range(TOP_K):
        idx_k = jnp.sum(jnp.where(hits[k], lane_f, 0.0), axis=1, keepdims=True)
        rank_k = jnp.sum(jnp.where(hits[k], ahead, 0.0), axis=1, keepdims=True)
        idx_out = jnp.where(lane == k, idx_k, idx_out)
        gate_out = jnp.where(lane == k, exps[k] / denom, gate_out)
        rank_out = jnp.where(lane == k, rank_k, rank_out)
    idx_ref[...] = idx_out.astype(I32)
    gate_ref[...] = gate_out
    rank_ref[...] = rank_out.astype(I32)
    run_ref[...] = run_ref[...] + jnp.sum(member, axis=0, keepdims=True)
    cnt_ref[...] = run_ref[...].astype(I32)


def moe_router(r, g, router_w, router_b, tm=512):
    n, d = r.shape
    n_exp = router_w.shape[1]
    w_pad = jnp.pad(router_w.astype(F32), ((0, 0), (0, LANES - n_exp)))
    w_hi = w_pad.astype(BF16)
    w_lo = (w_pad - w_hi.astype(F32)).astype(BF16)
    w_split = jnp.concatenate([w_hi, w_lo, w_hi], axis=0)
    b_pad = jnp.pad(router_b.astype(F32).reshape(1, n_exp), ((0, 0), (0, LANES - n_exp)),
                    constant_values=-jnp.inf)
    tile = pl.BlockSpec((tm, LANES), lambda i: (i, 0))
    idx, gates, rank, counts = pl.pallas_call(
        _router_kernel,
        grid=(n // tm,),
        in_specs=[pl.BlockSpec((tm, d), lambda i: (i, 0)),
                  pl.BlockSpec((1, d), lambda i: (0, 0)),
                  pl.BlockSpec((3 * d, LANES), lambda i: (0, 0)),
                  pl.BlockSpec((1, LANES), lambda i: (0, 0))],
        out_specs=[tile, tile, tile, pl.BlockSpec((1, LANES), lambda i: (0, 0))],
        out_shape=[jax.ShapeDtypeStruct((n, LANES), I32), jax.ShapeDtypeStruct((n, LANES), F32),
                   jax.ShapeDtypeStruct((n, LANES), I32), jax.ShapeDtypeStruct((1, LANES), I32)],
        scratch_shapes=[pltpu.VMEM((1, LANES), F32)],
        compiler_params=_cparams(1, 32),
        name="moe_router",
    )(r, g.reshape(1, d), w_split, b_pad)
    return idx[:, :TOP_K], gates, rank[:, :TOP_K], counts[0, :n_exp]


def _expert_kernel(be_ref, tok0_ref, tok1_ref, tok_ahead_ref, h_hbm, wg_ref, bg_ref, wu_ref, bu_ref,
                   wd_ref, bd_ref, ys_hbm, buf_ref, wg_bf, wu_bf, wd_bf, y_buf, sem, out_sem):
    b = pl.program_id(0)
    last = pl.num_programs(0) - 1
    out_slot = lax.rem(b, 2)
    cur = lax.rem(b, GATHER_BUFS)
    ahead = lax.rem(b + GATHER_BUFS - 1, GATHER_BUFS)
    tiles = MOE_ROWS // SUBLANES
    pair_tiles = h_hbm.shape[1]
    d_tiles = 2 * pair_tiles

    def row_copy(tok, buf, j):
        return pltpu.make_async_copy(
            h_hbm.at[tok], buf_ref.at[buf, pl.ds(j * pair_tiles, pair_tiles)], sem.at[buf])

    def start_gather_loop(toks, buf):
        def start(t, c):
            for sub in range(SUBLANES):
                j = t * SUBLANES + sub
                row_copy(toks[0, 0, j], buf, j).start()
            return c
        lax.fori_loop(0, tiles, start, 0)

    def wait_gather(buf):
        def wait(t, c):
            for sub in range(SUBLANES):
                row_copy(0, buf, 0).wait()
            return c
        lax.fori_loop(0, tiles, wait, 0)

    def out_copy(slot, blk, c):
        return pltpu.make_async_copy(
            y_buf.at[slot, :, pl.ds(c * LANES, LANES)],
            ys_hbm.at[pl.ds(blk * MOE_ROWS, MOE_ROWS), c, :], out_sem.at[slot])

    def wait_out(slot):
        for c in range(d_tiles):
            out_copy(slot, 0, c).wait()

    @pl.when(b == 0)
    def _():
        start_gather_loop(tok0_ref, 0)
        start_gather_loop(tok1_ref, 1)

    @pl.when(b >= 2)
    def _():
        wait_out(out_slot)

    @pl.when(jnp.logical_or(b == 0, be_ref[b] != be_ref[jnp.maximum(b - 1, 0)]))
    def _():
        wg_bf[...] = wg_ref[0].astype(BF16)
        wu_bf[...] = wu_ref[0].astype(BF16)
        wd_bf[...] = wd_ref[0].astype(BF16)

    wait_gather(cur)

    h = _load_rows_from_bf16_pair_tiles(buf_ref.at[cur], MOE_ROWS, pair_tiles)
    gate = jnp.minimum(_dot(h, wg_bf[...]) + bg_ref[0], SWIGLU_LIMIT)
    up = jnp.clip(_dot(h, wu_bf[...]) + bu_ref[0], -SWIGLU_LIMIT, SWIGLU_LIMIT)
    act = (gate * jax.nn.sigmoid(SWIGLU_ALPHA * gate) * (up + 1.0)).astype(BF16)
    for j in range(MOE_ROWS):
        row_copy(tok_ahead_ref[0, 0, j], ahead, j).start(priority=j % DMA_QUEUES)
    y_buf[out_slot] = _dot(act, wd_bf[...]) + bd_ref[0]
    for c in range(d_tiles):
        out_copy(out_slot, b, c).start()

    @pl.when(b == last)
    def _():
        for back in range(1, GATHER_BUFS):
            wait_gather(lax.rem(b + back, GATHER_BUFS))
        wait_out(1 - out_slot)
        wait_out(out_slot)


def moe_experts(h_tiles, slot_tok, block_expert, layer, w_gate, b_gate, w_up, b_up, w_down, b_down):
    n, pair_tiles, _ = h_tiles.shape
    d_tiles = 2 * pair_tiles
    d = d_tiles * LANES
    n_layers, n_exp, _, d_ff = w_gate.shape
    n_blocks = block_expert.shape[0]
    assert n_blocks >= GATHER_BUFS
    toks = slot_tok.reshape(n_blocks, 1, MOE_ROWS)

    def expert4(i, be):
        return (layer, be[i], 0, 0)

    def tok_block(index):
        return pl.BlockSpec((1, 1, MOE_ROWS), index, memory_space=pltpu.SMEM)

    grid_spec = pltpu.PrefetchScalarGridSpec(
        num_scalar_prefetch=1,
        grid=(n_blocks,),
        in_specs=[tok_block(lambda i, be: (0, 0, 0)),
                  tok_block(lambda i, be: (1, 0, 0)),
                  tok_block(lambda i, be: (jnp.minimum(i + GATHER_BUFS - 1, n_blocks - 1), 0, 0)),
                  pl.BlockSpec(memory_space=pl.ANY),
                  pl.BlockSpec((None, 1, d, d_ff), expert4), pl.BlockSpec((None, 1, 1, d_ff), expert4),
                  pl.BlockSpec((None, 1, d, d_ff), expert4), pl.BlockSpec((None, 1, 1, d_ff), expert4),
                  pl.BlockSpec((None, 1, d_ff, d), expert4), pl.BlockSpec((None, 1, 1, d), expert4)],
        out_specs=pl.BlockSpec(memory_space=pl.ANY),
        scratch_shapes=[pltpu.VMEM((GATHER_BUFS, MOE_ROWS * pair_tiles, LANES), jnp.uint32),
                        pltpu.VMEM((d, d_ff), BF16), pltpu.VMEM((d, d_ff), BF16), pltpu.VMEM((d_ff, d), BF16),
                        pltpu.VMEM((2, MOE_ROWS, d), F32),
                        pltpu.SemaphoreType.DMA((GATHER_BUFS,)), pltpu.SemaphoreType.DMA((2,))])
    return pl.pallas_call(
        _expert_kernel,
        grid_spec=grid_spec,
        out_shape=jax.ShapeDtypeStruct((n_blocks * MOE_ROWS, d_tiles, LANES), F32),
        compiler_params=_cparams(1, 56),
        name="moe_experts",
    )(block_expert, toks, toks, toks, h_tiles,
      w_gate, b_gate.reshape(n_layers, n_exp, 1, d_ff), w_up, b_up.reshape(n_layers, n_exp, 1, d_ff),
      w_down, b_down.reshape(n_layers, n_exp, 1, d))


def _combine_kernel(dest0_ref, dest1_ref, dest_ahead_ref, ys_hbm, gate_ref, r_ref, p_ref, gp_ref, wp_ref,
                    wpg_ref, gf_ref, o_ref, buf_ref, sem, *, tile, final_norm):
    i = pl.program_id(0)
    last = pl.num_programs(0) - 1
    cur = lax.rem(i, GATHER_BUFS)
    ahead = lax.rem(i + GATHER_BUFS - 1, GATHER_BUFS)
    tiles = tile // SUBLANES
    d_tiles = ys_hbm.shape[1]

    def row_copy(slot, buf, k, j):
        return pltpu.make_async_copy(
            ys_hbm.at[slot], buf_ref.at[buf, k, pl.ds(j * d_tiles, d_tiles)], sem.at[buf])

    def start_gather_loop(dests, buf):
        def start(t, c):
            for sub in range(SUBLANES):
                for k in range(TOP_K):
                    j = t * SUBLANES + sub
                    row_copy(dests[0, 0, j * TOP_K + k], buf, k, j).start()
            return c
        lax.fori_loop(0, tiles, start, 0)

    def wait_gather(buf):
        def wait(t, c):
            for sub in range(SUBLANES):
                for k in range(TOP_K):
                    row_copy(0, buf, k, 0).wait()
            return c
        lax.fori_loop(0, tiles, wait, 0)

    @pl.when(i == 0)
    def _():
        start_gather_loop(dest0_ref, 0)
        start_gather_loop(dest1_ref, 1)

    wait_gather(cur)
    gates = gate_ref[...]
    r = r_ref[...]
    for k in range(TOP_K):
        r = r + _load_rows_from_tiles(buf_ref.at[cur, k], tile, d_tiles) * gates[:, k:k + 1]
    h = _rms(r, gp_ref[...]).astype(BF16)
    gate = jax.nn.sigmoid(_dot(h, wpg_ref[...]))
    r = r + _dot(p_ref[...].astype(BF16), wp_ref[...]) * gate
    if final_norm:
        r = _rms(r, gf_ref[...])
    o_ref[...] = r
    for j in range(tile):
        for k in range(TOP_K):
            row_copy(dest_ahead_ref[0, 0, j * TOP_K + k], ahead, k, j).start(priority=k % DMA_QUEUES)

    @pl.when(i == last)
    def _():
        for back in range(1, GATHER_BUFS):
            wait_gather(lax.rem(i + back, GATHER_BUFS))


def moe_combine_embed(ys, dest, gates, r, p, g_ple, w_ple, w_ple_gate, g_final, final_norm, tile=256):
    n, d = r.shape
    pd = p.shape[1]
    n_tiles = n // tile
    assert n_tiles >= GATHER_BUFS
    dest_tiles = dest.reshape(n_tiles, 1, tile * TOP_K)

    def dest_block(index):
        return pl.BlockSpec((1, 1, tile * TOP_K), index, memory_space=pltpu.SMEM)

    return pl.pallas_call(
        functools.partial(_combine_kernel, tile=tile, final_norm=final_norm),
        grid=(n_tiles,),
        in_specs=[dest_block(lambda i: (0, 0, 0)),
                  dest_block(lambda i: (1, 0, 0)),
                  dest_block(lambda i: (jnp.minimum(i + GATHER_BUFS - 1, n_tiles - 1), 0, 0)),
                  pl.BlockSpec(memory_space=pl.ANY),
                  pl.BlockSpec((tile, LANES), lambda i: (i, 0)),
                  pl.BlockSpec((tile, d), lambda i: (i, 0)),
                  pl.BlockSpec((tile, pd), lambda i: (i, 0)),
                  pl.BlockSpec((1, d), lambda i: (0, 0)),
                  pl.BlockSpec((pd, d), lambda i: (0, 0)),
                  pl.BlockSpec((d, d), lambda i: (0, 0)),
                  pl.BlockSpec((1, d), lambda i: (0, 0))],
        out_specs=pl.BlockSpec((tile, d), lambda i: (i, 0)),
        out_shape=jax.ShapeDtypeStruct((n, d), F32),
        scratch_shapes=[pltpu.VMEM((GATHER_BUFS, TOP_K, tile * (d // LANES), LANES), F32),
                        pltpu.SemaphoreType.DMA((GATHER_BUFS,))],
        compiler_params=_cparams(1, 48),
        name="moe_combine_embed",
    )(dest_tiles, dest_tiles, dest_tiles, ys, gates, r, p, g_ple.reshape(1, d), w_ple, w_ple_gate,
      g_final.reshape(1, d))


def _slot_layout(idx, rank, counts, n_blocks):
    n_exp = counts.shape[0]
    padded = (counts + MOE_ROWS - 1) // MOE_ROWS * MOE_ROWS
    pad_end = jnp.cumsum(padded)
    pad_start = pad_end - padded
    cnt_start = jnp.cumsum(counts) - counts
    start_of = jnp.sum(jnp.where(idx[..., None] == jnp.arange(n_exp), pad_start, 0), axis=-1)
    dest = (start_of + rank).astype(I32).reshape(-1)
    block_first_row = jnp.arange(n_blocks) * MOE_ROWS
    block_expert = jnp.minimum(
        jnp.sum(pad_end[None, :] <= block_first_row[:, None], axis=1), n_exp - 1).astype(I32)
    sorted_tok = (jnp.argsort(dest) // TOP_K).astype(I32)
    in_expert = block_first_row[:, None] + jnp.arange(MOE_ROWS)[None, :] - pad_start[block_expert][:, None]
    real = in_expert < counts[block_expert][:, None]
    compact = jnp.clip(in_expert + cnt_start[block_expert][:, None], 0, dest.shape[0] - 1)
    slot_tok = jnp.where(real, sorted_tok[compact], 0).astype(I32)
    return dest, slot_tok, block_expert


def moe_and_embed(r, h_tiles, p, g_ffn, router_w, router_b, layer, w_gate, b_gate, w_up, b_up, w_down, b_down,
                  g_ple, w_ple, w_ple_gate, g_final, final_norm):
    n, _ = r.shape
    n_blocks = n * TOP_K // MOE_ROWS + router_w.shape[1]
    idx, gates, rank, counts = moe_router(r, g_ffn, router_w, router_b)
    dest, slot_tok, block_expert = _slot_layout(idx, rank, counts, n_blocks)
    ys = moe_experts(h_tiles, slot_tok, block_expert, layer, w_gate.astype(F32), b_gate.astype(F32),
                     w_up.astype(F32), b_up.astype(F32), w_down.astype(F32), b_down.astype(F32))
    return moe_combine_embed(ys, dest, gates, r, p, g_ple, w_ple.astype(BF16), w_ple_gate.astype(BF16),
                             g_final, final_norm)


def kernel(x, p, norm_mix, norm_ffn, norm_ple, norm_final, w_in_ab, ssm_a_re, ssm_a_im, ssm_b_re, ssm_b_im, ssm_c_re, ssm_c_im, ssm_d, ssm_log_step, glu_w, glu_b, w_out_ab, rel_bias, w_in_c, w_out_c, router_w, router_b, w_gate, b_gate, w_up, b_up, w_down, b_down, w_ple, w_ple_gate):
    bsz, seq, d = x.shape
    n = bsz * seq
    depth = p.shape[0]
    assert bsz == SUBLANES
    r = x.reshape(n, d).astype(F32)
    for i in range(depth):
        j = i // 2
        if i % 2 == 0:
            ssm_w = ssm_d.shape[1] * ssm_d.shape[2]
            moba_w = (w_in_ab.shape[2] - ssm_w) // 3
            u, qkv = norm_matmul(r, norm_mix[i], w_in_ab[j].astype(BF16),
                                 (ssm_w, 3 * moba_w), (F32, BF16))
            a_mat, b_mat, c_mat = _s5_params(ssm_a_re[j], ssm_a_im[j], ssm_b_re[j], ssm_b_im[j],
                                             ssm_c_re[j], ssm_c_im[j], ssm_log_step[j])
            u_tb = u.reshape(bsz, seq, ssm_w).transpose(1, 0, 2).reshape(n, ssm_w)
            y_a = s5_mixer(u_tb, a_mat, b_mat, c_mat, ssm_d[j].reshape(1, ssm_w).astype(F32),
                           glu_w[j].astype(BF16), glu_b[j].reshape(1, ssm_w).astype(F32))
            y_a = y_a.reshape(seq, bsz, ssm_w).transpose(1, 0, 2).reshape(n, ssm_w)
            y_b = moba_mixer(qkv.reshape(bsz, seq, 3 * moba_w), rel_bias, moba_w).reshape(n, moba_w)
            w_out = w_out_ab[j].astype(BF16)
            r, h_tiles = matmul_residual(r, norm_ffn[i], [y_a, y_b], [w_out[:ssm_w], w_out[ssm_w:]])
        else:
            q_scale = jnp.where(jnp.arange(3 * d) < d, HEAD_DIM ** -0.5 * math.log2(math.e), 1.0)
            w_in = (w_in_c[j].astype(F32) * q_scale).astype(BF16)
            (qkv,) = norm_matmul(r, norm_mix[i], w_in, (3 * d,), (BF16,))
            y_c = stick_breaking_mixer(qkv.reshape(bsz, seq, 3 * d), d).reshape(n, d)
            r, h_tiles = matmul_residual(r, norm_ffn[i], [y_c], [w_out_c[j].astype(BF16)])
        r = moe_and_embed(r, h_tiles, p[i].reshape(n, -1), norm_ffn[i], router_w[i], router_b[i],
                          i, w_gate, b_gate, w_up, b_up, w_down, b_down,
                          norm_ple[i], w_ple[i], w_ple_gate[i], norm_final, i == depth - 1)
    return r.reshape(bsz, seq, d).astype(x.dtype)
```

```python
import functools
import math

import jax
import jax.numpy as jnp
from jax import lax
from jax.experimental import pallas as pl
from jax.experimental.pallas import tpu as pltpu

F32 = jnp.float32
BF16 = jnp.bfloat16
I32 = jnp.int32

RMS_EPS = 1e-6
HEAD_DIM = 64
LANES = 128
SUBLANES = 8
SSM_GROUP = 16
SSM_STATE = 64
SSM_SLAB_GROUPS = LANES // SSM_GROUP
MOBA_BLOCK = 256
MOBA_TOPK = 3
MOBA_HEADS = 4
REL_BUCKETS = 32
REL_MAX_DIST = 128
SB_BLOCK = 256
SB_HEADS = 4
N_EXPERTS = 32
TOP_K = 4
SWIGLU_LIMIT = 7.0
SWIGLU_ALPHA = 1.702
MOE_ROWS = 256
GATHER_BUFS = 3
DMA_QUEUES = 2
NEG = -1e30
MIB = 1024 * 1024

_NT = (((1,), (1,)), ((), ()))


def _cparams(n_axes, vmem_mib):
    return pltpu.CompilerParams(
        dimension_semantics=("arbitrary",) * n_axes, vmem_limit_bytes=vmem_mib * MIB)


def _rms(x, g):
    ms = jnp.mean(x * x, axis=-1, keepdims=True)
    return x * lax.rsqrt(ms + RMS_EPS) * g


def _dot(a, b):
    return jnp.dot(a, b, preferred_element_type=F32)


def _dot_nt(a, b):
    return lax.dot_general(a, b, _NT, preferred_element_type=F32)


def _norm_matmul_kernel(x_ref, g_ref, w_ref, *o_refs, splits):
    h = _rms(x_ref[...], g_ref[...]).astype(BF16)
    y = _dot(h, w_ref[...])
    off = 0
    for o_ref, s in zip(o_refs, splits):
        o_ref[...] = y[:, off:off + s].astype(o_ref.dtype)
        off += s


def norm_matmul(x, g, w, splits, dtypes, tm=512):
    n, d = x.shape
    nout = w.shape[1]
    return pl.pallas_call(
        functools.partial(_norm_matmul_kernel, splits=splits),
        grid=(n // tm,),
        in_specs=[pl.BlockSpec((tm, d), lambda i: (i, 0)),
                  pl.BlockSpec((1, d), lambda i: (0, 0)),
                  pl.BlockSpec((d, nout), lambda i: (0, 0))],
        out_specs=[pl.BlockSpec((tm, s), lambda i: (i, 0)) for s in splits],
        out_shape=[jax.ShapeDtypeStruct((n, s), dt) for s, dt in zip(splits, dtypes)],
        compiler_params=_cparams(1, 48),
        name="norm_matmul",
    )(x, g.reshape(1, d), w)


def _load_rows_from_tiles(x2_ref, rows, d_tiles):
    return jnp.concatenate([x2_ref[pl.ds(c, rows, stride=d_tiles), :] for c in range(d_tiles)], axis=1)


def _bits(x):
    return lax.bitcast_convert_type(x, jnp.uint32)


def _store_rows_as_bf16_pair_tiles(o3_ref, x):
    for s in range(o3_ref.shape[1]):
        lo = x[:, 2 * s * LANES:(2 * s + 1) * LANES].astype(BF16).astype(F32)
        hi = x[:, (2 * s + 1) * LANES:(2 * s + 2) * LANES].astype(BF16).astype(F32)
        o3_ref[:, s, :] = (_bits(lo) >> 16) | (_bits(hi) & jnp.uint32(0xFFFF0000))


def _load_rows_from_bf16_pair_tiles(x2_ref, rows, pair_tiles):
    chunks = []
    for s in range(pair_tiles):
        word = x2_ref[pl.ds(s, rows, stride=pair_tiles), :]
        chunks.append(lax.bitcast_convert_type(word << 16, F32))
        chunks.append(lax.bitcast_convert_type(word & jnp.uint32(0xFFFF0000), F32))
    return jnp.concatenate(chunks, axis=1).astype(BF16)


def _matmul_residual_kernel(r_ref, g_ref, *refs, n_in):
    acc = r_ref[...]
    for a_ref, w_ref in zip(refs[:n_in], refs[n_in:2 * n_in]):
        acc = acc + _dot(a_ref[...], w_ref[...])
    refs[2 * n_in][...] = acc
    _store_rows_as_bf16_pair_tiles(refs[2 * n_in + 1], _rms(acc, g_ref[...]))


def matmul_residual(r, g_next, a_list, w_list, tm=512):
    n, d = r.shape
    n_in = len(a_list)
    in_specs = [pl.BlockSpec((tm, d), lambda i: (i, 0)), pl.BlockSpec((1, d), lambda i: (0, 0))]
    in_specs += [pl.BlockSpec((tm, a.shape[1]), lambda i: (i, 0)) for a in a_list]
    in_specs += [pl.BlockSpec(w.shape, lambda i: (0, 0)) for w in w_list]
    return pl.pallas_call(
        functools.partial(_matmul_residual_kernel, n_in=n_in),
        grid=(n // tm,),
        in_specs=in_specs,
        out_specs=[pl.BlockSpec((tm, d), lambda i: (i, 0)),
                   pl.BlockSpec((tm, d // (2 * LANES), LANES), lambda i: (i, 0, 0))],
        out_shape=[jax.ShapeDtypeStruct((n, d), F32),
                   jax.ShapeDtypeStruct((n, d // (2 * LANES), LANES), jnp.uint32)],
        compiler_params=_cparams(1, 40),
        name="matmul_residual",
    )(r, g_next.reshape(1, d), *a_list, *w_list)


def _s5_kernel(u_ref, a_ref, b_ref, c_ref, d_ref, gw_ref, gb_ref, o_ref, s_ref, x_ref,
               *, steps, n_slab):
    half = SSM_SLAB_GROUPS * SSM_STATE
    slab = 2 * half

    @pl.when(pl.program_id(0) == 0)
    def _():
        x_ref[...] = jnp.zeros_like(x_ref)

    u = u_ref[...]
    ub = u.astype(BF16)
    for s in range(n_slab):
        s_ref[:, s * slab:(s + 1) * slab] = _dot(ub[:, s * LANES:(s + 1) * LANES], b_ref[s])

    for s in range(n_slab):
        re = slice(s * slab, s * slab + half)
        im = slice(s * slab + half, (s + 1) * slab)
        ar = a_ref[:, re]
        ai = a_ref[:, im]

        def step(t, carry, re=re, im=im, ar=ar, ai=ai):
            xr, xi = carry
            rows = pl.ds(pl.multiple_of(t * SUBLANES, SUBLANES), SUBLANES)
            nr = ar * xr - ai * xi + s_ref[rows, re]
            ni = ar * xi + ai * xr + s_ref[rows, im]
            s_ref[rows, re] = nr
            s_ref[rows, im] = ni
            return nr, ni

        xr, xi = lax.fori_loop(0, steps, step, (x_ref[:, re], x_ref[:, im]), unroll=8)
        x_ref[:, re] = xr
        x_ref[:, im] = xi

    y = jnp.concatenate(
        [_dot(s_ref[:, s * slab:(s + 1) * slab].astype(BF16), c_ref[s]) for s in range(n_slab)],
        axis=1)
    y = y + d_ref[...] * u
    z = jax.nn.gelu(y)
    gate = jax.nn.sigmoid(_dot(z.astype(BF16), gw_ref[...]) + gb_ref[...])
    o_ref[...] = (z * gate).astype(o_ref.dtype)


def s5_mixer(u_tb, a_mat, b_mat, c_mat, d_vec, glu_w, glu_b, steps=64):
    rows, width = u_tb.shape
    n_slab = width // LANES
    n_state = 2 * n_slab * SSM_SLAB_GROUPS * SSM_STATE
    tm = steps * SUBLANES
    return pl.pallas_call(
        functools.partial(_s5_kernel, steps=steps, n_slab=n_slab),
        grid=(rows // tm,),
        in_specs=[pl.BlockSpec((tm, width), lambda i: (i, 0)),
                  pl.BlockSpec(a_mat.shape, lambda i: (0, 0)),
                  pl.BlockSpec(b_mat.shape, lambda i: (0, 0, 0)),
                  pl.BlockSpec(c_mat.shape, lambda i: (0, 0, 0)),
                  pl.BlockSpec((1, width), lambda i: (0, 0)),
                  pl.BlockSpec(glu_w.shape, lambda i: (0, 0)),
                  pl.BlockSpec((1, width), lambda i: (0, 0))],
        out_specs=pl.BlockSpec((tm, width), lambda i: (i, 0)),
        out_shape=jax.ShapeDtypeStruct((rows, width), BF16),
        scratch_shapes=[pltpu.VMEM((tm, n_state), F32), pltpu.VMEM((SUBLANES, n_state), F32)],
        compiler_params=_cparams(1, 40),
        name="s5_mixer",
    )(u_tb, a_mat, b_mat, c_mat, d_vec, glu_w, glu_b)


def _s5_params(a_re, a_im, b_re, b_im, c_re, c_im, log_step):
    n_grp = a_re.shape[0]
    n_slab = n_grp // SSM_SLAB_GROUPS
    lam = lax.complex(a_re.astype(F32), a_im.astype(F32))
    step = jnp.exp(log_step.astype(F32))[:, None]
    lam_bar = jnp.exp(lam * step)
    b_bar = ((lam_bar - 1.0) / lam)[:, :, None] * lax.complex(b_re.astype(F32), b_im.astype(F32))
    eye = jnp.eye(SSM_SLAB_GROUPS, dtype=F32)

    def slabbed(t):
        return t.reshape((n_slab, SSM_SLAB_GROUPS) + t.shape[1:])

    a_mat = jnp.concatenate(
        [slabbed(jnp.real(lam_bar)).reshape(n_slab, -1), slabbed(jnp.imag(lam_bar)).reshape(n_slab, -1)],
        axis=1).reshape(1, -1)
    a_mat = jnp.broadcast_to(a_mat, (SUBLANES, a_mat.shape[1]))
    b_parts = [jnp.einsum('sgnh,gk->sghkn', slabbed(part(b_bar)), eye)
               for part in (jnp.real, jnp.imag)]
    b_mat = jnp.stack(b_parts, axis=3).reshape(n_slab, LANES, -1)
    c_parts = [jnp.einsum('sghn,gk->sgnkh', slabbed(part), eye)
               for part in (c_re.astype(F32), -c_im.astype(F32))]
    c_mat = jnp.stack(c_parts, axis=1).reshape(n_slab, -1, LANES)
    return a_mat, b_mat.astype(BF16), c_mat.astype(BF16)


def _by_head(lane, cols):
    out = cols[-1]
    for h in reversed(range(len(cols) - 1)):
        out = jnp.where(lane // HEAD_DIM == h, cols[h], out)
    return out


def _moba_kernel(q_ref, k_ref, v_ref, tb_ref, o_ref, mean_ref, km_ref, *, n_blk):
    blk = MOBA_BLOCK
    heads = MOBA_HEADS
    width = heads * HEAD_DIM
    rows_all = heads * blk
    qb = pl.program_id(2)

    @pl.when(qb == 0)
    def _():
        mean_ref[...] = jnp.zeros_like(mean_ref)
        for n in range(n_blk):
            kb = k_ref[0, n * blk:(n + 1) * blk, :].astype(F32)
            mean_ref[n:n + 1, :] = jnp.sum(kb, axis=0, keepdims=True) / blk
        mean = mean_ref[...]
        hi = mean.astype(BF16)
        km_ref[:, :width] = hi
        km_ref[:, width:] = (mean - hi.astype(F32)).astype(BF16)

    q = q_ref[0]
    lane = lax.broadcasted_iota(I32, (blk, width), 1)
    q_stack = jnp.concatenate(
        [jnp.where(lane // HEAD_DIM == h, q, jnp.zeros_like(q)) for h in range(heads)], axis=0)

    blane = lax.broadcasted_iota(I32, (rows_all, LANES), 1)
    gate = _dot_nt(jnp.concatenate([q_stack, q_stack], axis=1), km_ref[...])
    gate = jnp.where(blane < qb, gate, -jnp.inf)
    sel = jnp.zeros((rows_all, LANES), F32)
    blane_f = blane.astype(F32)
    for r in range(MOBA_TOPK):
        top = jnp.max(gate, axis=1, keepdims=True)
        idx = jnp.min(jnp.where(gate == top, blane_f, float(LANES)), axis=1, keepdims=True)
        hit = blane_f == idx
        sel = jnp.where(jnp.logical_and(hit, r < qb), 1.0, sel)
        gate = jnp.where(hit, -jnp.inf, gate)

    qs = q_stack * (HEAD_DIM ** -0.5)
    row = lax.broadcasted_iota(I32, (rows_all, blk), 0)
    col = lax.broadcasted_iota(I32, (rows_all, blk), 1)

    def rows_of(n):
        return pl.ds(pl.multiple_of(n * blk, blk), blk)

    def weighted_values(p, n):
        vn = v_ref[0, rows_of(n), :]
        p = p.astype(BF16)
        p_cat = jnp.concatenate([p[h * blk:(h + 1) * blk] for h in range(heads)], axis=1)
        v_stack = jnp.concatenate(
            [jnp.where(lane // HEAD_DIM == h, vn, jnp.zeros_like(vn)) for h in range(heads)], axis=0)
        return _dot(p_cat, v_stack)

    def per_head(x):
        return _by_head(lane, [x[h * blk:(h + 1) * blk] for h in range(heads)])

    s = _dot_nt(qs, k_ref[0, rows_of(qb), :]) + tb_ref[0, 0]
    s = jnp.where(col <= (row & (blk - 1)), s, NEG)
    m = jnp.max(s, axis=1, keepdims=True)
    p = jnp.exp(s - m)
    l = jnp.sum(p, axis=1, keepdims=True)
    acc = weighted_values(p, qb)

    def past_blocks(first, per_trip):
        def body(i, carry):
            m, l, acc = carry
            for u in range(per_trip):
                n = first + per_trip * i + u
                chosen = jnp.sum(jnp.where(blane == n, sel, 0.0), axis=1, keepdims=True) > 0.0
                s = _dot_nt(qs, k_ref[0, rows_of(n), :]) + tb_ref[0, jnp.minimum(qb - n, 2)]
                s = jnp.where(chosen, s, NEG)
                m_new = jnp.maximum(m, jnp.max(s, axis=1, keepdims=True))
                alpha = jnp.exp(m - m_new)
                p = jnp.exp(s - m_new)
                l = alpha * l + jnp.sum(p, axis=1, keepdims=True)
                acc = per_head(alpha) * acc + weighted_values(p, n)
                m = m_new
            return m, l, acc
        return body

    odd = qb & 1
    carry = lax.fori_loop(0, odd, past_blocks(0, 1), (m, l, acc))
    _, l, acc = lax.fori_loop(0, qb // 2, past_blocks(odd, 2), carry)
    o_ref[0] = (acc / per_head(l)).astype(o_ref.dtype)


def _rel_bucket(dist):
    exact = REL_BUCKETS // 2
    n = jnp.maximum(dist, 0)
    nf = jnp.maximum(n, 1).astype(F32)
    log_ratio = jnp.log(nf / exact) / math.log(REL_MAX_DIST / exact)
    large = exact + (log_ratio * (REL_BUCKETS - exact)).astype(I32)
    large = jnp.minimum(large, REL_BUCKETS - 1)
    return jnp.where(n < exact, n, large)


def _moba_bias_tables(rel_bias):
    assert REL_MAX_DIST <= MOBA_BLOCK + 1
    i = jnp.arange(MOBA_BLOCK)[:, None]
    j = jnp.arange(MOBA_BLOCK)[None, :]
    dist = jnp.arange(3)[:, None, None] * MOBA_BLOCK + (i - j)[None]
    onehot = (_rel_bucket(dist)[..., None] == jnp.arange(REL_BUCKETS)).astype(F32)
    return jnp.einsum('oijb,bh->hoij', onehot, rel_bias.astype(F32), precision=lax.Precision.HIGHEST)


def moba_mixer(qkv, rel_bias, width):
    bsz, seq, _ = qkv.shape
    blk = MOBA_BLOCK
    n_blk = seq // blk
    gw = MOBA_HEADS * HEAD_DIM
    n_grp = width // gw
    assert seq % blk == 0 and n_blk <= LANES
    tables = _moba_bias_tables(rel_bias).reshape(n_grp, MOBA_HEADS, 3, blk, blk)
    tables = tables.transpose(0, 2, 1, 3, 4).reshape(n_grp, 3, MOBA_HEADS * blk, blk)
    return pl.pallas_call(
        functools.partial(_moba_kernel, n_blk=n_blk),
        grid=(bsz, n_grp, n_blk),
        in_specs=[pl.BlockSpec((1, blk, gw), lambda b, p, i: (b, i, p)),
                  pl.BlockSpec((1, seq, gw), lambda b, p, i: (b, 0, n_grp + p)),
                  pl.BlockSpec((1, seq, gw), lambda b, p, i: (b, 0, 2 * n_grp + p)),
                  pl.BlockSpec((1, 3, MOBA_HEADS * blk, blk), lambda b, p, i: (p, 0, 0, 0))],
        out_specs=pl.BlockSpec((1, blk, gw), lambda b, p, i: (b, i, p)),
        out_shape=jax.ShapeDtypeStruct((bsz, seq, width), BF16),
        scratch_shapes=[pltpu.VMEM((LANES, gw), F32), pltpu.VMEM((LANES, 2 * gw), BF16)],
        compiler_params=_cparams(3, 48),
        name="moba_mixer",
    )(qkv, qkv, qkv, tables)


def _sb_kernel(q_ref, k_ref, v_ref, tri_ref, o_ref):
    blk = SB_BLOCK
    width = SB_HEADS * HEAD_DIM
    n_sub = blk // LANES
    rows_all = SB_HEADS * blk
    qb = pl.program_id(2)
    q = q_ref[0]
    lane = lax.broadcasted_iota(I32, (blk, width), 1)
    q_stack = jnp.concatenate(
        [jnp.where(lane // HEAD_DIM == h, q, jnp.zeros_like(q)) for h in range(SB_HEADS)],
        axis=0)
    row = lax.broadcasted_iota(I32, (rows_all, blk), 0)
    col = lax.broadcasted_iota(I32, (rows_all, blk), 1)
    past = col < (row & (blk - 1))

    def rows_of(n):
        return pl.ds(pl.multiple_of(n * blk, blk), blk)

    def logits(n):
        return _dot_nt(q_stack, k_ref[0, rows_of(n), :])

    def weights(z, carried, diagonal):
        neg_abs = lax.bitcast_convert_type(
            lax.bitcast_convert_type(z, jnp.uint32) | jnp.uint32(0x80000000), F32)
        drop = jnp.maximum(z, 0.0) + jnp.log2(1.0 + jnp.exp2(neg_abs))
        if diagonal:
            drop = jnp.where(past, drop, 0.0)
        hi32 = lax.bitcast_convert_type(
            lax.bitcast_convert_type(drop, jnp.uint32) & jnp.uint32(0xFFFF0000), F32)
        hi = hi32.astype(BF16)
        lo = (drop - hi32).astype(BF16)
        lhs = jnp.concatenate(
            [jnp.concatenate([hi[:, c * LANES:(c + 1) * LANES], lo[:, c * LANES:(c + 1) * LANES]], axis=1)
             for c in range(n_sub)], axis=0)
        sums = _dot(lhs, tri_ref[...])
        newer = carried
        from_key = [None] * n_sub
        for c in reversed(range(n_sub)):
            within = sums[c * rows_all:(c + 1) * rows_all, :LANES]
            total = sums[c * rows_all:(c + 1) * rows_all, LANES:]
            from_key[c] = within if newer is None else within + newer
            newer = total if newer is None else newer + total
        w = jnp.exp2(z - jnp.concatenate(from_key, axis=1))
        if diagonal:
            w = jnp.where(past, w, 0.0)
        w = w.astype(BF16)
        return jnp.concatenate([w[h * blk:(h + 1) * blk] for h in range(SB_HEADS)], axis=1), newer

    def weighted_values(w_cat, n):
        vn = v_ref[0, rows_of(n), :]
        v_stack = jnp.concatenate(
            [jnp.where(lane // HEAD_DIM == h, vn, jnp.zeros_like(vn)) for h in range(SB_HEADS)], axis=0)
        return _dot(w_cat, v_stack)

    w_cat, carried = weights(logits(qb), None, True)
    acc = weighted_values(w_cat, qb)

    def older_blocks(first, per_trip):
        def body(i, carry):
            acc, carried = carry
            for u in range(per_trip):
                n = first - per_trip * i - u
                w_cat, carried = weights(logits(n), carried, False)
                acc = acc + weighted_values(w_cat, n)
            return acc, carried
        return body

    odd = qb & 1
    acc, carried = lax.fori_loop(0, odd, older_blocks(qb - 1, 1), (acc, carried))
    acc, _ = lax.fori_loop(0, qb // 2, older_blocks(qb - 1 - odd, 2), (acc, carried))
    o_ref[0] = acc.astype(o_ref.dtype)


def stick_breaking_mixer(qkv, width):
    bsz, seq, _ = qkv.shape
    blk = SB_BLOCK
    gw = SB_HEADS * HEAD_DIM
    n_grp = width // gw
    tri = (jnp.arange(LANES)[:, None] >= jnp.arange(LANES)[None, :]).astype(BF16)
    tri = jnp.concatenate([tri, jnp.ones((LANES, LANES), BF16)], axis=1)
    tri = jnp.concatenate([tri, tri], axis=0)
    return pl.pallas_call(
        _sb_kernel,
        grid=(bsz, n_grp, seq // blk),
        in_specs=[pl.BlockSpec((1, blk, gw), lambda b, p, i: (b, i, p)),
                  pl.BlockSpec((1, seq, gw), lambda b, p, i: (b, 0, n_grp + p)),
                  pl.BlockSpec((1, seq, gw), lambda b, p, i: (b, 0, 2 * n_grp + p)),
                  pl.BlockSpec(tri.shape, lambda b, p, i: (0, 0))],
        out_specs=pl.BlockSpec((1, blk, gw), lambda b, p, i: (b, i, p)),
        out_shape=jax.ShapeDtypeStruct((bsz, seq, width), BF16),
        compiler_params=_cparams(3, 48),
        name="stick_breaking",
    )(qkv, qkv, qkv, tri)


def _router_kernel(r_ref, g_ref, w_ref, b_ref, idx_ref, gate_ref, rank_ref, cnt_ref, run_ref):
    tm = r_ref.shape[0]

    @pl.when(pl.program_id(0) == 0)
    def _():
        run_ref[...] = jnp.zeros_like(run_ref)

    h = _rms(r_ref[...], g_ref[...])
    h_hi = h.astype(BF16)
    h_lo = (h - h_hi.astype(F32)).astype(BF16)
    logits = _dot(jnp.concatenate([h_hi, h_hi, h_lo], axis=1), w_ref[...]) + b_ref[...]
    lane = lax.broadcasted_iota(I32, (tm, LANES), 1)
    lane_f = lane.astype(F32)
    tops, hits = [], []
    for _ in range(TOP_K):
        top = jnp.max(logits, axis=1, keepdims=True)
        idx = jnp.min(jnp.where(logits == top, lane_f, float(LANES)), axis=1, keepdims=True)
        hit = lane_f == idx
        logits = jnp.where(hit, -jnp.inf, logits)
        tops.append(top)
        hits.append(hit)
    exps = [jnp.exp(t - tops[0]) for t in tops]
    denom = exps[0]
    for e in exps[1:]:
        denom = denom + e

    member = jnp.zeros((tm, LANES), F32)
    for hit in hits:
        member = jnp.where(hit, 1.0, member)
    before = (lax.broadcasted_iota(I32, (tm, tm), 1) < lax.broadcasted_iota(I32, (tm, tm), 0))
    ahead = _dot(before.astype(BF16), member.astype(BF16)) + run_ref[...]

    idx_out = jnp.zeros((tm, LANES), F32)
    gate_out = jnp.zeros((tm, LANES), F32)
    rank_out = jnp.zeros((tm, LANES), F32)
    for k in range(TOP_K):
        idx_k = jnp.sum(jnp.where(hits[k], lane_f, 0.0), axis=1, keepdims=True)
        rank_k = jnp.sum(jnp.where(hits[k], ahead, 0.0), axis=1, keepdims=True)
        idx_out = jnp.where(lane == k, idx_k, idx_out)
        gate_out = jnp.where(lane == k, exps[k] / denom, gate_out)
        rank_out = jnp.where(lane == k, rank_k, rank_out)
    idx_ref[...] = idx_out.astype(I32)
    gate_ref[...] = gate_out
    rank_ref[...] = rank_out.astype(I32)
    run_ref[...] = run_ref[...] + jnp.sum(member, axis=0, keepdims=True)
    cnt_ref[...] = run_ref[...].astype(I32)


def moe_router(r, g, router_w, router_b, tm=512):
    n, d = r.shape
    n_exp = router_w.shape[1]
    w_pad = jnp.pad(router_w.astype(F32), ((0, 0), (0, LANES - n_exp)))
    w_hi = w_pad.astype(BF16)
    w_lo = (w_pad - w_hi.astype(F32)).astype(BF16)
    w_split = jnp.concatenate([w_hi, w_lo, w_hi], axis=0)
    b_pad = jnp.pad(router_b.astype(F32).reshape(1, n_exp), ((0, 0), (0, LANES - n_exp)),
                    constant_values=-jnp.inf)
    tile = pl.BlockSpec((tm, LANES), lambda i: (i, 0))
    idx, gates, rank, counts = pl.pallas_call(
        _router_kernel,
        grid=(n // tm,),
        in_specs=[pl.BlockSpec((tm, d), lambda i: (i, 0)),
                  pl.BlockSpec((1, d), lambda i: (0, 0)),
                  pl.BlockSpec((3 * d, LANES), lambda i: (0, 0)),
                  pl.BlockSpec((1, LANES), lambda i: (0, 0))],
        out_specs=[tile, tile, tile, pl.BlockSpec((1, LANES), lambda i: (0, 0))],
        out_shape=[jax.ShapeDtypeStruct((n, LANES), I32), jax.ShapeDtypeStruct((n, LANES), F32),
                   jax.ShapeDtypeStruct((n, LANES), I32), jax.ShapeDtypeStruct((1, LANES), I32)],
        scratch_shapes=[pltpu.VMEM((1, LANES), F32)],
        compiler_params=_cparams(1, 32),
        name="moe_router",
    )(r, g.reshape(1, d), w_split, b_pad)
    return idx[:, :TOP_K], gates, rank[:, :TOP_K], counts[0, :n_exp]


def _expert_kernel(be_ref, na_ref, tok0_ref, tok1_ref, tok_ahead_ref, h_hbm, wg_ref, bg_ref, wu_ref, bu_ref,
                   wd_ref, bd_ref, ys_hbm, buf_ref, wg_bf, wu_bf, wd_bf, y_buf, sem, out_sem):
    b = pl.program_id(0)
    last = pl.num_programs(0) - 1
    active = b < na_ref[0]
    out_slot = lax.rem(b, 2)
    cur = lax.rem(b, GATHER_BUFS)
    ahead = lax.rem(b + GATHER_BUFS - 1, GATHER_BUFS)
    tiles = MOE_ROWS // SUBLANES
    pair_tiles = h_hbm.shape[1]
    d_tiles = 2 * pair_tiles

    def row_copy(tok, buf, j):
        return pltpu.make_async_copy(
            h_hbm.at[tok], buf_ref.at[buf, pl.ds(j * pair_tiles, pair_tiles)], sem.at[buf])

    def start_gather_loop(toks, buf):
        def start(t, c):
            for sub in range(SUBLANES):
                j = t * SUBLANES + sub
                row_copy(toks[0, 0, j], buf, j).start()
            return c
        lax.fori_loop(0, tiles, start, 0)

    def wait_gather(buf):
        def wait(t, c):
            for sub in range(SUBLANES):
                row_copy(0, buf, 0).wait()
            return c
        lax.fori_loop(0, tiles, wait, 0)

    def out_copy(slot, blk, c):
        return pltpu.make_async_copy(
            y_buf.at[slot, :, pl.ds(c * LANES, LANES)],
            ys_hbm.at[pl.ds(blk * MOE_ROWS, MOE_ROWS), c, :], out_sem.at[slot])

    def wait_out(slot):
        for c in range(d_tiles):
            out_copy(slot, 0, c).wait()

    @pl.when(b == 0)
    def _():
        start_gather_loop(tok0_ref, 0)
        start_gather_loop(tok1_ref, 1)

    @pl.when(b >= 2)
    def _():
        wait_out(out_slot)

    @pl.when(jnp.logical_or(b == 0, be_ref[b] != be_ref[jnp.maximum(b - 1, 0)]))
    def _():
        wg_bf[...] = wg_ref[0].astype(BF16)
        wu_bf[...] = wu_ref[0].astype(BF16)
        wd_bf[...] = wd_ref[0].astype(BF16)

    wait_gather(cur)

    @pl.when(active)
    def _():
        h = _load_rows_from_bf16_pair_tiles(buf_ref.at[cur], MOE_ROWS, pair_tiles)
        gate = jnp.minimum(_dot(h, wg_bf[...]) + bg_ref[0], SWIGLU_LIMIT)
        up = jnp.clip(_dot(h, wu_bf[...]) + bu_ref[0], -SWIGLU_LIMIT, SWIGLU_LIMIT)
        act = (gate * jax.nn.sigmoid(SWIGLU_ALPHA * gate) * (up + 1.0)).astype(BF16)
        for j in range(MOE_ROWS):
            row_copy(tok_ahead_ref[0, 0, j], ahead, j).start(priority=j % DMA_QUEUES)
        y_buf[out_slot] = _dot(act, wd_bf[...]) + bd_ref[0]

    @pl.when(jnp.logical_not(active))
    def _():
        start_gather_loop(tok_ahead_ref, ahead)
        y_buf[out_slot] = jnp.zeros((MOE_ROWS, d_tiles * LANES), F32)

    for c in range(d_tiles):
        out_copy(out_slot, b, c).start()

    @pl.when(b == last)
    def _():
        for back in range(1, GATHER_BUFS):
            wait_gather(lax.rem(b + back, GATHER_BUFS))
        wait_out(1 - out_slot)
        wait_out(out_slot)


def moe_experts(h_tiles, slot_tok, block_expert, n_active, layer, w_gate, b_gate, w_up, b_up, w_down, b_down):
    n, pair_tiles, _ = h_tiles.shape
    d_tiles = 2 * pair_tiles
    d = d_tiles * LANES
    n_layers, n_exp, _, d_ff = w_gate.shape
    n_blocks = block_expert.shape[0]
    assert n_blocks >= GATHER_BUFS
    toks = slot_tok.reshape(n_blocks, 1, MOE_ROWS)

    def expert4(i, be, na):
        return (layer, be[i], 0, 0)

    def tok_block(index):
        return pl.BlockSpec((1, 1, MOE_ROWS), index, memory_space=pltpu.SMEM)

    grid_spec = pltpu.PrefetchScalarGridSpec(
        num_scalar_prefetch=2,
        grid=(n_blocks,),
        in_specs=[tok_block(lambda i, be, na: (0, 0, 0)),
                  tok_block(lambda i, be, na: (1, 0, 0)),
                  tok_block(lambda i, be, na: (jnp.minimum(i + GATHER_BUFS - 1, n_blocks - 1), 0, 0)),
                  pl.BlockSpec(memory_space=pl.ANY),
                  pl.BlockSpec((None, 1, d, d_ff), expert4), pl.BlockSpec((None, 1, 1, d_ff), expert4),
                  pl.BlockSpec((None, 1, d, d_ff), expert4), pl.BlockSpec((None, 1, 1, d_ff), expert4),
                  pl.BlockSpec((None, 1, d_ff, d), expert4), pl.BlockSpec((None, 1, 1, d), expert4)],
        out_specs=pl.BlockSpec(memory_space=pl.ANY),
        scratch_shapes=[pltpu.VMEM((GATHER_BUFS, MOE_ROWS * pair_tiles, LANES), jnp.uint32),
                        pltpu.VMEM((d, d_ff), BF16), pltpu.VMEM((d, d_ff), BF16), pltpu.VMEM((d_ff, d), BF16),
                        pltpu.VMEM((2, MOE_ROWS, d), F32),
                        pltpu.SemaphoreType.DMA((GATHER_BUFS,)), pltpu.SemaphoreType.DMA((2,))])
    return pl.pallas_call(
        _expert_kernel,
        grid_spec=grid_spec,
        out_shape=jax.ShapeDtypeStruct((n_blocks * MOE_ROWS, d_tiles, LANES), F32),
        compiler_params=_cparams(1, 56),
        name="moe_experts",
    )(block_expert, n_active, toks, toks, toks, h_tiles,
      w_gate, b_gate.reshape(n_layers, n_exp, 1, d_ff), w_up, b_up.reshape(n_layers, n_exp, 1, d_ff),
      w_down, b_down.reshape(n_layers, n_exp, 1, d))


def _combine_kernel(dest0_ref, dest1_ref, dest_ahead_ref, ys_hbm, gate_ref, r_ref, p_ref, gp_ref, wp_ref,
                    wpg_ref, gf_ref, o_ref, buf_ref, sem, *, tile, final_norm):
    i = pl.program_id(0)
    last = pl.num_programs(0) - 1
    cur = lax.rem(i, GATHER_BUFS)
    ahead = lax.rem(i + GATHER_BUFS - 1, GATHER_BUFS)
    tiles = tile // SUBLANES
    d_tiles = ys_hbm.shape[1]

    def row_copy(slot, buf, k, j):
        return pltpu.make_async_copy(
            ys_hbm.at[slot], buf_ref.at[buf, k, pl.ds(j * d_tiles, d_tiles)], sem.at[buf])

    def start_gather_loop(dests, buf):
        def start(t, c):
            for sub in range(SUBLANES):
                for k in range(TOP_K):
                    j = t * SUBLANES + sub
                    row_copy(dests[0, 0, j * TOP_K + k], buf, k, j).start()
            return c
        lax.fori_loop(0, tiles, start, 0)

    def wait_gather(buf):
        def wait(t, c):
            for sub in range(SUBLANES):
                for k in range(TOP_K):
                    row_copy(0, buf, k, 0).wait()
            return c
        lax.fori_loop(0, tiles, wait, 0)

    @pl.when(i == 0)
    def _():
        start_gather_loop(dest0_ref, 0)
        start_gather_loop(dest1_ref, 1)

    wait_gather(cur)
    gates = gate_ref[...]
    r = r_ref[...]
    for k in range(TOP_K):
        r = r + _load_rows_from_tiles(buf_ref.at[cur, k], tile, d_tiles) * gates[:, k:k + 1]
    h = _rms(r, gp_ref[...]).astype(BF16)
    gate = jax.nn.sigmoid(_dot(h, wpg_ref[...]))
    r = r + _dot(p_ref[...].astype(BF16), wp_ref[...]) * gate
    if final_norm:
        r = _rms(r, gf_ref[...])
    o_ref[...] = r
    for j in range(tile):
        for k in range(TOP_K):
            row_copy(dest_ahead_ref[0, 0, j * TOP_K + k], ahead, k, j).start(priority=k % DMA_QUEUES)

    @pl.when(i == last)
    def _():
        for back in range(1, GATHER_BUFS):
            wait_gather(lax.rem(i + back, GATHER_BUFS))


def moe_combine_embed(ys, dest, gates, r, p, g_ple, w_ple, w_ple_gate, g_final, final_norm, tile=256):
    n, d = r.shape
    pd = p.shape[1]
    n_tiles = n // tile
    assert n_tiles >= GATHER_BUFS
    dest_tiles = dest.reshape(n_tiles, 1, tile * TOP_K)

    def dest_block(index):
        return pl.BlockSpec((1, 1, tile * TOP_K), index, memory_space=pltpu.SMEM)

    return pl.pallas_call(
        functools.partial(_combine_kernel, tile=tile, final_norm=final_norm),
        grid=(n_tiles,),
        in_specs=[dest_block(lambda i: (0, 0, 0)),
                  dest_block(lambda i: (1, 0, 0)),
                  dest_block(lambda i: (jnp.minimum(i + GATHER_BUFS - 1, n_tiles - 1), 0, 0)),
                  pl.BlockSpec(memory_space=pl.ANY),
                  pl.BlockSpec((tile, LANES), lambda i: (i, 0)),
                  pl.BlockSpec((tile, d), lambda i: (i, 0)),
                  pl.BlockSpec((tile, pd), lambda i: (i, 0)),
                  pl.BlockSpec((1, d), lambda i: (0, 0)),
                  pl.BlockSpec((pd, d), lambda i: (0, 0)),
                  pl.BlockSpec((d, d), lambda i: (0, 0)),
                  pl.BlockSpec((1, d), lambda i: (0, 0))],
        out_specs=pl.BlockSpec((tile, d), lambda i: (i, 0)),
        out_shape=jax.ShapeDtypeStruct((n, d), F32),
        scratch_shapes=[pltpu.VMEM((GATHER_BUFS, TOP_K, tile * (d // LANES), LANES), F32),
                        pltpu.SemaphoreType.DMA((GATHER_BUFS,))],
        compiler_params=_cparams(1, 48),
        name="moe_combine_embed",
    )(dest_tiles, dest_tiles, dest_tiles, ys, gates, r, p, g_ple.reshape(1, d), w_ple, w_ple_gate,
      g_final.reshape(1, d))


def _slot_layout(idx, rank, counts, n_blocks):
    n_exp = counts.shape[0]
    padded = (counts + MOE_ROWS - 1) // MOE_ROWS * MOE_ROWS
    pad_end = jnp.cumsum(padded)
    pad_start = pad_end - padded
    cnt_start = jnp.cumsum(counts) - counts
    start_of = jnp.sum(jnp.where(idx[..., None] == jnp.arange(n_exp), pad_start, 0), axis=-1)
    dest = (start_of + rank).astype(I32).reshape(-1)
    block_first_row = jnp.arange(n_blocks) * MOE_ROWS
    block_expert = jnp.minimum(
        jnp.sum(pad_end[None, :] <= block_first_row[:, None], axis=1), n_exp - 1).astype(I32)
    sorted_tok = (jnp.argsort(dest) // TOP_K).astype(I32)
    in_expert = block_first_row[:, None] + jnp.arange(MOE_ROWS)[None, :] - pad_start[block_expert][:, None]
    real = in_expert < counts[block_expert][:, None]
    compact = jnp.clip(in_expert + cnt_start[block_expert][:, None], 0, dest.shape[0] - 1)
    slot_tok = jnp.where(real, sorted_tok[compact], 0).astype(I32)
    n_active = (pad_end[-1:] // MOE_ROWS).astype(I32)
    return dest, slot_tok, block_expert, n_active


def moe_and_embed(r, h_tiles, p, g_ffn, router_w, router_b, layer, w_gate, b_gate, w_up, b_up, w_down, b_down,
                  g_ple, w_ple, w_ple_gate, g_final, final_norm):
    n, _ = r.shape
    n_blocks = n * TOP_K // MOE_ROWS + router_w.shape[1]
    idx, gates, rank, counts = moe_router(r, g_ffn, router_w, router_b)
    dest, slot_tok, block_expert, n_active = _slot_layout(idx, rank, counts, n_blocks)
    ys = moe_experts(h_tiles, slot_tok, block_expert, n_active, layer, w_gate.astype(F32), b_gate.astype(F32),
                     w_up.astype(F32), b_up.astype(F32), w_down.astype(F32), b_down.astype(F32))
    return moe_combine_embed(ys, dest, gates, r, p, g_ple, w_ple.astype(BF16), w_ple_gate.astype(BF16),
                             g_final, final_norm)


def kernel(x, p, norm_mix, norm_ffn, norm_ple, norm_final, w_in_ab, ssm_a_re, ssm_a_im, ssm_b_re, ssm_b_im, ssm_c_re, ssm_c_im, ssm_d, ssm_log_step, glu_w, glu_b, w_out_ab, rel_bias, w_in_c, w_out_c, router_w, router_b, w_gate, b_gate, w_up, b_up, w_down, b_down, w_ple, w_ple_gate):
    bsz, seq, d = x.shape
    n = bsz * seq
    depth = p.shape[0]
    assert bsz == SUBLANES
    r = x.reshape(n, d).astype(F32)
    for i in range(depth):
        j = i // 2
        if i % 2 == 0:
            ssm_w = ssm_d.shape[1] * ssm_d.shape[2]
            moba_w = (w_in_ab.shape[2] - ssm_w) // 3
            u, qkv = norm_matmul(r, norm_mix[i], w_in_ab[j].astype(BF16),
                                 (ssm_w, 3 * moba_w), (F32, BF16))
            a_mat, b_mat, c_mat = _s5_params(ssm_a_re[j], ssm_a_im[j], ssm_b_re[j], ssm_b_im[j],
                                             ssm_c_re[j], ssm_c_im[j], ssm_log_step[j])
            u_tb = u.reshape(bsz, seq, ssm_w).transpose(1, 0, 2).reshape(n, ssm_w)
            y_a = s5_mixer(u_tb, a_mat, b_mat, c_mat, ssm_d[j].reshape(1, ssm_w).astype(F32),
                           glu_w[j].astype(BF16), glu_b[j].reshape(1, ssm_w).astype(F32))
            y_a = y_a.reshape(seq, bsz, ssm_w).transpose(1, 0, 2).reshape(n, ssm_w)
            y_b = moba_mixer(qkv.reshape(bsz, seq, 3 * moba_w), rel_bias, moba_w).reshape(n, moba_w)
            w_out = w_out_ab[j].astype(BF16)
            r, h_tiles = matmul_residual(r, norm_ffn[i], [y_a, y_b], [w_out[:ssm_w], w_out[ssm_w:]])
        else:
            q_scale = jnp.where(jnp.arange(3 * d) < d, HEAD_DIM ** -0.5 * math.log2(math.e), 1.0)
            w_in = (w_in_c[j].astype(F32) * q_scale).astype(BF16)
            (qkv,) = norm_matmul(r, norm_mix[i], w_in, (3 * d,), (BF16,))
            y_c = stick_breaking_mixer(qkv.reshape(bsz, seq, 3 * d), d).reshape(n, d)
            r, h_tiles = matmul_residual(r, norm_ffn[i], [y_c], [w_out_c[j].astype(BF16)])
        r = moe_and_embed(r, h_tiles, p[i].reshape(n, -1), norm_ffn[i], router_w[i], router_b[i],
                          i, w_gate, b_gate, w_up, b_up, w_down, b_down,
                          norm_ple[i], w_ple[i], w_ple_gate[i], norm_final, i == depth - 1)
    return r.reshape(bsz, seq, d).astype(x.dtype)
```

```python
import functools
import math

import jax
import jax.numpy as jnp
from jax import lax
from jax.experimental import pallas as pl
from jax.experimental.pallas import tpu as pltpu

F32 = jnp.float32
BF16 = jnp.bfloat16
I32 = jnp.int32

RMS_EPS = 1e-6
HEAD_DIM = 64
LANES = 128
SUBLANES = 8
SSM_GROUP = 16
SSM_STATE = 64
SSM_SLAB_GROUPS = LANES // SSM_GROUP
MOBA_BLOCK = 256
MOBA_TOPK = 3
MOBA_HEADS = 4
REL_BUCKETS = 32
REL_MAX_DIST = 128
SB_BLOCK = 256
SB_HEADS = 4
N_EXPERTS = 32
TOP_K = 4
SWIGLU_LIMIT = 7.0
SWIGLU_ALPHA = 1.702
MOE_ROWS = 256
GATHER_BUFS = 3
DMA_QUEUES = 2
NEG = -1e30
MIB = 1024 * 1024

_NT = (((1,), (1,)), ((), ()))


def _cparams(n_axes, vmem_mib):
    return pltpu.CompilerParams(
        dimension_semantics=("arbitrary",) * n_axes, vmem_limit_bytes=vmem_mib * MIB)


def _rms(x, g):
    ms = jnp.mean(x * x, axis=-1, keepdims=True)
    return x * lax.rsqrt(ms + RMS_EPS) * g


def _dot(a, b):
    return jnp.dot(a, b, preferred_element_type=F32)


def _dot_nt(a, b):
    return lax.dot_general(a, b, _NT, preferred_element_type=F32)


def _norm_matmul_kernel(x_ref, g_ref, w_ref, *o_refs, splits):
    h = _rms(x_ref[...], g_ref[...]).astype(BF16)
    y = _dot(h, w_ref[...])
    off = 0
    for o_ref, s in zip(o_refs, splits):
        o_ref[...] = y[:, off:off + s].astype(o_ref.dtype)
        off += s


def norm_matmul(x, g, w, splits, dtypes, tm=512):
    n, d = x.shape
    nout = w.shape[1]
    return pl.pallas_call(
        functools.partial(_norm_matmul_kernel, splits=splits),
        grid=(n // tm,),
        in_specs=[pl.BlockSpec((tm, d), lambda i: (i, 0)),
                  pl.BlockSpec((1, d), lambda i: (0, 0)),
                  pl.BlockSpec((d, nout), lambda i: (0, 0))],
        out_specs=[pl.BlockSpec((tm, s), lambda i: (i, 0)) for s in splits],
        out_shape=[jax.ShapeDtypeStruct((n, s), dt) for s, dt in zip(splits, dtypes)],
        compiler_params=_cparams(1, 48),
        name="norm_matmul",
    )(x, g.reshape(1, d), w)


def _load_rows_from_tiles(x2_ref, rows, d_tiles):
    return jnp.concatenate([x2_ref[pl.ds(c, rows, stride=d_tiles), :] for c in range(d_tiles)], axis=1)


def _bits(x):
    return lax.bitcast_convert_type(x, jnp.uint32)


def _store_rows_as_bf16_pair_tiles(o3_ref, x):
    for s in range(o3_ref.shape[1]):
        lo = x[:, 2 * s * LANES:(2 * s + 1) * LANES].astype(BF16).astype(F32)
        hi = x[:, (2 * s + 1) * LANES:(2 * s + 2) * LANES].astype(BF16).astype(F32)
        o3_ref[:, s, :] = (_bits(lo) >> 16) | (_bits(hi) & jnp.uint32(0xFFFF0000))


def _load_rows_from_bf16_pair_tiles(x2_ref, rows, pair_tiles):
    chunks = []
    for s in range(pair_tiles):
        word = x2_ref[pl.ds(s, rows, stride=pair_tiles), :]
        chunks.append(lax.bitcast_convert_type(word << 16, F32))
        chunks.append(lax.bitcast_convert_type(word & jnp.uint32(0xFFFF0000), F32))
    return jnp.concatenate(chunks, axis=1).astype(BF16)


def _matmul_residual_kernel(r_ref, g_ref, rw_ref, rb_ref, *refs, n_in):
    acc = r_ref[...]
    for a_ref, w_ref in zip(refs[:n_in], refs[n_in:2 * n_in]):
        acc = acc + _dot(a_ref[...], w_ref[...])
    o_ref, h3_ref, idx_ref, gate_ref, rank_ref, cnt_ref, run_ref = refs[2 * n_in:]
    o_ref[...] = acc
    h = _rms(acc, g_ref[...])
    _store_rows_as_bf16_pair_tiles(h3_ref, h)
    _route(h, rw_ref, rb_ref, idx_ref, gate_ref, rank_ref, cnt_ref, run_ref)


def matmul_residual(r, g_next, router_w, router_b, a_list, w_list, tm=512):
    n, d = r.shape
    n_in = len(a_list)
    n_exp = router_w.shape[1]
    w_split, b_pad = _router_operands(router_w, router_b)
    in_specs = [pl.BlockSpec((tm, d), lambda i: (i, 0)), pl.BlockSpec((1, d), lambda i: (0, 0)),
                pl.BlockSpec((3 * d, LANES), lambda i: (0, 0)), pl.BlockSpec((1, LANES), lambda i: (0, 0))]
    in_specs += [pl.BlockSpec((tm, a.shape[1]), lambda i: (i, 0)) for a in a_list]
    in_specs += [pl.BlockSpec(w.shape, lambda i: (0, 0)) for w in w_list]
    tile = pl.BlockSpec((tm, LANES), lambda i: (i, 0))
    r_new, h_tiles, idx, gates, rank, counts = pl.pallas_call(
        functools.partial(_matmul_residual_kernel, n_in=n_in),
        grid=(n // tm,),
        in_specs=in_specs,
        out_specs=[pl.BlockSpec((tm, d), lambda i: (i, 0)),
                   pl.BlockSpec((tm, d // (2 * LANES), LANES), lambda i: (i, 0, 0)),
                   tile, tile, tile, pl.BlockSpec((1, LANES), lambda i: (0, 0))],
        out_shape=[jax.ShapeDtypeStruct((n, d), F32),
                   jax.ShapeDtypeStruct((n, d // (2 * LANES), LANES), jnp.uint32),
                   jax.ShapeDtypeStruct((n, LANES), I32), jax.ShapeDtypeStruct((n, LANES), F32),
                   jax.ShapeDtypeStruct((n, LANES), I32), jax.ShapeDtypeStruct((1, LANES), I32)],
        scratch_shapes=[pltpu.VMEM((1, LANES), F32)],
        compiler_params=_cparams(1, 48),
        name="matmul_residual",
    )(r, g_next.reshape(1, d), w_split, b_pad, *a_list, *w_list)
    return r_new, h_tiles, (idx[:, :TOP_K], gates, rank[:, :TOP_K], counts[0, :n_exp])


def _s5_kernel(u_ref, a_ref, b_ref, c_ref, d_ref, gw_ref, gb_ref, o_ref, s_ref, x_ref,
               *, steps, n_slab):
    half = SSM_SLAB_GROUPS * SSM_STATE
    slab = 2 * half

    @pl.when(pl.program_id(0) == 0)
    def _():
        x_ref[...] = jnp.zeros_like(x_ref)

    u = u_ref[...]
    ub = u.astype(BF16)
    for s in range(n_slab):
        s_ref[:, s * slab:(s + 1) * slab] = _dot(ub[:, s * LANES:(s + 1) * LANES], b_ref[s])

    for s in range(n_slab):
        re = slice(s * slab, s * slab + half)
        im = slice(s * slab + half, (s + 1) * slab)
        ar = a_ref[:, re]
        ai = a_ref[:, im]

        def step(t, carry, re=re, im=im, ar=ar, ai=ai):
            xr, xi = carry
            rows = pl.ds(pl.multiple_of(t * SUBLANES, SUBLANES), SUBLANES)
            nr = ar * xr - ai * xi + s_ref[rows, re]
            ni = ar * xi + ai * xr + s_ref[rows, im]
            s_ref[rows, re] = nr
            s_ref[rows, im] = ni
            return nr, ni

        xr, xi = lax.fori_loop(0, steps, step, (x_ref[:, re], x_ref[:, im]), unroll=8)
        x_ref[:, re] = xr
        x_ref[:, im] = xi

    y = jnp.concatenate(
        [_dot(s_ref[:, s * slab:(s + 1) * slab].astype(BF16), c_ref[s]) for s in range(n_slab)],
        axis=1)
    y = y + d_ref[...] * u
    z = jax.nn.gelu(y)
    gate = jax.nn.sigmoid(_dot(z.astype(BF16), gw_ref[...]) + gb_ref[...])
    o_ref[...] = (z * gate).astype(o_ref.dtype)


def s5_mixer(u_tb, a_mat, b_mat, c_mat, d_vec, glu_w, glu_b, steps=64):
    rows, width = u_tb.shape
    n_slab = width // LANES
    n_state = 2 * n_slab * SSM_SLAB_GROUPS * SSM_STATE
    tm = steps * SUBLANES
    return pl.pallas_call(
        functools.partial(_s5_kernel, steps=steps, n_slab=n_slab),
        grid=(rows // tm,),
        in_specs=[pl.BlockSpec((tm, width), lambda i: (i, 0)),
                  pl.BlockSpec(a_mat.shape, lambda i: (0, 0)),
                  pl.BlockSpec(b_mat.shape, lambda i: (0, 0, 0)),
                  pl.BlockSpec(c_mat.shape, lambda i: (0, 0, 0)),
                  pl.BlockSpec((1, width), lambda i: (0, 0)),
                  pl.BlockSpec(glu_w.shape, lambda i: (0, 0)),
                  pl.BlockSpec((1, width), lambda i: (0, 0))],
        out_specs=pl.BlockSpec((tm, width), lambda i: (i, 0)),
        out_shape=jax.ShapeDtypeStruct((rows, width), BF16),
        scratch_shapes=[pltpu.VMEM((tm, n_state), F32), pltpu.VMEM((SUBLANES, n_state), F32)],
        compiler_params=_cparams(1, 40),
        name="s5_mixer",
    )(u_tb, a_mat, b_mat, c_mat, d_vec, glu_w, glu_b)


def _s5_params(a_re, a_im, b_re, b_im, c_re, c_im, log_step):
    n_grp = a_re.shape[0]
    n_slab = n_grp // SSM_SLAB_GROUPS
    lam = lax.complex(a_re.astype(F32), a_im.astype(F32))
    step = jnp.exp(log_step.astype(F32))[:, None]
    lam_bar = jnp.exp(lam * step)
    b_bar = ((lam_bar - 1.0) / lam)[:, :, None] * lax.complex(b_re.astype(F32), b_im.astype(F32))
    eye = jnp.eye(SSM_SLAB_GROUPS, dtype=F32)

    def slabbed(t):
        return t.reshape((n_slab, SSM_SLAB_GROUPS) + t.shape[1:])

    a_mat = jnp.concatenate(
        [slabbed(jnp.real(lam_bar)).reshape(n_slab, -1), slabbed(jnp.imag(lam_bar)).reshape(n_slab, -1)],
        axis=1).reshape(1, -1)
    a_mat = jnp.broadcast_to(a_mat, (SUBLANES, a_mat.shape[1]))
    b_parts = [jnp.einsum('sgnh,gk->sghkn', slabbed(part(b_bar)), eye)
               for part in (jnp.real, jnp.imag)]
    b_mat = jnp.stack(b_parts, axis=3).reshape(n_slab, LANES, -1)
    c_parts = [jnp.einsum('sghn,gk->sgnkh', slabbed(part), eye)
               for part in (c_re.astype(F32), -c_im.astype(F32))]
    c_mat = jnp.stack(c_parts, axis=1).reshape(n_slab, -1, LANES)
    return a_mat, b_mat.astype(BF16), c_mat.astype(BF16)


def _by_head(lane, cols):
    out = cols[-1]
    for h in reversed(range(len(cols) - 1)):
        out = jnp.where(lane // HEAD_DIM == h, cols[h], out)
    return out


def _moba_kernel(q_ref, k_ref, v_ref, tb_ref, o_ref, mean_ref, km_ref, *, n_blk):
    blk = MOBA_BLOCK
    heads = MOBA_HEADS
    width = heads * HEAD_DIM
    rows_all = heads * blk
    qb = pl.program_id(2)

    @pl.when(qb == 0)
    def _():
        mean_ref[...] = jnp.zeros_like(mean_ref)
        for n in range(n_blk):
            kb = k_ref[0, n * blk:(n + 1) * blk, :].astype(F32)
            mean_ref[n:n + 1, :] = jnp.sum(kb, axis=0, keepdims=True) / blk
        mean = mean_ref[...]
        hi = mean.astype(BF16)
        km_ref[:, :width] = hi
        km_ref[:, width:] = (mean - hi.astype(F32)).astype(BF16)

    q = q_ref[0]
    lane = lax.broadcasted_iota(I32, (blk, width), 1)
    q_stack = jnp.concatenate(
        [jnp.where(lane // HEAD_DIM == h, q, jnp.zeros_like(q)) for h in range(heads)], axis=0)

    blane = lax.broadcasted_iota(I32, (rows_all, LANES), 1)
    gate = _dot_nt(jnp.concatenate([q_stack, q_stack], axis=1), km_ref[...])
    gate = jnp.where(blane < qb, gate, -jnp.inf)
    sel = jnp.zeros((rows_all, LANES), F32)
    blane_f = blane.astype(F32)
    for r in range(MOBA_TOPK):
        top = jnp.max(gate, axis=1, keepdims=True)
        idx = jnp.min(jnp.where(gate == top, blane_f, float(LANES)), axis=1, keepdims=True)
        hit = blane_f == idx
        sel = jnp.where(jnp.logical_and(hit, r < qb), 1.0, sel)
        gate = jnp.where(hit, -jnp.inf, gate)

    qs = q_stack * (HEAD_DIM ** -0.5)
    row = lax.broadcasted_iota(I32, (rows_all, blk), 0)
    col = lax.broadcasted_iota(I32, (rows_all, blk), 1)

    def rows_of(n):
        return pl.ds(pl.multiple_of(n * blk, blk), blk)

    def weighted_values(p, n):
        vn = v_ref[0, rows_of(n), :]
        p = p.astype(BF16)
        p_cat = jnp.concatenate([p[h * blk:(h + 1) * blk] for h in range(heads)], axis=1)
        v_stack = jnp.concatenate(
            [jnp.where(lane // HEAD_DIM == h, vn, jnp.zeros_like(vn)) for h in range(heads)], axis=0)
        return _dot(p_cat, v_stack)

    def per_head(x):
        return _by_head(lane, [x[h * blk:(h + 1) * blk] for h in range(heads)])

    s = _dot_nt(qs, k_ref[0, rows_of(qb), :]) + tb_ref[0, 0]
    s = jnp.where(col <= (row & (blk - 1)), s, NEG)
    m = jnp.max(s, axis=1, keepdims=True)
    p = jnp.exp(s - m)
    l = jnp.sum(p, axis=1, keepdims=True)
    acc = weighted_values(p, qb)

    def past_blocks(first, per_trip):
        def body(i, carry):
            m, l, acc = carry
            for u in range(per_trip):
                n = first + per_trip * i + u
                chosen = jnp.sum(jnp.where(blane == n, sel, 0.0), axis=1, keepdims=True) > 0.0
                s = _dot_nt(qs, k_ref[0, rows_of(n), :]) + tb_ref[0, jnp.minimum(qb - n, 2)]
                s = jnp.where(chosen, s, NEG)
                m_new = jnp.maximum(m, jnp.max(s, axis=1, keepdims=True))
                alpha = jnp.exp(m - m_new)
                p = jnp.exp(s - m_new)
                l = alpha * l + jnp.sum(p, axis=1, keepdims=True)
                acc = per_head(alpha) * acc + weighted_values(p, n)
                m = m_new
            return m, l, acc
        return body

    odd = qb & 1
    carry = lax.fori_loop(0, odd, past_blocks(0, 1), (m, l, acc))
    _, l, acc = lax.fori_loop(0, qb // 2, past_blocks(odd, 2), carry)
    o_ref[0] = (acc / per_head(l)).astype(o_ref.dtype)


def _rel_bucket(dist):
    exact = REL_BUCKETS // 2
    n = jnp.maximum(dist, 0)
    nf = jnp.maximum(n, 1).astype(F32)
    log_ratio = jnp.log(nf / exact) / math.log(REL_MAX_DIST / exact)
    large = exact + (log_ratio * (REL_BUCKETS - exact)).astype(I32)
    large = jnp.minimum(large, REL_BUCKETS - 1)
    return jnp.where(n < exact, n, large)


def _moba_bias_tables(rel_bias):
    assert REL_MAX_DIST <= MOBA_BLOCK + 1
    i = jnp.arange(MOBA_BLOCK)[:, None]
    j = jnp.arange(MOBA_BLOCK)[None, :]
    dist = jnp.arange(3)[:, None, None] * MOBA_BLOCK + (i - j)[None]
    onehot = (_rel_bucket(dist)[..., None] == jnp.arange(REL_BUCKETS)).astype(F32)
    return jnp.einsum('oijb,bh->hoij', onehot, rel_bias.astype(F32), precision=lax.Precision.HIGHEST)


def moba_mixer(qkv, rel_bias, width):
    bsz, seq, _ = qkv.shape
    blk = MOBA_BLOCK
    n_blk = seq // blk
    gw = MOBA_HEADS * HEAD_DIM
    n_grp = width // gw
    assert seq % blk == 0 and n_blk <= LANES
    tables = _moba_bias_tables(rel_bias).reshape(n_grp, MOBA_HEADS, 3, blk, blk)
    tables = tables.transpose(0, 2, 1, 3, 4).reshape(n_grp, 3, MOBA_HEADS * blk, blk)
    return pl.pallas_call(
        functools.partial(_moba_kernel, n_blk=n_blk),
        grid=(bsz, n_grp, n_blk),
        in_specs=[pl.BlockSpec((1, blk, gw), lambda b, p, i: (b, i, p)),
                  pl.BlockSpec((1, seq, gw), lambda b, p, i: (b, 0, n_grp + p)),
                  pl.BlockSpec((1, seq, gw), lambda b, p, i: (b, 0, 2 * n_grp + p)),
                  pl.BlockSpec((1, 3, MOBA_HEADS * blk, blk), lambda b, p, i: (p, 0, 0, 0))],
        out_specs=pl.BlockSpec((1, blk, gw), lambda b, p, i: (b, i, p)),
        out_shape=jax.ShapeDtypeStruct((bsz, seq, width), BF16),
        scratch_shapes=[pltpu.VMEM((LANES, gw), F32), pltpu.VMEM((LANES, 2 * gw), BF16)],
        compiler_params=_cparams(3, 48),
        name="moba_mixer",
    )(qkv, qkv, qkv, tables)


def _sb_kernel(q_ref, k_ref, v_ref, tri_ref, o_ref):
    blk = SB_BLOCK
    width = SB_HEADS * HEAD_DIM
    n_sub = blk // LANES
    rows_all = SB_HEADS * blk
    qb = pl.program_id(2)
    q = q_ref[0]
    lane = lax.broadcasted_iota(I32, (blk, width), 1)
    q_stack = jnp.concatenate(
        [jnp.where(lane // HEAD_DIM == h, q, jnp.zeros_like(q)) for h in range(SB_HEADS)],
        axis=0)
    row = lax.broadcasted_iota(I32, (rows_all, blk), 0)
    col = lax.broadcasted_iota(I32, (rows_all, blk), 1)
    past = col < (row & (blk - 1))

    def rows_of(n):
        return pl.ds(pl.multiple_of(n * blk, blk), blk)

    def logits(n):
        return _dot_nt(q_stack, k_ref[0, rows_of(n), :])

    def weights(z, carried, diagonal):
        neg_abs = lax.bitcast_convert_type(
            lax.bitcast_convert_type(z, jnp.uint32) | jnp.uint32(0x80000000), F32)
        drop = jnp.maximum(z, 0.0) + jnp.log2(1.0 + jnp.exp2(neg_abs))
        if diagonal:
            drop = jnp.where(past, drop, 0.0)
        hi32 = lax.bitcast_convert_type(
            lax.bitcast_convert_type(drop, jnp.uint32) & jnp.uint32(0xFFFF0000), F32)
        hi = hi32.astype(BF16)
        lo = (drop - hi32).astype(BF16)
        lhs = jnp.concatenate(
            [jnp.concatenate([hi[:, c * LANES:(c + 1) * LANES], lo[:, c * LANES:(c + 1) * LANES]], axis=1)
             for c in range(n_sub)], axis=0)
        sums = _dot(lhs, tri_ref[...])
        newer = carried
        from_key = [None] * n_sub
        for c in reversed(range(n_sub)):
            within = sums[c * rows_all:(c + 1) * rows_all, :LANES]
            total = sums[c * rows_all:(c + 1) * rows_all, LANES:]
            from_key[c] = within if newer is None else within + newer
            newer = total if newer is None else newer + total
        w = jnp.exp2(z - jnp.concatenate(from_key, axis=1))
        if diagonal:
            w = jnp.where(past, w, 0.0)
        w = w.astype(BF16)
        return jnp.concatenate([w[h * blk:(h + 1) * blk] for h in range(SB_HEADS)], axis=1), newer

    def weighted_values(w_cat, n):
        vn = v_ref[0, rows_of(n), :]
        v_stack = jnp.concatenate(
            [jnp.where(lane // HEAD_DIM == h, vn, jnp.zeros_like(vn)) for h in range(SB_HEADS)], axis=0)
        return _dot(w_cat, v_stack)

    w_cat, carried = weights(logits(qb), None, True)
    acc = weighted_values(w_cat, qb)

    def older_blocks(first, per_trip):
        def body(i, carry):
            acc, carried = carry
            for u in range(per_trip):
                n = first - per_trip * i - u
                w_cat, carried = weights(logits(n), carried, False)
                acc = acc + weighted_values(w_cat, n)
            return acc, carried
        return body

    odd = qb & 1
    acc, carried = lax.fori_loop(0, odd, older_blocks(qb - 1, 1), (acc, carried))
    acc, _ = lax.fori_loop(0, qb // 2, older_blocks(qb - 1 - odd, 2), (acc, carried))
    o_ref[0] = acc.astype(o_ref.dtype)


def stick_breaking_mixer(qkv, width):
    bsz, seq, _ = qkv.shape
    blk = SB_BLOCK
    gw = SB_HEADS * HEAD_DIM
    n_grp = width // gw
    tri = (jnp.arange(LANES)[:, None] >= jnp.arange(LANES)[None, :]).astype(BF16)
    tri = jnp.concatenate([tri, jnp.ones((LANES, LANES), BF16)], axis=1)
    tri = jnp.concatenate([tri, tri], axis=0)
    return pl.pallas_call(
        _sb_kernel,
        grid=(bsz, n_grp, seq // blk),
        in_specs=[pl.BlockSpec((1, blk, gw), lambda b, p, i: (b, i, p)),
                  pl.BlockSpec((1, seq, gw), lambda b, p, i: (b, 0, n_grp + p)),
                  pl.BlockSpec((1, seq, gw), lambda b, p, i: (b, 0, 2 * n_grp + p)),
                  pl.BlockSpec(tri.shape, lambda b, p, i: (0, 0))],
        out_specs=pl.BlockSpec((1, blk, gw), lambda b, p, i: (b, i, p)),
        out_shape=jax.ShapeDtypeStruct((bsz, seq, width), BF16),
        compiler_params=_cparams(3, 48),
        name="stick_breaking",
    )(qkv, qkv, qkv, tri)


def _route(h, w_ref, b_ref, idx_ref, gate_ref, rank_ref, cnt_ref, run_ref):
    tm = h.shape[0]

    @pl.when(pl.program_id(0) == 0)
    def _():
        run_ref[...] = jnp.zeros_like(run_ref)

    h_hi = h.astype(BF16)
    h_lo = (h - h_hi.astype(F32)).astype(BF16)
    logits = _dot(jnp.concatenate([h_hi, h_hi, h_lo], axis=1), w_ref[...]) + b_ref[...]
    lane = lax.broadcasted_iota(I32, (tm, LANES), 1)
    lane_f = lane.astype(F32)
    tops, hits = [], []
    for _ in range(TOP_K):
        top = jnp.max(logits, axis=1, keepdims=True)
        idx = jnp.min(jnp.where(logits == top, lane_f, float(LANES)), axis=1, keepdims=True)
        hit = lane_f == idx
        logits = jnp.where(hit, -jnp.inf, logits)
        tops.append(top)
        hits.append(hit)
    exps = [jnp.exp(t - tops[0]) for t in tops]
    denom = exps[0]
    for e in exps[1:]:
        denom = denom + e

    member = jnp.zeros((tm, LANES), F32)
    for hit in hits:
        member = jnp.where(hit, 1.0, member)
    before = (lax.broadcasted_iota(I32, (tm, tm), 1) < lax.broadcasted_iota(I32, (tm, tm), 0))
    ahead = _dot(before.astype(BF16), member.astype(BF16)) + run_ref[...]

    idx_out = jnp.zeros((tm, LANES), F32)
    gate_out = jnp.zeros((tm, LANES), F32)
    rank_out = jnp.zeros((tm, LANES), F32)
    for k in range(TOP_K):
        idx_k = jnp.sum(jnp.where(hits[k], lane_f, 0.0), axis=1, keepdims=True)
        rank_k = jnp.sum(jnp.where(hits[k], ahead, 0.0), axis=1, keepdims=True)
        idx_out = jnp.where(lane == k, idx_k, idx_out)
        gate_out = jnp.where(lane == k, exps[k] / denom, gate_out)
        rank_out = jnp.where(lane == k, rank_k, rank_out)
    idx_ref[...] = idx_out.astype(I32)
    gate_ref[...] = gate_out
    rank_ref[...] = rank_out.astype(I32)
    run_ref[...] = run_ref[...] + jnp.sum(member, axis=0, keepdims=True)
    cnt_ref[...] = run_ref[...].astype(I32)


def _router_operands(router_w, router_b):
    n_exp = router_w.shape[1]
    w_pad = jnp.pad(router_w.astype(F32), ((0, 0), (0, LANES - n_exp)))
    w_hi = w_pad.astype(BF16)
    w_lo = (w_pad - w_hi.astype(F32)).astype(BF16)
    w_split = jnp.concatenate([w_hi, w_lo, w_hi], axis=0)
    b_pad = jnp.pad(router_b.astype(F32).reshape(1, n_exp), ((0, 0), (0, LANES - n_exp)),
                    constant_values=-jnp.inf)
    return w_split, b_pad


def _expert_kernel(be_ref, tok0_ref, tok1_ref, tok_ahead_ref, h_hbm, wg_ref, bg_ref, wu_ref, bu_ref,
                   wd_ref, bd_ref, ys_hbm, buf_ref, wg_bf, wu_bf, wd_bf, y_buf, sem, out_sem):
    b = pl.program_id(0)
    last = pl.num_programs(0) - 1
    out_slot = lax.rem(b, 2)
    cur = lax.rem(b, GATHER_BUFS)
    ahead = lax.rem(b + GATHER_BUFS - 1, GATHER_BUFS)
    tiles = MOE_ROWS // SUBLANES
    pair_tiles = h_hbm.shape[1]
    d_tiles = 2 * pair_tiles

    def row_copy(tok, buf, j):
        return pltpu.make_async_copy(
            h_hbm.at[tok], buf_ref.at[buf, pl.ds(j * pair_tiles, pair_tiles)], sem.at[buf])

    def start_gather_loop(toks, buf):
        def start(t, c):
            for sub in range(SUBLANES):
                j = t * SUBLANES + sub
                row_copy(toks[0, 0, j], buf, j).start()
            return c
        lax.fori_loop(0, tiles, start, 0)

    def wait_gather(buf):
        def wait(t, c):
            for sub in range(SUBLANES):
                row_copy(0, buf, 0).wait()
            return c
        lax.fori_loop(0, tiles, wait, 0)

    def out_copy(slot, blk, c):
        return pltpu.make_async_copy(
            y_buf.at[slot, :, pl.ds(c * LANES, LANES)],
            ys_hbm.at[pl.ds(blk * MOE_ROWS, MOE_ROWS), c, :], out_sem.at[slot])

    def wait_out(slot):
        for c in range(d_tiles):
            out_copy(slot, 0, c).wait()

    @pl.when(b == 0)
    def _():
        start_gather_loop(tok0_ref, 0)
        start_gather_loop(tok1_ref, 1)

    @pl.when(b >= 2)
    def _():
        wait_out(out_slot)

    @pl.when(jnp.logical_or(b == 0, be_ref[b] != be_ref[jnp.maximum(b - 1, 0)]))
    def _():
        wg_bf[...] = wg_ref[0].astype(BF16)
        wu_bf[...] = wu_ref[0].astype(BF16)
        wd_bf[...] = wd_ref[0].astype(BF16)

    wait_gather(cur)

    h = _load_rows_from_bf16_pair_tiles(buf_ref.at[cur], MOE_ROWS, pair_tiles)
    gate = jnp.minimum(_dot(h, wg_bf[...]) + bg_ref[0], SWIGLU_LIMIT)
    up = jnp.clip(_dot(h, wu_bf[...]) + bu_ref[0], -SWIGLU_LIMIT, SWIGLU_LIMIT)
    act = (gate * jax.nn.sigmoid(SWIGLU_ALPHA * gate) * (up + 1.0)).astype(BF16)
    for j in range(MOE_ROWS):
        row_copy(tok_ahead_ref[0, 0, j], ahead, j).start(priority=j % DMA_QUEUES)
    y_buf[out_slot] = _dot(act, wd_bf[...]) + bd_ref[0]
    for c in range(d_tiles):
        out_copy(out_slot, b, c).start()

    @pl.when(b == last)
    def _():
        for back in range(1, GATHER_BUFS):
            wait_gather(lax.rem(b + back, GATHER_BUFS))
        wait_out(1 - out_slot)
        wait_out(out_slot)


def moe_experts(h_tiles, slot_tok, block_expert, layer, w_gate, b_gate, w_up, b_up, w_down, b_down):
    n, pair_tiles, _ = h_tiles.shape
    d_tiles = 2 * pair_tiles
    d = d_tiles * LANES
    n_layers, n_exp, _, d_ff = w_gate.shape
    n_blocks = block_expert.shape[0]
    assert n_blocks >= GATHER_BUFS
    toks = slot_tok.reshape(n_blocks, 1, MOE_ROWS)

    def expert4(i, be):
        return (layer, be[i], 0, 0)

    def tok_block(index):
        return pl.BlockSpec((1, 1, MOE_ROWS), index, memory_space=pltpu.SMEM)

    grid_spec = pltpu.PrefetchScalarGridSpec(
        num_scalar_prefetch=1,
        grid=(n_blocks,),
        in_specs=[tok_block(lambda i, be: (0, 0, 0)),
                  tok_block(lambda i, be: (1, 0, 0)),
                  tok_block(lambda i, be: (jnp.minimum(i + GATHER_BUFS - 1, n_blocks - 1), 0, 0)),
                  pl.BlockSpec(memory_space=pl.ANY),
                  pl.BlockSpec((None, 1, d, d_ff), expert4), pl.BlockSpec((None, 1, 1, d_ff), expert4),
                  pl.BlockSpec((None, 1, d, d_ff), expert4), pl.BlockSpec((None, 1, 1, d_ff), expert4),
                  pl.BlockSpec((None, 1, d_ff, d), expert4), pl.BlockSpec((None, 1, 1, d), expert4)],
        out_specs=pl.BlockSpec(memory_space=pl.ANY),
        scratch_shapes=[pltpu.VMEM((GATHER_BUFS, MOE_ROWS * pair_tiles, LANES), jnp.uint32),
                        pltpu.VMEM((d, d_ff), BF16), pltpu.VMEM((d, d_ff), BF16), pltpu.VMEM((d_ff, d), BF16),
                        pltpu.VMEM((2, MOE_ROWS, d), F32),
                        pltpu.SemaphoreType.DMA((GATHER_BUFS,)), pltpu.SemaphoreType.DMA((2,))])
    return pl.pallas_call(
        _expert_kernel,
        grid_spec=grid_spec,
        out_shape=jax.ShapeDtypeStruct((n_blocks * MOE_ROWS, d_tiles, LANES), F32),
        compiler_params=_cparams(1, 56),
        name="moe_experts",
    )(block_expert, toks, toks, toks, h_tiles,
      w_gate, b_gate.reshape(n_layers, n_exp, 1, d_ff), w_up, b_up.reshape(n_layers, n_exp, 1, d_ff),
      w_down, b_down.reshape(n_layers, n_exp, 1, d))


def _combine_kernel(dest0_ref, dest1_ref, dest_ahead_ref, ys_hbm, gate_ref, r_ref, p_ref, gp_ref, wp_ref,
                    wpg_ref, gf_ref, o_ref, buf_ref, sem, *, tile, final_norm):
    i = pl.program_id(0)
    last = pl.num_programs(0) - 1
    cur = lax.rem(i, GATHER_BUFS)
    ahead = lax.rem(i + GATHER_BUFS - 1, GATHER_BUFS)
    tiles = tile // SUBLANES
    d_tiles = ys_hbm.shape[1]

    def row_copy(slot, buf, k, j):
        return pltpu.make_async_copy(
            ys_hbm.at[slot], buf_ref.at[buf, k, pl.ds(j * d_tiles, d_tiles)], sem.at[buf])

    def start_gather_loop(dests, buf):
        def start(t, c):
            for sub in range(SUBLANES):
                for k in range(TOP_K):
                    j = t * SUBLANES + sub
                    row_copy(dests[0, 0, j * TOP_K + k], buf, k, j).start()
            return c
        lax.fori_loop(0, tiles, start, 0)

    def wait_gather(buf):
        def wait(t, c):
            for sub in range(SUBLANES):
                for k in range(TOP_K):
                    row_copy(0, buf, k, 0).wait()
            return c
        lax.fori_loop(0, tiles, wait, 0)

    @pl.when(i == 0)
    def _():
        start_gather_loop(dest0_ref, 0)
        start_gather_loop(dest1_ref, 1)

    wait_gather(cur)
    gates = gate_ref[...]
    r = r_ref[...]
    for k in range(TOP_K):
        r = r + _load_rows_from_tiles(buf_ref.at[cur, k], tile, d_tiles) * gates[:, k:k + 1]
    h = _rms(r, gp_ref[...]).astype(BF16)
    gate = jax.nn.sigmoid(_dot(h, wpg_ref[...]))
    r = r + _dot(p_ref[...].astype(BF16), wp_ref[...]) * gate
    if final_norm:
        r = _rms(r, gf_ref[...])
    o_ref[...] = r
    for j in range(tile):
        for k in range(TOP_K):
            row_copy(dest_ahead_ref[0, 0, j * TOP_K + k], ahead, k, j).start(priority=k % DMA_QUEUES)

    @pl.when(i == last)
    def _():
        for back in range(1, GATHER_BUFS):
            wait_gather(lax.rem(i + back, GATHER_BUFS))


def moe_combine_embed(ys, dest, gates, r, p, g_ple, w_ple, w_ple_gate, g_final, final_norm, tile=256):
    n, d = r.shape
    pd = p.shape[1]
    n_tiles = n // tile
    assert n_tiles >= GATHER_BUFS
    dest_tiles = dest.reshape(n_tiles, 1, tile * TOP_K)

    def dest_block(index):
        return pl.BlockSpec((1, 1, tile * TOP_K), index, memory_space=pltpu.SMEM)

    return pl.pallas_call(
        functools.partial(_combine_kernel, tile=tile, final_norm=final_norm),
        grid=(n_tiles,),
        in_specs=[dest_block(lambda i: (0, 0, 0)),
                  dest_block(lambda i: (1, 0, 0)),
                  dest_block(lambda i: (jnp.minimum(i + GATHER_BUFS - 1, n_tiles - 1), 0, 0)),
                  pl.BlockSpec(memory_space=pl.ANY),
                  pl.BlockSpec((tile, LANES), lambda i: (i, 0)),
                  pl.BlockSpec((tile, d), lambda i: (i, 0)),
                  pl.BlockSpec((tile, pd), lambda i: (i, 0)),
                  pl.BlockSpec((1, d), lambda i: (0, 0)),
                  pl.BlockSpec((pd, d), lambda i: (0, 0)),
                  pl.BlockSpec((d, d), lambda i: (0, 0)),
                  pl.BlockSpec((1, d), lambda i: (0, 0))],
        out_specs=pl.BlockSpec((tile, d), lambda i: (i, 0)),
        out_shape=jax.ShapeDtypeStruct((n, d), F32),
        scratch_shapes=[pltpu.VMEM((GATHER_BUFS, TOP_K, tile * (d // LANES), LANES), F32),
                        pltpu.SemaphoreType.DMA((GATHER_BUFS,))],
        compiler_params=_cparams(1, 48),
        name="moe_combine_embed",
    )(dest_tiles, dest_tiles, dest_tiles, ys, gates, r, p, g_ple.reshape(1, d), w_ple, w_ple_gate,
      g_final.reshape(1, d))


def _slot_layout(idx, rank, counts, n_blocks):
    n_exp = counts.shape[0]
    padded = (counts + MOE_ROWS - 1) // MOE_ROWS * MOE_ROWS
    pad_end = jnp.cumsum(padded)
    pad_start = pad_end - padded
    cnt_start = jnp.cumsum(counts) - counts
    start_of = jnp.sum(jnp.where(idx[..., None] == jnp.arange(n_exp), pad_start, 0), axis=-1)
    dest = (start_of + rank).astype(I32).reshape(-1)
    block_first_row = jnp.arange(n_blocks) * MOE_ROWS
    block_expert = jnp.minimum(
        jnp.sum(pad_end[None, :] <= block_first_row[:, None], axis=1), n_exp - 1).astype(I32)
    sorted_tok = (jnp.argsort(dest) // TOP_K).astype(I32)
    in_expert = block_first_row[:, None] + jnp.arange(MOE_ROWS)[None, :] - pad_start[block_expert][:, None]
    real = in_expert < counts[block_expert][:, None]
    compact = jnp.clip(in_expert + cnt_start[block_expert][:, None], 0, dest.shape[0] - 1)
    slot_tok = jnp.where(real, sorted_tok[compact], 0).astype(I32)
    return dest, slot_tok, block_expert


def moe_and_embed(r, h_tiles, routing, p, layer, w_gate, b_gate, w_up, b_up, w_down, b_down,
                  g_ple, w_ple, w_ple_gate, g_final, final_norm):
    n, _ = r.shape
    idx, gates, rank, counts = routing
    n_blocks = n * TOP_K // MOE_ROWS + counts.shape[0]
    dest, slot_tok, block_expert = _slot_layout(idx, rank, counts, n_blocks)
    ys = moe_experts(h_tiles, slot_tok, block_expert, layer, w_gate.astype(F32), b_gate.astype(F32),
                     w_up.astype(F32), b_up.astype(F32), w_down.astype(F32), b_down.astype(F32))
    return moe_combine_embed(ys, dest, gates, r, p, g_ple, w_ple.astype(BF16), w_ple_gate.astype(BF16),
                             g_final, final_norm)


def kernel(x, p, norm_mix, norm_ffn, norm_ple, norm_final, w_in_ab, ssm_a_re, ssm_a_im, ssm_b_re, ssm_b_im, ssm_c_re, ssm_c_im, ssm_d, ssm_log_step, glu_w, glu_b, w_out_ab, rel_bias, w_in_c, w_out_c, router_w, router_b, w_gate, b_gate, w_up, b_up, w_down, b_down, w_ple, w_ple_gate):
    bsz, seq, d = x.shape
    n = bsz * seq
    depth = p.shape[0]
    assert bsz == SUBLANES
    r = x.reshape(n, d).astype(F32)
    for i in range(depth):
        j = i // 2
        if i % 2 == 0:
            ssm_w = ssm_d.shape[1] * ssm_d.shape[2]
            moba_w = (w_in_ab.shape[2] - ssm_w) // 3
            u, qkv = norm_matmul(r, norm_mix[i], w_in_ab[j].astype(BF16),
                                 (ssm_w, 3 * moba_w), (F32, BF16))
            a_mat, b_mat, c_mat = _s5_params(ssm_a_re[j], ssm_a_im[j], ssm_b_re[j], ssm_b_im[j],
                                             ssm_c_re[j], ssm_c_im[j], ssm_log_step[j])
            u_tb = u.reshape(bsz, seq, ssm_w).transpose(1, 0, 2).reshape(n, ssm_w)
            y_a = s5_mixer(u_tb, a_mat, b_mat, c_mat, ssm_d[j].reshape(1, ssm_w).astype(F32),
                           glu_w[j].astype(BF16), glu_b[j].reshape(1, ssm_w).astype(F32))
            y_a = y_a.reshape(seq, bsz, ssm_w).transpose(1, 0, 2).reshape(n, ssm_w)
            y_b = moba_mixer(qkv.reshape(bsz, seq, 3 * moba_w), rel_bias, moba_w).reshape(n, moba_w)
            w_out = w_out_ab[j].astype(BF16)
            r, h_tiles, routing = matmul_residual(r, norm_ffn[i], router_w[i], router_b[i],
                                                  [y_a, y_b], [w_out[:ssm_w], w_out[ssm_w:]])
        else:
            q_scale = jnp.where(jnp.arange(3 * d) < d, HEAD_DIM ** -0.5 * math.log2(math.e), 1.0)
            w_in = (w_in_c[j].astype(F32) * q_scale).astype(BF16)
            (qkv,) = norm_matmul(r, norm_mix[i], w_in, (3 * d,), (BF16,))
            y_c = stick_breaking_mixer(qkv.reshape(bsz, seq, 3 * d), d).reshape(n, d)
            r, h_tiles, routing = matmul_residual(r, norm_ffn[i], router_w[i], router_b[i],
                                                  [y_c], [w_out_c[j].astype(BF16)])
        r = moe_and_embed(r, h_tiles, routing, p[i].reshape(n, -1),
                          i, w_gate, b_gate, w_up, b_up, w_down, b_down,
                          norm_ple[i], w_ple[i], w_ple_gate[i], norm_final, i == depth - 1)
    return r.reshape(bsz, seq, d).astype(x.dtype)
```
